```python
import math
import jax
import jax.numpy as jnp
from jax import lax
import numpy as np

D_MODEL = 1024
BATCH = 16
SEQ = 2048
DEPTH = 2
DEC_BATCH = 128
DEC_SEQ = 1
PAST_LEN = 8192
PAGE_SIZE = 128

MLA_HEADS = 8
MLA_NOPE = 64
MLA_ROPE = 32
MLA_V = 64
Q_LORA = 256
KV_LORA = 256
ROPE_THETA = 10000.0
MLA_SCALE = (MLA_NOPE + MLA_ROPE) ** -0.5
Q_BLOCK = 128
MLA_IN = Q_LORA + KV_LORA + MLA_ROPE
RWKV_HEADS = 8
RWKV_HEAD = 64
RWKV_DIM = RWKV_HEADS * RWKV_HEAD
DECAY_LORA = 64
AAA_LORA = 64
GATE_LORA = 128
RWKV_IN = 3 * RWKV_DIM + DECAY_LORA + AAA_LORA + GATE_LORA
RWKV_LN_EPS = 64e-5
SSM_HEADS = 8
SSM_HEAD = 64
SSM_DIM = SSM_HEADS * SSM_HEAD
SSM_GROUPS = 2
SSM_STATE = 128
SSM_CONV = 4
SSM_CHUNK = 128
SSM_CONV_DIM = SSM_DIM + 2 * SSM_GROUPS * SSM_STATE
SSM_IN = SSM_DIM + SSM_CONV_DIM + SSM_HEADS
N_BRANCH = 3
BRANCH_DIM = 512
IN_DIM = MLA_IN + RWKV_IN + SSM_IN + N_BRANCH * D_MODEL
D_FF = 2816
N_EXPERTS = 8
TOP_K = 2
D_FF_EXPERT = 3584
EPS = 1e-6

kernel_name = 'hybrid_mla_rwkv7_mamba2_adaln_decode_step'


def _split(x, sizes):
    return jnp.split(x, list(np.cumsum(sizes)[:-1]), axis=-1)


def rmsnorm(x, w):
    x32 = x.astype(jnp.float32)
    y = x32 * lax.rsqrt(jnp.mean(x32 * x32, axis=-1, keepdims=True) + EPS)
    return (y * w.astype(jnp.float32)).astype(x.dtype)


def rope(x, pos):
    half = x.shape[-1] // 2
    freq = ROPE_THETA ** (-jnp.arange(half, dtype=jnp.float32) / half)
    ang = pos.astype(jnp.float32)[:, None] * freq[None, :]
    ang = ang.reshape((1, pos.shape[0]) + (1,) * (x.ndim - 3) + (half,))
    cos, sin = jnp.cos(ang), jnp.sin(ang)
    x32 = x.astype(jnp.float32)
    x1, x2 = x32[..., :half], x32[..., half:]
    return jnp.concatenate([x1 * cos - x2 * sin, x2 * cos + x1 * sin], axis=-1).astype(x.dtype)


def mla_attend_causal(q_abs, q_pe, ckv, kpe):
    b, s, h, c = q_abs.shape
    nb = s // Q_BLOCK
    qa = jnp.moveaxis(q_abs.reshape(b, nb, Q_BLOCK, h, c), 1, 0)
    qp = jnp.moveaxis(q_pe.reshape(b, nb, Q_BLOCK, h, MLA_ROPE), 1, 0)
    kpos = jnp.arange(s)

    def block(args):
        qa_i, qp_i, i = args
        sc = (jnp.einsum('bqhc,bkc->bhqk', qa_i, ckv) + jnp.einsum('bqhr,bkr->bhqk', qp_i, kpe)).astype(jnp.float32) * MLA_SCALE
        qpos = i * Q_BLOCK + jnp.arange(Q_BLOCK)
        sc = jnp.where(kpos[None, :] <= qpos[:, None], sc, -jnp.inf)
        pr = jax.nn.softmax(sc, axis=-1).astype(ckv.dtype)
        return jnp.einsum('bhqk,bkc->bqhc', pr, ckv)

    o = lax.map(block, (qa, qp, jnp.arange(nb)))
    return jnp.moveaxis(o, 0, 1).reshape(b, s, h, c)


def paged_attend(ckv_past, kpe_past):
    def attend(q_abs, q_pe, ckv, kpe):
        s = q_abs.shape[1]
        n_past = ckv_past.shape[1]
        sc_past = jnp.einsum('bqhc,bkc->bhqk', q_abs, ckv_past) + jnp.einsum('bqhr,bkr->bhqk', q_pe, kpe_past)
        sc_new = jnp.einsum('bqhc,bkc->bhqk', q_abs, ckv) + jnp.einsum('bqhr,bkr->bhqk', q_pe, kpe)
        causal = jnp.arange(s)[None, :] <= jnp.arange(s)[:, None]
        sc_new = jnp.where(causal, sc_new.astype(jnp.float32), -jnp.inf)
        sc = jnp.concatenate([sc_past.astype(jnp.float32), sc_new], axis=-1) * MLA_SCALE
        pr = jax.nn.softmax(sc, axis=-1).astype(ckv.dtype)
        return (jnp.einsum('bhqk,bkc->bqhc', pr[..., :n_past], ckv_past)
                + jnp.einsum('bhqk,bkc->bqhc', pr[..., n_past:], ckv))
    return attend


def mla_branch(q_c, kv_c, kr, pos, p, attend):
    b, s, _ = q_c.shape
    q = (rmsnorm(q_c, p['q_norm']) @ p['w_uq']).reshape(b, s, MLA_HEADS, MLA_NOPE + MLA_ROPE)
    q_nope, q_pe = q[..., :MLA_NOPE], rope(q[..., MLA_NOPE:], pos)
    ckv = rmsnorm(kv_c, p['kv_norm'])
    kpe = rope(kr, pos)
    q_abs = jnp.einsum('bshd,chd->bshc', q_nope, p['w_uk'])
    o_lat = attend(q_abs, q_pe, ckv, kpe)
    out = jnp.einsum('bshc,chv->bshv', o_lat, p['w_uv']).reshape(b, s, MLA_HEADS * MLA_V)
    return out, ckv, kpe


def rwkv_branch(rw, shift0, wkv0, p):
    b, s, _ = rw.shape
    f32 = jnp.float32
    prev = jnp.concatenate([shift0[:, None, :], rw[:, :-1]], axis=1)
    xs = rw + (prev - rw) * p['rwkv_mu']
    r, k, v, wl, al, gl = _split(xs, [RWKV_DIM, RWKV_DIM, RWKV_DIM, DECAY_LORA, AAA_LORA, GATE_LORA])
    w = -jax.nn.softplus(-(p['rwkv_w0'] + jnp.tanh(wl) @ p['rwkv_w2']).astype(f32)) - 0.5
    decay = jnp.exp(-jnp.exp(w))
    a = jax.nn.sigmoid((p['rwkv_a0'] + al @ p['rwkv_a2']).astype(f32))
    g = jax.nn.sigmoid(gl) @ p['rwkv_g2']

    def heads(t):
        return t.astype(f32).reshape(b, s, RWKV_HEADS, RWKV_HEAD)

    r_h, v_h, a_h, w_h = heads(r), heads(v), heads(a), heads(decay)
    kk = heads(k * p['rwkv_k_k'])
    kk = kk * lax.rsqrt(jnp.maximum(jnp.sum(kk * kk, axis=-1, keepdims=True), 1e-24))
    k_a = p['rwkv_k_a'].astype(f32).reshape(RWKV_HEADS, RWKV_HEAD)
    k_h = heads(k) * (1.0 + (a_h - 1.0) * k_a)

    def step(S, inp):
        r_t, w_t, k_t, v_t, kk_t, a_t = inp
        sa = jnp.einsum('bhij,bhj->bhi', S, -kk_t)
        S = S * w_t[:, :, None, :] + sa[..., None] * (kk_t * a_t)[:, :, None, :] + v_t[..., None] * k_t[:, :, None, :]
        return S, jnp.einsum('bhij,bhj->bhi', S, r_t)

    seq = tuple(jnp.moveaxis(t, 1, 0) for t in (r_h, w_h, k_h, v_h, kk, a_h))
    S1, y = lax.scan(step, wkv0.astype(f32), seq)
    y = jnp.moveaxis(y, 0, 1)
    mu = jnp.mean(y, axis=-1, keepdims=True)
    var = jnp.mean(jnp.square(y - mu), axis=-1, keepdims=True)
    y = (y - mu) * lax.rsqrt(var + RWKV_LN_EPS)
    y = (y * p['rwkv_ln_w'].astype(f32).reshape(RWKV_HEADS, RWKV_HEAD)
         + p['rwkv_ln_b'].astype(f32).reshape(RWKV_HEADS, RWKV_HEAD))
    y = y + jnp.sum(r_h * k_h * p['rwkv_r_k'].astype(f32), axis=-1, keepdims=True) * v_h
    out = y.reshape(b, s, RWKV_DIM).astype(rw.dtype) * g
    return out, rw[:, -1], S1.astype(rw.dtype)


def ssd(x, dt, a, bm, cm, h0):
    b, l, h, pdim = x.shape
    cl = min(SSM_CHUNK, l)
    pad = (-l) % cl
    if pad:
        x = jnp.pad(x, ((0, 0), (0, pad), (0, 0), (0, 0)))
        dt = jnp.pad(dt, ((0, 0), (0, pad), (0, 0)))
        bm = jnp.pad(bm, ((0, 0), (0, pad), (0, 0), (0, 0)))
        cm = jnp.pad(cm, ((0, 0), (0, pad), (0, 0), (0, 0)))
    nc = (l + pad) // cl
    rep = h // SSM_GROUPS
    Bh = jnp.repeat(bm, rep, axis=2).reshape(b, nc, cl, h, SSM_STATE)
    Ch = jnp.repeat(cm, rep, axis=2).reshape(b, nc, cl, h, SSM_STATE)
    X = (x * dt[..., None]).reshape(b, nc, cl, h, pdim)
    adt = jnp.transpose((dt * a).reshape(b, nc, cl, h), (0, 3, 1, 2))
    cs = jnp.cumsum(adt, axis=-1)
    seg = cs[..., :, None] - cs[..., None, :]
    causal = jnp.tril(jnp.ones((cl, cl), dtype=bool))
    Lmat = jnp.where(causal, jnp.exp(jnp.where(causal, seg, 0.0)), 0.0)
    y_diag = jnp.einsum('bclhn,bcshn,bhcls,bcshp->bclhp', Ch, Bh, Lmat, X)
    decay_states = jnp.exp(cs[..., -1:] - cs)
    states = jnp.einsum('bclhn,bhcl,bclhp->bchpn', Bh, decay_states, X)
    chunk_decay = jnp.exp(cs[..., -1])

    def step(hc, inp):
        st, dec = inp
        return hc * dec[:, :, None, None] + st, hc

    hT, h_prev = lax.scan(step, h0, (jnp.moveaxis(states, 1, 0), jnp.moveaxis(chunk_decay, 2, 0)))
    h_prev = jnp.moveaxis(h_prev, 0, 1)
    y_off = jnp.einsum('bclhn,bchpn,bhcl->bclhp', Ch, h_prev, jnp.exp(cs))
    y = (y_diag + y_off).reshape(b, nc * cl, h, pdim)[:, :l]
    return y, hT


def ssm_branch(z, xbc, dt_raw, conv0, ssm0, p):
    b, s, _ = xbc.shape
    f32 = jnp.float32
    full = jnp.concatenate([conv0, xbc], axis=1)
    conv = p['conv_b'] + sum(full[:, i:i + s] * p['conv_w'][i] for i in range(SSM_CONV))
    conv1 = full[:, -(SSM_CONV - 1):]
    xa, bm, cm = _split(jax.nn.silu(conv), [SSM_DIM, SSM_GROUPS * SSM_STATE, SSM_GROUPS * SSM_STATE])
    x = xa.astype(f32).reshape(b, s, SSM_HEADS, SSM_HEAD)
    dt = jax.nn.softplus((dt_raw + p['dt_bias']).astype(f32))
    a = -jnp.exp(p['a_log'].astype(f32))
    y, ssm1 = ssd(x, dt, a, bm.astype(f32).reshape(b, s, SSM_GROUPS, SSM_STATE),
                  cm.astype(f32).reshape(b, s, SSM_GROUPS, SSM_STATE), ssm0.astype(f32))
    y = y + p['d_skip'].astype(f32)[:, None] * x
    y = y.reshape(b, s, SSM_DIM) * jax.nn.silu(z.astype(f32))
    yg = y.reshape(b, s, SSM_GROUPS, SSM_DIM // SSM_GROUPS)
    yg = yg * lax.rsqrt(jnp.mean(yg * yg, axis=-1, keepdims=True) + EPS)
    out = (yg.reshape(b, s, SSM_DIM) * p['ssm_norm'].astype(f32)).astype(xbc.dtype)
    return out, conv1, ssm1.astype(xbc.dtype)


def token_mixers(h, pos, p, attend, shift0, wkv0, conv0, ssm0):
    b, s, _ = h.shape
    proj = h @ p['w_in']
    q_c, kv_c, kr, rw, z, xbc, dt_raw, gate_raw = _split(
        proj, [Q_LORA, KV_LORA, MLA_ROPE, RWKV_IN, SSM_DIM, SSM_CONV_DIM, SSM_HEADS, N_BRANCH * D_MODEL])
    a_out, ckv, kpe = mla_branch(q_c, kv_c, kr, pos, p, attend)
    r_out, shift1, wkv1 = rwkv_branch(rw, shift0, wkv0, p)
    s_out, conv1, ssm1 = ssm_branch(z, xbc, dt_raw, conv0, ssm0, p)
    branches = jnp.stack([a_out, r_out, s_out], axis=2)
    per_branch = jnp.einsum('bsrw,rwd->bsrd', branches, p['w_branch'])
    gates = jax.nn.sigmoid(gate_raw.reshape(b, s, N_BRANCH, D_MODEL))
    mix = jnp.sum(gates * per_branch, axis=2) @ p['w_out']
    return mix, ckv, kpe, shift1, wkv1, conv1, ssm1


def swiglu(h, w1, w3, w2):
    return (jax.nn.silu(h @ w1) * (h @ w3)) @ w2


def moe_ffn(h, router, router_b, w1, w3, w2):
    logits = (h @ router + router_b).astype(jnp.float32)
    top_v, top_i = lax.top_k(logits, TOP_K)
    top_w = jax.nn.softmax(top_v, axis=-1)
    gate = jnp.sum(jax.nn.one_hot(top_i, N_EXPERTS, dtype=jnp.float32) * top_w[..., None], axis=-2).astype(h.dtype)
    out = jnp.zeros_like(h)
    for e in range(N_EXPERTS):
        out = out + gate[..., e:e + 1] * swiglu(h, w1[e], w3[e], w2[e])
    return out


def setup_inputs(seed: int = 0) -> dict:
    key = jax.random.key(seed)
    ks = jax.random.split(key, 64)
    cnt = [0]

    def nk():
        cnt[0] += 1
        return ks[cnt[0] - 1]

    def nrm(shape, scale):
        return jax.random.normal(nk(), shape, jnp.float32) * scale

    def unif(shape, lo, hi):
        return jax.random.uniform(nk(), shape, jnp.float32, lo, hi)

    n_pages = PAST_LEN // PAGE_SIZE
    n_phys = (DEC_BATCH * n_pages * 5) // 4
    n_dense, n_moe = (DEPTH + 1) // 2, DEPTH // 2
    page_table = jax.random.permutation(nk(), n_phys)[:DEC_BATCH * n_pages].reshape(DEC_BATCH, n_pages).astype(jnp.int32)
    dt0 = jnp.exp(unif((DEPTH, SSM_HEADS), math.log(1e-3), math.log(1e-1)))
    return {
        'x_prompt': nrm((BATCH, SEQ, D_MODEL), 1.0),
        'x_sample': nrm((DEC_BATCH, DEC_SEQ, D_MODEL), 1.0),
        'cache_ckv': nrm((DEPTH, n_phys, PAGE_SIZE, KV_LORA), 1.0),
        'cache_kpe': nrm((DEPTH, n_phys, PAGE_SIZE, MLA_ROPE), 1.0),
        'state_rwkv_shift': nrm((DEPTH, DEC_BATCH, RWKV_IN), 1.0),
        'state_rwkv_wkv': nrm((DEPTH, DEC_BATCH, RWKV_HEADS, RWKV_HEAD, RWKV_HEAD), 0.3),
        'state_ssm_conv': nrm((DEPTH, DEC_BATCH, SSM_CONV - 1, SSM_CONV_DIM), 1.0),
        'state_ssm': nrm((DEPTH, DEC_BATCH, SSM_HEADS, SSM_HEAD, SSM_STATE), 0.3),
        'page_table': page_table,
        'c_prompt': nrm((BATCH, D_MODEL), 1.0),
        'c_sample': nrm((DEC_BATCH, D_MODEL), 1.0),
        'w_ada': nrm((DEPTH, D_MODEL, 6 * D_MODEL), 0.5 * D_MODEL ** -0.5),
        'b_ada': nrm((DEPTH, 6 * D_MODEL), 0.02),
        'norm_attn': 1.0 + nrm((DEPTH, D_MODEL), 0.05),
        'norm_ffn': 1.0 + nrm((DEPTH, D_MODEL), 0.05),
        'norm_final': 1.0 + nrm((D_MODEL,), 0.05),
        'w_in': nrm((DEPTH, D_MODEL, IN_DIM), D_MODEL ** -0.5),
        'mla_q_norm': 1.0 + nrm((DEPTH, Q_LORA), 0.05),
        'mla_w_uq': nrm((DEPTH, Q_LORA, MLA_HEADS * (MLA_NOPE + MLA_ROPE)), Q_LORA ** -0.5),
        'mla_kv_norm': 1.0 + nrm((DEPTH, KV_LORA), 0.05),
        'mla_w_uk': nrm((DEPTH, KV_LORA, MLA_HEADS, MLA_NOPE), KV_LORA ** -0.5),
        'mla_w_uv': nrm((DEPTH, KV_LORA, MLA_HEADS, MLA_V), KV_LORA ** -0.5),
        'rwkv_mu': unif((DEPTH, RWKV_IN), 0.0, 1.0),
        'rwkv_w0': nrm((DEPTH, RWKV_DIM), 0.5),
        'rwkv_w2': nrm((DEPTH, DECAY_LORA, RWKV_DIM), 0.1),
        'rwkv_a0': nrm((DEPTH, RWKV_DIM), 0.1),
        'rwkv_a2': nrm((DEPTH, AAA_LORA, RWKV_DIM), 0.1),
        'rwkv_g2': nrm((DEPTH, GATE_LORA, RWKV_DIM), GATE_LORA ** -0.5),
        'rwkv_k_k': 0.85 + nrm((DEPTH, RWKV_DIM), 0.05),
        'rwkv_k_a': 1.0 + nrm((DEPTH, RWKV_DIM), 0.05),
        'rwkv_r_k': nrm((DEPTH, RWKV_HEADS, RWKV_HEAD), 0.1),
        'rwkv_ln_w': 1.0 + nrm((DEPTH, RWKV_DIM), 0.05),
        'rwkv_ln_b': nrm((DEPTH, RWKV_DIM), 0.02),
        'ssm_conv_w': nrm((DEPTH, SSM_CONV, SSM_CONV_DIM), SSM_CONV ** -0.5),
        'ssm_conv_b': nrm((DEPTH, SSM_CONV_DIM), 0.02),
        'ssm_dt_bias': dt0 + jnp.log(-jnp.expm1(-dt0)),
        'ssm_a_log': jnp.log(unif((DEPTH, SSM_HEADS), 1.0, 16.0)),
        'ssm_d': 1.0 + nrm((DEPTH, SSM_HEADS), 0.1),
        'ssm_norm': 1.0 + nrm((DEPTH, SSM_DIM), 0.05),
        'w_branch': nrm((DEPTH, N_BRANCH, BRANCH_DIM, D_MODEL), BRANCH_DIM ** -0.5),
        'w_out': nrm((DEPTH, D_MODEL, D_MODEL), D_MODEL ** -0.5),
        'ffn_w1': nrm((n_dense, D_MODEL, D_FF), D_MODEL ** -0.5),
        'ffn_w3': nrm((n_dense, D_MODEL, D_FF), D_MODEL ** -0.5),
        'ffn_w2': nrm((n_dense, D_FF, D_MODEL), D_FF ** -0.5),
        'moe_router': nrm((n_moe, D_MODEL, N_EXPERTS), D_MODEL ** -0.5),
        'moe_router_b': nrm((n_moe, N_EXPERTS), 0.01),
        'moe_w1': nrm((n_moe, N_EXPERTS, D_MODEL, D_FF_EXPERT), D_MODEL ** -0.5),
        'moe_w3': nrm((n_moe, N_EXPERTS, D_MODEL, D_FF_EXPERT), D_MODEL ** -0.5),
        'moe_w2': nrm((n_moe, N_EXPERTS, D_FF_EXPERT, D_MODEL), D_FF_EXPERT ** -0.5),
    }


def reference(x_prompt, x_sample, cache_ckv, cache_kpe, state_rwkv_shift, state_rwkv_wkv, state_ssm_conv, state_ssm,
              page_table, c_prompt, c_sample, w_ada, b_ada, norm_attn, norm_ffn, norm_final, w_in,
              mla_q_norm, mla_w_uq, mla_kv_norm, mla_w_uk, mla_w_uv,
              rwkv_mu, rwkv_w0, rwkv_w2, rwkv_a0, rwkv_a2, rwkv_g2, rwkv_k_k, rwkv_k_a, rwkv_r_k, rwkv_ln_w, rwkv_ln_b,
              ssm_conv_w, ssm_conv_b, ssm_dt_bias, ssm_a_log, ssm_d, ssm_norm, w_branch, w_out,
              ffn_w1, ffn_w3, ffn_w2, moe_router, moe_router_b, moe_w1, moe_w3, moe_w2):

    def trunk(x, c, pos, paged, shift0, wkv0, conv0, ssm0):
        outs = [[] for _ in range(6)]
        for l in range(DEPTH):
            ada = (jax.nn.silu(c) @ w_ada[l] + b_ada[l])[:, None, :]
            sh_a, sc_a, g_a, sh_f, sc_f, g_f = jnp.split(ada, 6, axis=-1)
            h = rmsnorm(x, norm_attn[l]) * (1.0 + sc_a) + sh_a
            if paged is None:
                attend = mla_attend_causal
            else:
                ck, kp, pt = paged
                attend = paged_attend(ck[l][pt].reshape(pt.shape[0], -1, KV_LORA),
                                      kp[l][pt].reshape(pt.shape[0], -1, MLA_ROPE))
            p = dict(w_in=w_in[l], q_norm=mla_q_norm[l], w_uq=mla_w_uq[l], kv_norm=mla_kv_norm[l],
                     w_uk=mla_w_uk[l], w_uv=mla_w_uv[l], rwkv_mu=rwkv_mu[l], rwkv_w0=rwkv_w0[l],
                     rwkv_w2=rwkv_w2[l], rwkv_a0=rwkv_a0[l], rwkv_a2=rwkv_a2[l], rwkv_g2=rwkv_g2[l],
                     rwkv_k_k=rwkv_k_k[l], rwkv_k_a=rwkv_k_a[l], rwkv_r_k=rwkv_r_k[l], rwkv_ln_w=rwkv_ln_w[l],
                     rwkv_ln_b=rwkv_ln_b[l], conv_w=ssm_conv_w[l], conv_b=ssm_conv_b[l], dt_bias=ssm_dt_bias[l],
                     a_log=ssm_a_log[l], d_skip=ssm_d[l], ssm_norm=ssm_norm[l], w_branch=w_branch[l], w_out=w_out[l])
            mix, *st = token_mixers(h, pos, p, attend, shift0[l], wkv0[l], conv0[l], ssm0[l])
            x = x + g_a * mix
            h = rmsnorm(x, norm_ffn[l]) * (1.0 + sc_f) + sh_f
            if l % 2 == 0:
                f = swiglu(h, ffn_w1[l // 2], ffn_w3[l // 2], ffn_w2[l // 2])
            else:
                f = moe_ffn(h, moe_router[l // 2], moe_router_b[l // 2], moe_w1[l // 2], moe_w3[l // 2], moe_w2[l // 2])
            x = x + g_f * f
            for lst, v in zip(outs, st):
                lst.append(v)
        return rmsnorm(x, norm_final), [jnp.stack(v) for v in outs]

    bp, sp = x_prompt.shape[0], x_prompt.shape[1]
    dt = x_prompt.dtype
    pos_p = jnp.arange(sp, dtype=jnp.int32)
    y_prompt, (p_ckv, p_kpe, p_shift, p_wkv, p_conv, p_ssm) = trunk(
        x_prompt, c_prompt, pos_p, None,
        jnp.zeros((DEPTH, bp, RWKV_IN), dt),
        jnp.zeros((DEPTH, bp, RWKV_HEADS, RWKV_HEAD, RWKV_HEAD), dt),
        jnp.zeros((DEPTH, bp, SSM_CONV - 1, SSM_CONV_DIM), dt),
        jnp.zeros((DEPTH, bp, SSM_HEADS, SSM_HEAD, SSM_STATE), dt))
    p_ckv = p_ckv.reshape(DEPTH, bp * sp // PAGE_SIZE, PAGE_SIZE, KV_LORA)
    p_kpe = p_kpe.reshape(DEPTH, bp * sp // PAGE_SIZE, PAGE_SIZE, MLA_ROPE)

    past_len = page_table.shape[1] * PAGE_SIZE
    pos_s = past_len + jnp.arange(x_sample.shape[1], dtype=jnp.int32)
    y_sample, (s_ckv, s_kpe, s_shift, s_wkv, s_conv, s_ssm) = trunk(
        x_sample, c_sample, pos_s, (cache_ckv, cache_kpe, page_table),
        state_rwkv_shift, state_rwkv_wkv, state_ssm_conv, state_ssm)
    return (y_prompt, y_sample, p_ckv, p_kpe, p_shift, p_wkv, p_conv, p_ssm,
            s_ckv, s_kpe, s_shift, s_wkv, s_conv, s_ssm)
```

```python
import functools

import jax
import jax.numpy as jnp
from jax import lax
from jax.experimental import pallas as pl
from jax.experimental.pallas import tpu as pltpu

F32 = jnp.float32
BF16 = jnp.bfloat16

D_MODEL = 1024
DEPTH = 2
PAGE = 128
H = 8
NOPE = 64
ROPE = 32
VD = 64
QL = 256
KVL = 256
ROPE_THETA = 10000.0
MLA_SCALE = (NOPE + ROPE) ** -0.5
QK = 384
RDIM = 512
RHD = 64
DECAY_LORA = 64
AAA_LORA = 64
GATE_LORA = 128
RWKV_IN = 3 * RDIM + DECAY_LORA + AAA_LORA + GATE_LORA
RWKV_LN_EPS = 64e-5
SHEADS = 8
SP = 64
SDIM = 512
SGROUPS = 2
SN = 128
SCONV = 4
SCHUNK = 128
SCD = SDIM + 2 * SGROUPS * SN
NBRANCH = 3
NE = 8
TOPK = 2
EPS = 1e-6
SEG_A = 768
DT_TILE = 4
DT_LANE = 32
SEG_WIDTHS = (SEG_A, RWKV_IN, SDIM, SCD, NBRANCH * D_MODEL)
W_IN_COLS = sum(SEG_WIDTHS)
VMEM_LIMIT = 56 * 1024 * 1024


def _cparams(*sem):
    return pltpu.CompilerParams(dimension_semantics=sem, vmem_limit_bytes=VMEM_LIMIT)


def _dot(a, b):
    return jnp.dot(a, b, preferred_element_type=F32)


def _dot_nt(a, b):
    return lax.dot_general(a, b, (((1,), (1,)), ((), ())), preferred_element_type=F32)


def _hi_lo(x):
    hi = x.astype(BF16)
    return hi, (x.astype(F32) - hi.astype(F32)).astype(BF16)


def _dot3(a, w, dot=_dot):
    a_hi, a_lo = _hi_lo(a)
    w_hi, w_lo = _hi_lo(w)
    return dot(a_hi, w_hi) + dot(a_lo, w_hi) + dot(a_hi, w_lo)


def _mm(a, w, precise):
    return _dot3(a, w) if precise else _dot(a.astype(BF16), w)


def _split_dot(x, w01):
    hi = x.astype(BF16)
    lo = (x - hi.astype(F32)).astype(BF16)
    return _dot(hi, w01) + _dot(lo, w01)


def _split3_dot_left(w01, x):
    hi = x.astype(BF16)
    r1 = x - hi.astype(F32)
    mid = r1.astype(BF16)
    lo = (r1 - mid.astype(F32)).astype(BF16)
    return _dot(w01, hi) + _dot(w01, mid) + _dot(w01, lo)


def _sigmoid(x):
    return 1.0 / (1.0 + jnp.exp(-x))


def _silu(x):
    return x * _sigmoid(x)


def _softplus(x):
    return jnp.maximum(x, 0.0) + jnp.log(1.0 + jnp.exp(-jnp.abs(x)))


def _rms(x):
    return x * lax.rsqrt(jnp.mean(x * x, axis=-1, keepdims=True) + EPS)


class _Group:
    def __init__(self, B, S, tm):
        self.B, self.S, self.M = B, S, B * S
        self.precise = S == 1
        self.act_dtype = F32 if self.precise else BF16
        if S == 1:
            self.tm = min(tm, self.M)
            self.grid = (1, self.M // self.tm)
        else:
            self.tm = min(tm, S)
            self.grid = (B, S // self.tm)
        self.ns = self.grid[1]

    def rows(self, width, colblock=0):
        ns = self.ns
        return pl.BlockSpec((self.tm, width), lambda b, s: (b * ns + s, colblock))

    def full(self, shape):
        nd = len(shape)
        return pl.BlockSpec(shape, lambda b, s: (0,) * nd)

    def mod_array(self, m):
        return m.reshape(1, self.M, -1) if self.S == 1 else m.reshape(self.B, 1, -1)

    def mod_spec(self, width):
        if self.S == 1:
            return pl.BlockSpec((1, self.tm, width), lambda b, s: (0, s, 0))
        return pl.BlockSpec((1, 1, width), lambda b, s: (b, 0, 0))

    def pos_spec(self, width):
        if self.S == 1:
            return self.rows(width)
        return pl.BlockSpec((self.tm, width), lambda b, s: (s, 0))

    def tmaj_shape(self, width):
        return (self.M, width) if self.S == 1 else (self.S, self.B * width)

    def tmaj_spec(self, width):
        if self.S == 1:
            return self.rows(width)
        return pl.BlockSpec((self.tm, width), lambda b, s: (s, b))


def _ada_kernel(c_ref, w_ref, b_ref, o_ref):
    c = c_ref[...]
    o_ref[...] = _dot3(_silu(c), w_ref[...]) + b_ref[...]


def ada_matmul(c, w, b):
    m, k = c.shape
    n = w.shape[1]
    tn = 1024
    return pl.pallas_call(
        _ada_kernel,
        grid=(n // tn,),
        in_specs=[pl.BlockSpec((m, k), lambda j: (0, 0)), pl.BlockSpec((k, tn), lambda j: (0, j)),
                  pl.BlockSpec((1, tn), lambda j: (0, j))],
        out_specs=pl.BlockSpec((m, tn), lambda j: (0, j)),
        out_shape=jax.ShapeDtypeStruct((m, n), F32),
        compiler_params=_cparams("arbitrary"),
        name="ada_matmul",
    )(c, w, b.reshape(1, n))


def _win_kernel(x_ref, nw_ref, sc_ref, sh_ref, w_ref, *o_refs):
    h = (_rms(x_ref[...]) * nw_ref[...] * (1.0 + sc_ref[0]) + sh_ref[0]).astype(BF16)
    off = 0
    for o in o_refs:
        n = o.shape[-1]
        for c in range(0, n, 256):
            o[:, c:c + 256] = _dot(h, w_ref[:, off + c:off + c + 256])
        off += n


def _win_cols_kernel(x_ref, nw_ref, sc_ref, sh_ref, w_ref, o_ref):
    h = _rms(x_ref[...]) * nw_ref[...] * (1.0 + sc_ref[0]) + sh_ref[0]
    o_ref[...] = _dot3(h, w_ref[...])


def win_project_precise(g, x, nw, sc, sh, w_packed):
    tn = 512
    out = pl.pallas_call(
        _win_cols_kernel,
        grid=(W_IN_COLS // tn,),
        in_specs=[pl.BlockSpec((g.M, D_MODEL), lambda j: (0, 0)), pl.BlockSpec((1, D_MODEL), lambda j: (0, 0)),
                  pl.BlockSpec((1, g.M, D_MODEL), lambda j: (0, 0, 0)),
                  pl.BlockSpec((1, g.M, D_MODEL), lambda j: (0, 0, 0)),
                  pl.BlockSpec((D_MODEL, tn), lambda j: (0, j))],
        out_specs=pl.BlockSpec((g.M, tn), lambda j: (0, j)),
        out_shape=jax.ShapeDtypeStruct((g.M, W_IN_COLS), F32),
        compiler_params=_cparams("parallel"),
        name="win_project_precise",
    )(x, nw.reshape(1, -1), g.mod_array(sc), g.mod_array(sh), w_packed)
    offs = [0]
    for w in SEG_WIDTHS:
        offs.append(offs[-1] + w)
    return [out[:, offs[i]:offs[i + 1]] for i in range(len(SEG_WIDTHS))]


def win_project(g, x, nw, sc, sh, w_packed):
    if g.precise:
        return win_project_precise(g, x, nw, sc, sh, w_packed)
    return pl.pallas_call(
        _win_kernel,
        grid=g.grid,
        in_specs=[g.rows(D_MODEL), g.full((1, D_MODEL)), g.mod_spec(D_MODEL), g.mod_spec(D_MODEL),
                  g.full((D_MODEL, W_IN_COLS))],
        out_specs=[g.rows(w) for w in SEG_WIDTHS],
        out_shape=[jax.ShapeDtypeStruct((g.M, w), F32) for w in SEG_WIDTHS],
        compiler_params=_cparams("parallel", "parallel"),
        name="win_project",
    )(x, nw.reshape(1, -1), g.mod_array(sc), g.mod_array(sh), w_packed)


def _mla_prep_kernel(a_ref, cos_ref, sin_ref, cos8_ref, sin8_ref, qn_ref, kvn_ref, wuq_ref, wuk_ref,
                     qc_ref, kc_ref, ckv_ref, kpe_ref, *, precise):
    a = a_ref[...]
    tm = a.shape[0]
    odt = qc_ref.dtype
    qn = _rms(a[:, 0:QL]) * qn_ref[...]
    qa = _mm(qn, wuq_ref[...], precise)
    q_rope = qa[:, 512:768] * cos8_ref[...] + qa[:, 768:1024] * sin8_ref[...]
    zpad = jnp.zeros((tm, QK - KVL - ROPE), odt)
    for h in range(H):
        q_abs = _mm(qa[:, h * NOPE:(h + 1) * NOPE], wuk_ref[h], precise) * MLA_SCALE
        qc_ref[h, :, 0:KVL] = q_abs.astype(odt)
        qc_ref[h, :, KVL:KVL + ROPE] = (q_rope[:, h * ROPE:(h + 1) * ROPE] * MLA_SCALE).astype(odt)
        qc_ref[h, :, KVL + ROPE:QK] = zpad
    ckv = _rms(a[:, QL:QL + KVL]) * kvn_ref[...]
    kpe = a[:, 512:544] * cos_ref[...] + a[:, 552:584] * sin_ref[...]
    ckv_ref[...] = ckv
    kpe_ref[...] = kpe
    kc_ref[:, 0:KVL] = ckv.astype(odt)
    kc_ref[:, KVL:KVL + ROPE] = kpe.astype(odt)
    kc_ref[:, KVL + ROPE:QK] = zpad


def mla_prep(g, seg_a, tabs, qn, kvn, wuq, wuk):
    cos, sin, cos8, sin8 = tabs
    ns = g.ns
    return pl.pallas_call(
        functools.partial(_mla_prep_kernel, precise=g.precise),
        grid=g.grid,
        in_specs=[g.rows(SEG_A), g.pos_spec(ROPE), g.pos_spec(ROPE), g.pos_spec(H * ROPE), g.pos_spec(H * ROPE),
                  g.full((1, QL)), g.full((1, KVL)), g.full((QL, 1024)), g.full((H, NOPE, KVL))],
        out_specs=[pl.BlockSpec((H, g.tm, QK), lambda b, s: (0, b * ns + s, 0)), g.rows(QK), g.rows(KVL),
                   g.rows(ROPE)],
        out_shape=[jax.ShapeDtypeStruct((H, g.M, QK), g.act_dtype), jax.ShapeDtypeStruct((g.M, QK), g.act_dtype),
                   jax.ShapeDtypeStruct((g.M, KVL), F32), jax.ShapeDtypeStruct((g.M, ROPE), F32)],
        compiler_params=_cparams("parallel", "parallel"),
        name="mla_prep",
    )(seg_a, cos, sin, cos8, sin8, qn.reshape(1, -1), kvn.reshape(1, -1), wuq, wuk)


NEG = -1e30


def _attn_kernel(q_ref, k_ref, wuv_ref, o_ref, m_scr, l_scr, acc_scr, *, tq, tk):
    qi = pl.program_id(1)
    q = q_ref[...].reshape(H * tq, QK)
    m_scr[...] = jnp.full(m_scr.shape, NEG, F32)
    l_scr[...] = jnp.zeros(l_scr.shape, F32)
    acc_scr[...] = jnp.zeros(acc_scr.shape, F32)
    row_tok = qi * tq + lax.broadcasted_iota(jnp.int32, (H * tq, tk), 0) % tq
    col = lax.broadcasted_iota(jnp.int32, (H * tq, tk), 1)

    def body(j, carry):
        k = k_ref[pl.ds(pl.multiple_of(j * tk, tk), tk), :]
        s = _dot_nt(q, k)
        s = jnp.where(col + j * tk <= row_tok, s, NEG)
        m_prev = m_scr[...]
        m_new = jnp.maximum(m_prev, jnp.max(s, axis=-1, keepdims=True))
        alpha = jnp.exp(m_prev - m_new)
        p = jnp.exp(s - m_new)
        l_scr[...] = alpha * l_scr[...] + jnp.sum(p, axis=-1, keepdims=True)
        acc_scr[...] = alpha * acc_scr[...] + _dot(p.astype(BF16), k[:, 0:KVL])
        m_scr[...] = m_new
        return carry

    lax.fori_loop(0, (qi * tq + tq + tk - 1) // tk, body, 0)
    o = acc_scr[...] / l_scr[...]
    for h in range(H):
        o_ref[:, h * VD:(h + 1) * VD] = _dot(o[h * tq:(h + 1) * tq].astype(BF16), wuv_ref[h]).astype(o_ref.dtype)


def mla_attention(B, S, qc, kc, wuv):
    tq = min(128, S)
    tk = min(256, S)
    nq = S // tq
    return pl.pallas_call(
        functools.partial(_attn_kernel, tq=tq, tk=tk),
        grid=(B, nq),
        in_specs=[pl.BlockSpec((H, tq, QK), lambda b, i: (0, b * nq + i, 0)),
                  pl.BlockSpec((S, QK), lambda b, i: (b, 0)),
                  pl.BlockSpec((H, KVL, VD), lambda b, i: (0, 0, 0))],
        out_specs=pl.BlockSpec((tq, H * VD), lambda b, i: (b * nq + i, 0)),
        out_shape=jax.ShapeDtypeStruct((B * S, H * VD), BF16),
        scratch_shapes=[pltpu.VMEM((H * tq, 1), F32), pltpu.VMEM((H * tq, 1), F32), pltpu.VMEM((H * tq, KVL), F32)],
        compiler_params=_cparams("parallel", "parallel"),
        name="mla_attention",
    )(qc, kc, wuv)


PAGES_PER_STEP = 8


def _paged_kernel(pt_ref, q_ref, knew_ref, wuv_ref, *rest):
    pp = PAGES_PER_STEP
    ckv_refs, kpe_refs = rest[:pp], rest[pp:2 * pp]
    o_ref, m_scr, l_scr, acc_scr = rest[2 * pp:]
    j = pl.program_id(1)
    q = q_ref[...]

    @pl.when(j == 0)
    def _():
        m_scr[...] = jnp.full(m_scr.shape, NEG, F32)
        l_scr[...] = jnp.zeros(l_scr.shape, F32)
        acc_scr[...] = jnp.zeros(acc_scr.shape, F32)

    ckv_hi, ckv_lo = _hi_lo(jnp.concatenate([r[...] for r in ckv_refs], axis=0))
    kpe_hi, kpe_lo = _hi_lo(jnp.concatenate([r[...] for r in kpe_refs], axis=0))
    qa_hi, qa_lo = _hi_lo(q[:, 0:KVL])
    qp_hi, qp_lo = _hi_lo(q[:, KVL:KVL + ROPE])
    s = (_dot_nt(qa_hi, ckv_hi) + _dot_nt(qa_lo, ckv_hi) + _dot_nt(qa_hi, ckv_lo)
         + _dot_nt(qp_hi, kpe_hi) + _dot_nt(qp_lo, kpe_hi) + _dot_nt(qp_hi, kpe_lo))
    m_prev = m_scr[...]
    m_new = jnp.maximum(m_prev, jnp.max(s, axis=-1, keepdims=True))
    alpha = jnp.exp(m_prev - m_new)
    p = jnp.exp(s - m_new)
    p_hi, p_lo = _hi_lo(p)
    l_scr[...] = alpha * l_scr[...] + jnp.sum(p, axis=-1, keepdims=True)
    acc_scr[...] = alpha * acc_scr[...] + _dot(p_hi, ckv_hi) + _dot(p_lo, ckv_hi) + _dot(p_hi, ckv_lo)
    m_scr[...] = m_new

    @pl.when(j == pl.num_programs(1) - 1)
    def _():
        kn = knew_ref[0]
        s_new = jnp.sum(q * kn, axis=-1, keepdims=True)
        m_prev = m_scr[...]
        m_new = jnp.maximum(m_prev, s_new)
        alpha = jnp.exp(m_prev - m_new)
        p_new = jnp.exp(s_new - m_new)
        l = alpha * l_scr[...] + p_new
        acc = alpha * acc_scr[...] + p_new * kn[:, 0:KVL]
        o = acc / l
        for h in range(H):
            o_ref[0, :, h * VD:(h + 1) * VD] = _dot3(o[h:h + 1], wuv_ref[h])


def paged_attention(page_table, qc, kc_new, cache_ckv, cache_kpe, wuv):
    B, n_pages = page_table.shape
    pp = PAGES_PER_STEP
    nsteps = n_pages // pp

    def page_spec(i, width):
        return pl.BlockSpec((None, PAGE, width), lambda b, j, pt: (pt[b, j * pp + i], 0, 0))

    grid_spec = pltpu.PrefetchScalarGridSpec(
        num_scalar_prefetch=1,
        grid=(B, nsteps),
        in_specs=[pl.BlockSpec((None, H, QK), lambda b, j, pt: (b, 0, 0)),
                  pl.BlockSpec((1, 1, QK), lambda b, j, pt: (b, 0, 0)),
                  pl.BlockSpec((H, KVL, VD), lambda b, j, pt: (0, 0, 0))]
        + [page_spec(i, KVL) for i in range(pp)] + [page_spec(i, ROPE) for i in range(pp)],
        out_specs=pl.BlockSpec((1, 1, H * VD), lambda b, j, pt: (b, 0, 0)),
        scratch_shapes=[pltpu.VMEM((H, 1), F32), pltpu.VMEM((H, 1), F32), pltpu.VMEM((H, KVL), F32)],
    )
    out = pl.pallas_call(
        _paged_kernel,
        grid_spec=grid_spec,
        out_shape=jax.ShapeDtypeStruct((B, 1, H * VD), F32),
        compiler_params=_cparams("parallel", "arbitrary"),
        name="paged_attention",
    )(page_table, jnp.transpose(qc, (1, 0, 2)), kc_new.reshape(B, 1, QK), wuv, *([cache_ckv] * pp),
      *([cache_kpe] * pp))
    return out.reshape(B, H * VD)


def _rwkv_prep_kernel(rw_ref, prev_ref, mu_ref, w0_ref, a0_ref, kk_ref_, ka_ref, w2_ref, a2_ref, g2_ref, ones_ref,
                      r_o, w_o, k_o, v_o, kk_o, b_o, g_o, *, precise):
    rw = rw_ref[...]
    xs = rw + (prev_ref[...] - rw) * mu_ref[...]
    r = xs[:, 0:RDIM]
    k = xs[:, RDIM:2 * RDIM]
    v = xs[:, 2 * RDIM:3 * RDIM]
    o = 3 * RDIM
    wl = xs[:, o:o + DECAY_LORA]
    al = xs[:, o + DECAY_LORA:o + DECAY_LORA + AAA_LORA]
    gl = xs[:, o + DECAY_LORA + AAA_LORA:RWKV_IN]
    w = -_softplus(-(w0_ref[...] + _mm(jnp.tanh(wl), w2_ref[...], precise))) - 0.5
    a = _sigmoid(a0_ref[...] + _mm(al, a2_ref[...], precise))
    kk = k * kk_ref_[...]
    kk = kk * lax.rsqrt(jnp.maximum(_split_dot(kk * kk, ones_ref[...]), 1e-24))
    r_o[...] = r
    w_o[...] = jnp.exp(-jnp.exp(w))
    k_o[...] = k * (1.0 + (a - 1.0) * ka_ref[...])
    v_o[...] = v
    kk_o[...] = kk
    b_o[...] = kk * a
    g_o[...] = _mm(_sigmoid(gl), g2_ref[...], precise)


def rwkv_prep(g, rw, prev, p, mw, ones_bd):
    vec = lambda a: a.reshape(1, -1)
    tshape = jax.ShapeDtypeStruct(g.tmaj_shape(RDIM), F32)
    return pl.pallas_call(
        functools.partial(_rwkv_prep_kernel, precise=g.precise),
        grid=g.grid,
        in_specs=[g.rows(RWKV_IN), g.rows(RWKV_IN), g.full((1, RWKV_IN)), g.full((1, RDIM)), g.full((1, RDIM)),
                  g.full((1, RDIM)), g.full((1, RDIM)), g.full((DECAY_LORA, RDIM)), g.full((AAA_LORA, RDIM)),
                  g.full((GATE_LORA, RDIM)), g.full((RDIM, RDIM))],
        out_specs=[g.tmaj_spec(RDIM)] * 6 + [g.rows(RDIM)],
        out_shape=[tshape] * 6 + [jax.ShapeDtypeStruct((g.M, RDIM), F32)],
        compiler_params=_cparams("parallel", "parallel"),
        name="rwkv_prep",
    )(rw, prev, vec(p["mu"]), vec(p["w0"]), vec(p["a0"]), vec(p["k_k"]), vec(p["k_a"]), mw["w2"], mw["a2"], mw["g2"],
      ones_bd)


SCAN_NB = 16
SCAN_GB = 8


def _scan_kernel(r_ref, w_ref, k_ref, kk_ref, b_ref, vt_ref, s0_ref, yt_ref, sout_ref, s_scr, *, tc_len, ng, precise):
    tc = pl.program_id(1)

    def pick(x, w01):
        return _split_dot(x, w01) if precise else _dot(x.astype(BF16), w01)

    @pl.when(tc == 0)
    def _():
        s_scr[...] = s0_ref[...]

    ri = lax.broadcasted_iota(jnp.int32, (128, 128), 0)
    ci = lax.broadcasted_iota(jnp.int32, (128, 128), 1)
    same_head = (ri // RHD) == (ci // RHD)
    ones_bd = same_head.astype(BF16)
    lane = lax.broadcasted_iota(jnp.int32, (RHD, 128), 1) % RHD
    yt_ref[...] = jnp.zeros(yt_ref.shape, F32)
    gb = SCAN_GB

    def step(tt, carry):
        e_t = (same_head & ((ri % RHD) == tt)).astype(BF16)
        sel = lane == tt
        for g0 in range(0, ng, gb):
            def rows(ref):
                return jnp.stack([jnp.broadcast_to(ref[tt, pl.ds(g0 + i, 1), :], (RHD, 128)) for i in range(gb)])

            s = s_scr[g0:g0 + gb]
            sa = pick((s * rows(kk_ref)).reshape(gb * RHD, 128), ones_bd).reshape(gb, RHD, 128)
            vcol = pick(vt_ref[0, g0:g0 + gb].reshape(gb * RHD, 128), e_t).reshape(gb, RHD, 128)
            s = s * rows(w_ref) - sa * rows(b_ref) + vcol * rows(k_ref)
            s_scr[g0:g0 + gb] = s
            y = pick((s * rows(r_ref)).reshape(gb * RHD, 128), ones_bd).reshape(gb, RHD, 128)
            yt_ref[0, g0:g0 + gb] = jnp.where(sel, y, yt_ref[0, g0:g0 + gb])
        return carry

    lax.fori_loop(0, tc_len, step, 0)

    @pl.when(tc == pl.num_programs(1) - 1)
    def _():
        sout_ref[...] = s_scr[...]


def rwkv_scan(B, S, r, w, k, kk, b, vt, s0, precise):
    tc_len = min(RHD, S)
    nb = min(SCAN_NB, B)
    ng = nb * 4
    nchunks = S // tc_len
    row_spec = pl.BlockSpec((tc_len, ng, 128), lambda bg, c: (c, bg, 0))
    st_spec = pl.BlockSpec((ng, RHD, 128), lambda bg, c: (bg, 0, 0))
    ch_spec = pl.BlockSpec((1, ng, RHD, 128), lambda bg, c: (c, bg, 0, 0))
    return pl.pallas_call(
        functools.partial(_scan_kernel, tc_len=tc_len, ng=ng, precise=precise),
        grid=(B // nb, nchunks),
        in_specs=[row_spec] * 5 + [ch_spec, st_spec],
        out_specs=[ch_spec, st_spec],
        out_shape=[jax.ShapeDtypeStruct((nchunks, B * 4, RHD, 128), F32),
                   jax.ShapeDtypeStruct((B * 4, RHD, 128), F32)],
        scratch_shapes=[pltpu.VMEM((ng, RHD, 128), F32)],
        compiler_params=_cparams("parallel", "arbitrary"),
        name="rwkv_scan",
    )(r, w, k, kk, b, vt, s0)


def _rwkv_post_kernel(y_ref, r_ref, k_ref, v_ref, g_ref, lnw_ref, lnb_ref, rk_ref, ones_ref, o_ref):
    ones = ones_ref[...]
    y = y_ref[...]
    mu = _split_dot(y, ones) * (1.0 / RHD)
    yc = y - mu
    var = _split_dot(yc * yc, ones) * (1.0 / RHD)
    yn = yc * lax.rsqrt(var + RWKV_LN_EPS) * lnw_ref[...] + lnb_ref[...]
    v = v_ref[...]
    bonus = _split_dot(r_ref[...] * k_ref[...] * rk_ref[...], ones)
    o_ref[...] = ((yn + bonus * v) * g_ref[...]).astype(o_ref.dtype)


def rwkv_post(g, y, r, k, v, gate, p, ones_bd):
    vec = lambda a: a.reshape(1, -1)
    return pl.pallas_call(
        _rwkv_post_kernel,
        grid=g.grid,
        in_specs=[g.rows(RDIM), g.tmaj_spec(RDIM), g.tmaj_spec(RDIM), g.tmaj_spec(RDIM), g.rows(RDIM),
                  g.full((1, RDIM)), g.full((1, RDIM)), g.full((1, RDIM)), g.full((RDIM, RDIM))],
        out_specs=g.rows(RDIM),
        out_shape=jax.ShapeDtypeStruct((g.M, RDIM), g.act_dtype),
        compiler_params=_cparams("parallel", "parallel"),
        name="rwkv_post",
    )(y, r, k, v, gate, vec(p["ln_w"]), vec(p["ln_b"]), vec(p["r_k"]), ones_bd)


def _ssm_gate_norm(y, z, nw):
    y = y * _silu(z)
    gw = SDIM // SGROUPS
    parts = [_rms(y[:, i * gw:(i + 1) * gw]) for i in range(SGROUPS)]
    return jnp.concatenate(parts, axis=-1) * nw


def _ssd_kernel(xbc_ref, z_ref, dt_ref, tail0_ref, cw_ref, cb_ref, dtb_ref, a_ref, dsk_ref, nw_ref,
                o_ref, hout_ref, tail_scr, h_scr, y_scr):
    c = pl.program_id(1)
    L = SCHUNK

    @pl.when(c == 0)
    def _():
        tail_scr[...] = tail0_ref[0]
        h_scr[...] = jnp.zeros(h_scr.shape, F32)

    xbc = xbc_ref[...]
    tail = tail_scr[...]
    row8 = lax.broadcasted_iota(jnp.int32, (8, SCD), 0)
    conv = cb_ref[...] + xbc * cw_ref[SCONV - 1:SCONV, :]
    for sft in range(1, SCONV):
        sh = pltpu.roll(xbc, sft, 0)
        top = jnp.where(row8 < sft, pltpu.roll(tail, sft, 0), sh[0:8])
        sh = jnp.concatenate([top, sh[8:]], axis=0)
        conv = conv + sh * cw_ref[SCONV - 1 - sft:SCONV - sft, :]
    tail_scr[...] = xbc[L - 8:L]
    act = _silu(conv)
    xa = act[:, 0:SDIM]
    lane = lax.broadcasted_iota(jnp.int32, (L, 128), 1)
    dt_valid = (lane >= DT_LANE) & (lane < DT_LANE + SHEADS)
    dtt = jnp.where(dt_valid, _softplus(dt_ref[...] + dtb_ref[...]), 0.0)
    adt = dtt * a_ref[...]
    ri = lax.broadcasted_iota(jnp.int32, (L, L), 0)
    ci = lax.broadcasted_iota(jnp.int32, (L, L), 1)
    causal = ri >= ci
    cs = _split3_dot_left(causal.astype(BF16), adt)
    cs_t = cs.T
    total = cs[L - 1:L, :]
    for grp in range(SGROUPS):
        bm = act[:, SDIM + grp * SN:SDIM + (grp + 1) * SN]
        cm = act[:, SDIM + SGROUPS * SN + grp * SN:SDIM + SGROUPS * SN + (grp + 1) * SN]
        cb = _dot_nt(cm.astype(BF16), bm.astype(BF16))
        for hh in range(SHEADS // SGROUPS):
            h = grp * (SHEADS // SGROUPS) + hh
            ln = DT_LANE + h
            col = cs[:, ln:ln + 1]
            row = cs_t[ln:ln + 1, :]
            tot = total[:, ln:ln + 1]
            lmat = jnp.where(causal, jnp.exp(jnp.where(causal, col - row, 0.0)), 0.0)
            xh = xa[:, h * SP:(h + 1) * SP]
            xdt = (xh * dtt[:, ln:ln + 1]).astype(BF16)
            hprev = h_scr[h]
            y = _dot((cb * lmat).astype(BF16), xdt)
            y = y + _dot((cm * jnp.exp(col)).astype(BF16), hprev.astype(BF16))
            y_scr[:, h * SP:(h + 1) * SP] = y + dsk_ref[:, h * SP:(h + 1) * SP] * xh
            bdec = (bm * jnp.exp(tot - col)).T.astype(BF16)
            h_scr[h] = hprev * jnp.exp(tot) + _dot(bdec, xdt)
    o_ref[...] = _ssm_gate_norm(y_scr[...], z_ref[...], nw_ref[...]).astype(o_ref.dtype)

    @pl.when(c == pl.num_programs(1) - 1)
    def _():
        hout_ref[0] = h_scr[...]


def ssd_prompt(B, S, xbc, z, seg_a, tail0, p):
    nc = S // SCHUNK
    L = SCHUNK
    full = lambda shape: pl.BlockSpec(shape, lambda b, c: (0,) * len(shape))
    return pl.pallas_call(
        _ssd_kernel,
        grid=(B, nc),
        in_specs=[pl.BlockSpec((L, SCD), lambda b, c: (b * nc + c, 0)),
                  pl.BlockSpec((L, SDIM), lambda b, c: (b * nc + c, 0)),
                  pl.BlockSpec((L, 128), lambda b, c: (b * nc + c, DT_TILE)),
                  pl.BlockSpec((1, 8, SCD), lambda b, c: (b, 0, 0)),
                  full((SCONV, SCD)), full((1, SCD)), full((1, 128)), full((1, 128)), full((1, SDIM)),
                  full((1, SDIM))],
        out_specs=[pl.BlockSpec((L, SDIM), lambda b, c: (b * nc + c, 0)),
                   pl.BlockSpec((1, SHEADS, SN, SP), lambda b, c: (b, 0, 0, 0))],
        out_shape=[jax.ShapeDtypeStruct((B * S, SDIM), BF16), jax.ShapeDtypeStruct((B, SHEADS, SN, SP), F32)],
        scratch_shapes=[pltpu.VMEM((8, SCD), F32), pltpu.VMEM((SHEADS, SN, SP), F32), pltpu.VMEM((L, SDIM), F32)],
        compiler_params=_cparams("parallel", "arbitrary"),
        name="ssd_prompt",
    )(xbc, z, seg_a, tail0, p["conv_w"], p["conv_b"].reshape(1, -1), p["dtb128"], p["a128"], p["dskip512"],
      p["ssm_norm"].reshape(1, -1))


def _ssm_step_pre_kernel(xbc_ref, c0_ref, c1_ref, c2_ref, dt_ref, cw_ref, cb_ref, dtb_ref, a_ref,
                         act_ref, xdt_ref, dec_ref):
    conv = (cb_ref[...] + c0_ref[...] * cw_ref[0:1, :] + c1_ref[...] * cw_ref[1:2, :] + c2_ref[...] * cw_ref[2:3, :]
            + xbc_ref[...] * cw_ref[3:4, :])
    act = _silu(conv)
    act_ref[...] = act
    dtt = _softplus(dt_ref[...] + dtb_ref[...])
    dec_ref[...] = jnp.exp(dtt * a_ref[...])
    for h in range(SHEADS):
        xdt_ref[:, h * SP:(h + 1) * SP] = act[:, h * SP:(h + 1) * SP] * dtt[:, DT_LANE + h:DT_LANE + h + 1]


def ssm_step_pre(g, xbc, c0, c1, c2, seg_a, p):
    return pl.pallas_call(
        _ssm_step_pre_kernel,
        grid=g.grid,
        in_specs=[g.rows(SCD)] * 4 + [g.rows(128, DT_TILE), g.full((SCONV, SCD)), g.full((1, SCD)),
                                      g.full((1, 128)), g.full((1, 128))],
        out_specs=[g.rows(SCD), g.rows(SDIM), g.rows(128)],
        out_shape=[jax.ShapeDtypeStruct((g.M, SCD), F32), jax.ShapeDtypeStruct((g.M, SDIM), F32),
                   jax.ShapeDtypeStruct((g.M, 128), F32)],
        compiler_params=_cparams("parallel", "parallel"),
        name="ssm_step_pre",
    )(xbc, c0, c1, c2, seg_a, p["conv_w"], p["conv_b"].reshape(1, -1), p["dtb128"], p["a128"])


SSM_STEP_BT = 8


def _ssm_step_kernel(h0_ref, xb_ref, dec_ref, bm_ref, cm_ref, h1_ref, y_ref):
    bt = SSM_STEP_BT
    rp = lax.broadcasted_iota(jnp.int32, (SP, 128), 0)
    lp = lax.broadcasted_iota(jnp.int32, (SP, 128), 1)
    pick = [lp == rp, lp == rp + SP]
    hpg = SHEADS // SGROUPS
    for i in range(bt):
        for hp in range(SHEADS // 2):
            yrow = jnp.zeros((1, 128), F32)
            for e in range(2):
                h = hp * 2 + e
                grp = h // hpg
                bm = bm_ref[i:i + 1, grp * SN:(grp + 1) * SN]
                cm = cm_ref[i:i + 1, grp * SN:(grp + 1) * SN]
                h1 = h0_ref[i, h] * dec_ref[i, h] + xb_ref[i, h] * bm
                h1_ref[i, h] = h1
                ycol = jnp.sum(h1 * cm, axis=-1, keepdims=True)
                yrow = yrow + jnp.sum(jnp.where(pick[e], ycol, 0.0), axis=0, keepdims=True)
            y_ref[i:i + 1, hp * 128:(hp + 1) * 128] = yrow


def ssm_step(B, h0, xb, dec, bm, cm):
    bt = SSM_STEP_BT
    st = pl.BlockSpec((bt, SHEADS, SP, SN), lambda i: (i, 0, 0, 0))
    return pl.pallas_call(
        _ssm_step_kernel,
        grid=(B // bt,),
        in_specs=[st, st, pl.BlockSpec((bt, SHEADS, 1, SN), lambda i: (i, 0, 0, 0)),
                  pl.BlockSpec((bt, SGROUPS * SN), lambda i: (i, 0)), pl.BlockSpec((bt, SGROUPS * SN), lambda i: (i, 0))],
        out_specs=[st, pl.BlockSpec((bt, SDIM), lambda i: (i, 0))],
        out_shape=[jax.ShapeDtypeStruct((B, SHEADS, SP, SN), F32), jax.ShapeDtypeStruct((B, SDIM), F32)],
        compiler_params=_cparams("parallel"),
        name="ssm_step",
    )(h0, xb, dec, bm, cm)


def _ssm_step_post_kernel(y_ref, x_ref, z_ref, dsk_ref, nw_ref, o_ref):
    y = y_ref[...] + dsk_ref[...] * x_ref[...]
    o_ref[...] = _ssm_gate_norm(y, z_ref[...], nw_ref[...]).astype(o_ref.dtype)


def ssm_step_post(g, y, act, z, p):
    return pl.pallas_call(
        _ssm_step_post_kernel,
        grid=g.grid,
        in_specs=[g.rows(SDIM), g.rows(SDIM), g.rows(SDIM), g.full((1, SDIM)), g.full((1, SDIM))],
        out_specs=g.rows(SDIM),
        out_shape=jax.ShapeDtypeStruct((g.M, SDIM), g.act_dtype),
        compiler_params=_cparams("parallel", "parallel"),
        name="ssm_step_post",
    )(y, act, z, p["dskip512"], p["ssm_norm"].reshape(1, -1))


def _merge_kernel(a_ref, r_ref, s_ref, gate_ref, x_ref, ga_ref, wb_ref, wo_ref, o_ref, *, precise):
    acc = None
    for i, br in enumerate((a_ref, r_ref, s_ref)):
        t = _sigmoid(gate_ref[:, i * D_MODEL:(i + 1) * D_MODEL]) * _mm(br[...], wb_ref[i], precise)
        acc = t if acc is None else acc + t
    mix = _mm(acc, wo_ref[...], precise)
    o_ref[...] = x_ref[...] + ga_ref[0] * mix


def merge_branches(g, a_out, r_out, s_out, gate, x, ga, wb, wo):
    return pl.pallas_call(
        functools.partial(_merge_kernel, precise=g.precise),
        grid=g.grid,
        in_specs=[g.rows(512), g.rows(512), g.rows(512), g.rows(NBRANCH * D_MODEL), g.rows(D_MODEL),
                  g.mod_spec(D_MODEL), g.full((NBRANCH, 512, D_MODEL)), g.full((D_MODEL, D_MODEL))],
        out_specs=g.rows(D_MODEL),
        out_shape=jax.ShapeDtypeStruct((g.M, D_MODEL), F32),
        compiler_params=_cparams("parallel", "parallel"),
        name="merge_branches",
    )(a_out, r_out, s_out, gate, x, g.mod_array(ga), wb, wo)


def _ffn_kernel(x_ref, nw_ref, sc_ref, sh_ref, gf_ref, w1_ref, w3_ref, w2_ref, o_ref, h_scr, acc_scr, *, precise):
    j = pl.program_id(2)

    @pl.when(j == 0)
    def _():
        h_scr[...] = (_rms(x_ref[...]) * nw_ref[...] * (1.0 + sc_ref[0]) + sh_ref[0]).astype(h_scr.dtype)
        acc_scr[...] = jnp.zeros(acc_scr.shape, F32)

    h = h_scr[...]
    u = _silu(_mm(h, w1_ref[...], precise)) * _mm(h, w3_ref[...], precise)
    acc_scr[...] += _mm(u, w2_ref[...], precise)

    @pl.when(j == pl.num_programs(2) - 1)
    def _():
        o_ref[...] = x_ref[...] + gf_ref[0] * acc_scr[...]


def dense_ffn(g, x, nw, sc, sh, gf, w1, w3, w2, tf):
    dff = w1.shape[1]
    lift = lambda spec: pl.BlockSpec(spec.block_shape, lambda b, s, j, f=spec.index_map: f(b, s))
    return pl.pallas_call(
        functools.partial(_ffn_kernel, precise=g.precise),
        grid=g.grid + (dff // tf,),
        in_specs=[lift(g.rows(D_MODEL)), lift(g.full((1, D_MODEL))), lift(g.mod_spec(D_MODEL)),
                  lift(g.mod_spec(D_MODEL)), lift(g.mod_spec(D_MODEL)),
                  pl.BlockSpec((D_MODEL, tf), lambda b, s, j: (0, j)), pl.BlockSpec((D_MODEL, tf), lambda b, s, j: (0, j)),
                  pl.BlockSpec((tf, D_MODEL), lambda b, s, j: (j, 0))],
        out_specs=lift(g.rows(D_MODEL)),
        out_shape=jax.ShapeDtypeStruct((g.M, D_MODEL), F32),
        scratch_shapes=[pltpu.VMEM((g.tm, D_MODEL), g.act_dtype), pltpu.VMEM((g.tm, D_MODEL), F32)],
        compiler_params=_cparams("parallel", "parallel", "arbitrary"),
        name="dense_ffn",
    )(x, nw.reshape(1, -1), g.mod_array(sc), g.mod_array(sh), g.mod_array(gf), w1, w3, w2)


def _router_kernel(x_ref, nw_ref, sc_ref, sh_ref, wr_hi_ref, wr_lo_ref, rb_ref, h_ref, logit_ref):
    h = _rms(x_ref[...]) * nw_ref[...] * (1.0 + sc_ref[0]) + sh_ref[0]
    hi = h.astype(BF16)
    lo = (h - hi.astype(F32)).astype(BF16)
    h_ref[...] = hi
    logit_ref[...] = (_dot(hi, wr_hi_ref[...]) + _dot(lo, wr_hi_ref[...]) + _dot(hi, wr_lo_ref[...])
                      + rb_ref[...])


def moe_router(g, x, nw, sc, sh, wr_hi, wr_lo, rb):
    return pl.pallas_call(
        _router_kernel,
        grid=g.grid,
        in_specs=[g.rows(D_MODEL), g.full((1, D_MODEL)), g.mod_spec(D_MODEL), g.mod_spec(D_MODEL),
                  g.full((D_MODEL, 128)), g.full((D_MODEL, 128)), g.full((1, 128))],
        out_specs=[g.rows(D_MODEL), g.rows(128)],
        out_shape=[jax.ShapeDtypeStruct((g.M, D_MODEL), BF16), jax.ShapeDtypeStruct((g.M, 128), F32)],
        compiler_params=_cparams("parallel", "parallel"),
        name="moe_router",
    )(x, nw.reshape(1, -1), g.mod_array(sc), g.mod_array(sh), wr_hi, wr_lo, rb)


def _expert_kernel(te_ref, tv_ref, h_ref, w1_ref, w3_ref, w2_ref, o_ref, acc_scr):
    i = pl.program_id(0)
    j = pl.program_id(1)

    @pl.when(j == 0)
    def _():
        acc_scr[...] = jnp.zeros(acc_scr.shape, F32)

    @pl.when(tv_ref[i] > 0)
    def _():
        h = h_ref[...]
        u = (_silu(_dot(h, w1_ref[...])) * _dot(h, w3_ref[...])).astype(BF16)
        acc_scr[...] += _dot(u, w2_ref[...])

    @pl.when(j == pl.num_programs(1) - 1)
    def _():
        o_ref[...] = acc_scr[...]


def expert_ffn(tile_expert, tile_valid, h_sorted, w1, w3, w2, tm, tf):
    rows = h_sorted.shape[0]
    dffe = w1.shape[2]
    grid_spec = pltpu.PrefetchScalarGridSpec(
        num_scalar_prefetch=2,
        grid=(rows // tm, dffe // tf),
        in_specs=[pl.BlockSpec((tm, D_MODEL), lambda i, j, te, tv: (i, 0)),
                  pl.BlockSpec((None, D_MODEL, tf), lambda i, j, te, tv: (te[i], 0, j)),
                  pl.BlockSpec((None, D_MODEL, tf), lambda i, j, te, tv: (te[i], 0, j)),
                  pl.BlockSpec((None, tf, D_MODEL), lambda i, j, te, tv: (te[i], j, 0))],
        out_specs=pl.BlockSpec((tm, D_MODEL), lambda i, j, te, tv: (i, 0)),
        scratch_shapes=[pltpu.VMEM((tm, D_MODEL), F32)],
    )
    return pl.pallas_call(
        _expert_kernel,
        grid_spec=grid_spec,
        out_shape=jax.ShapeDtypeStruct((rows, D_MODEL), F32),
        compiler_params=_cparams("parallel", "arbitrary"),
        name="expert_ffn",
    )(tile_expert, tile_valid, h_sorted, w1, w3, w2)


def _combine_kernel(x_ref, gf_ref, y0_ref, y1_ref, wt_ref, o_ref):
    wt = wt_ref[...]
    f = wt[:, 0:1] * y0_ref[...] + wt[:, 1:2] * y1_ref[...]
    o_ref[...] = x_ref[...] + gf_ref[0] * f


def moe_combine(g, x, gf, y0, y1, wt):
    return pl.pallas_call(
        _combine_kernel,
        grid=g.grid,
        in_specs=[g.rows(D_MODEL), g.mod_spec(D_MODEL), g.rows(D_MODEL), g.rows(D_MODEL), g.rows(128)],
        out_specs=g.rows(D_MODEL),
        out_shape=jax.ShapeDtypeStruct((g.M, D_MODEL), F32),
        compiler_params=_cparams("parallel", "parallel"),
        name="moe_combine",
    )(x, g.mod_array(gf), y0, y1, wt)


def moe_ffn(g, x, nw, sc, sh, gf, wr_hi, wr_lo, rb, w1, w3, w2):
    M = g.M
    h, logits = moe_router(g, x, nw, sc, sh, wr_hi, wr_lo, rb)
    top_v, top_i = lax.top_k(logits[:, :NE], TOPK)
    top_w = jax.nn.softmax(top_v, axis=-1)
    tm = min(512, max(128, M // 4))
    tf = 1792
    flat_e = top_i.reshape(-1)
    onehot = (flat_e[:, None] == jnp.arange(NE)[None, :]).astype(jnp.int32)
    rank = jnp.take_along_axis(jnp.cumsum(onehot, axis=0) - onehot, flat_e[:, None], axis=1)[:, 0]
    counts = jnp.sum(onehot, axis=0)
    padded = ((counts + tm - 1) // tm) * tm
    starts = jnp.cumsum(padded) - padded
    pos = starts[flat_e] + rank
    n_rows = M * TOPK + NE * tm
    row_token = jnp.zeros((n_rows,), jnp.int32).at[pos].set(jnp.arange(M * TOPK, dtype=jnp.int32) // TOPK)
    tile_start = jnp.arange(n_rows // tm, dtype=jnp.int32) * tm
    ends = starts + padded
    tile_expert = jnp.minimum(jnp.sum((tile_start[:, None] >= ends[None, :]).astype(jnp.int32), axis=1), NE - 1)
    tile_valid = (tile_start < ends[NE - 1]).astype(jnp.int32)
    h_sorted = jnp.take(h, row_token, axis=0)
    y_sorted = expert_ffn(tile_expert.astype(jnp.int32), tile_valid, h_sorted, w1, w3, w2, tm, tf)
    pos2 = pos.reshape(M, TOPK)
    y0 = jnp.take(y_sorted, pos2[:, 0], axis=0)
    y1 = jnp.take(y_sorted, pos2[:, 1], axis=0)
    wt = jnp.pad(top_w, ((0, 0), (0, 128 - TOPK)))
    return moe_combine(g, x, gf, y0, y1, wt)


def _final_norm_kernel(x_ref, w_ref, o_ref):
    o_ref[...] = _rms(x_ref[...]) * w_ref[...]


def final_norm(g, x, w):
    return pl.pallas_call(
        _final_norm_kernel,
        grid=g.grid,
        in_specs=[g.rows(D_MODEL), g.full((1, D_MODEL))],
        out_specs=g.rows(D_MODEL),
        out_shape=jax.ShapeDtypeStruct((g.M, D_MODEL), F32),
        compiler_params=_cparams("parallel", "parallel"),
        name="final_norm",
    )(x, w.reshape(1, -1))


def _rot_half_cols(w):
    half = ROPE // 2
    return jnp.concatenate([-w[..., half:], w[..., :half]], axis=-1)


def _pack_layer(l, W):
    p = {}
    w_in = W["w_in"][l]
    o = 0
    q_c, kv_c, kr = w_in[:, 0:QL], w_in[:, QL:QL + KVL], w_in[:, QL + KVL:QL + KVL + ROPE]
    o = QL + KVL + ROPE
    rw = w_in[:, o:o + RWKV_IN]
    o += RWKV_IN
    z = w_in[:, o:o + SDIM]
    o += SDIM
    xbc = w_in[:, o:o + SCD]
    o += SCD
    dt = w_in[:, o:o + SHEADS]
    o += SHEADS
    gate = w_in[:, o:]
    seg_a = jnp.concatenate([q_c, kv_c, kr, dt, _rot_half_cols(kr),
                             jnp.zeros((D_MODEL, SEG_A - (QL + KVL + 2 * ROPE + SHEADS)), F32)], axis=1)
    mw = {}
    mw["w_in"] = jnp.concatenate([seg_a, rw, z, xbc, gate], axis=1)
    wq = W["mla_w_uq"][l].reshape(QL, H, NOPE + ROPE)
    pe = wq[:, :, NOPE:]
    mw["wuq"] = jnp.concatenate([wq[:, :, :NOPE].reshape(QL, H * NOPE), pe.reshape(QL, H * ROPE),
                                 _rot_half_cols(pe).reshape(QL, H * ROPE)], axis=1)
    mw["wuk"] = jnp.transpose(W["mla_w_uk"][l], (1, 2, 0))
    mw["wuv"] = jnp.transpose(W["mla_w_uv"][l], (1, 0, 2))
    mw["w2"], mw["a2"], mw["g2"] = W["rwkv_w2"][l], W["rwkv_a2"][l], W["rwkv_g2"][l]
    mw["w_branch"], mw["w_out"] = W["w_branch"][l], W["w_out"][l]
    if l % 2 == 0:
        mw["ffn"] = tuple(W[n][l // 2] for n in ("ffn_w1", "ffn_w3", "ffn_w2"))
    p["mw"] = mw
    p["mw16"] = jax.tree_util.tree_map(lambda a: a.astype(BF16), mw)
    p["q_norm"], p["kv_norm"] = W["mla_q_norm"][l], W["mla_kv_norm"][l]
    p["rwkv"] = dict(mu=W["rwkv_mu"][l], w0=W["rwkv_w0"][l], a0=W["rwkv_a0"][l], k_k=W["rwkv_k_k"][l],
                     k_a=W["rwkv_k_a"][l], ln_w=W["rwkv_ln_w"][l], ln_b=W["rwkv_ln_b"][l],
                     r_k=W["rwkv_r_k"][l].reshape(-1))
    lanes = jnp.arange(128)
    head_lane = (lanes >= DT_LANE) & (lanes < DT_LANE + SHEADS)
    idx = jnp.clip(lanes - DT_LANE, 0, SHEADS - 1)
    p["ssm"] = dict(conv_w=W["ssm_conv_w"][l], conv_b=W["ssm_conv_b"][l],
                    dtb128=jnp.where(head_lane, W["ssm_dt_bias"][l][idx], 0.0).reshape(1, 128),
                    a128=jnp.where(head_lane, -jnp.exp(W["ssm_a_log"][l][idx]), 0.0).reshape(1, 128),
                    dskip512=jnp.repeat(W["ssm_d"][l], SP).reshape(1, SDIM), ssm_norm=W["ssm_norm"][l])
    p["norm_attn"], p["norm_ffn"] = W["norm_attn"][l], W["norm_ffn"][l]
    p["w_ada"], p["b_ada"] = W["w_ada"][l], W["b_ada"][l]
    if l % 2 == 1:
        wr = jnp.pad(W["moe_router"][l // 2], ((0, 0), (0, 128 - NE)))
        wr_hi = wr.astype(BF16)
        p["router"] = (wr_hi, (wr - wr_hi.astype(F32)).astype(BF16),
                       jnp.pad(W["moe_router_b"][l // 2], (0, 128 - NE)).reshape(1, 128))
        p["moe"] = tuple(W[n][l // 2].astype(BF16) for n in ("moe_w1", "moe_w3", "moe_w2"))
    return p


def _rope_tables(pos):
    half = ROPE // 2
    freq = ROPE_THETA ** (-jnp.arange(half, dtype=F32) / half)
    ang = pos.astype(F32)[:, None] * freq[None, :]
    cos = jnp.concatenate([jnp.cos(ang)] * 2, axis=-1)
    sin = jnp.concatenate([jnp.sin(ang)] * 2, axis=-1)
    return cos, sin, jnp.tile(cos, (1, H)), jnp.tile(sin, (1, H))


def _trunk(x3, c, pos, paged, shift0, wkv0, conv0, ssm0, layers, norm_final):
    B, S, _ = x3.shape
    M = B * S
    g = _Group(B, S, 512)
    g_win = _Group(B, S, 256)
    x = x3.reshape(M, D_MODEL)
    tabs = _rope_tables(pos if S > 1 else jnp.broadcast_to(pos, (M,)))
    ri = jnp.arange(RDIM)
    ones_bd = ((ri[:, None] // RHD) == (ri[None, :] // RHD)).astype(BF16)
    outs = [[] for _ in range(6)]
    for l, p in enumerate(layers):
        mw = p["mw"] if g.precise else p["mw16"]
        ada = ada_matmul(c, p["w_ada"], p["b_ada"])
        sh_a, sc_a, g_a, sh_f, sc_f, g_f = jnp.split(ada, 6, axis=-1)
        seg_a, rw, z, xbc, gate = win_project(g_win, x, p["norm_attn"], sc_a, sh_a, mw["w_in"])

        qc, kc, ckv, kpe = mla_prep(g, seg_a, tabs, p["q_norm"], p["kv_norm"], mw["wuq"], mw["wuk"])
        if paged is None:
            a_out = mla_attention(B, S, qc, kc, mw["wuv"])
        else:
            cache_ckv, cache_kpe, page_table = paged
            a_out = paged_attention(page_table, qc, kc, cache_ckv[l], cache_kpe[l], mw["wuv"])

        rw3 = rw.reshape(B, S, RWKV_IN)
        prev = jnp.concatenate([shift0[l][:, None, :], rw3[:, :-1]], axis=1).reshape(M, RWKV_IN)
        r_t, w_t, k_t, v_t, kk_t, b_t, gate_r = rwkv_prep(g, rw, prev, p["rwkv"], mw, ones_bd)
        tc_len = min(RHD, S)
        nch = S // tc_len
        as_rows = lambda a: a.reshape(S, B * 4, 128)
        vt = v_t.reshape(nch, tc_len, B, 4, 2, RHD).transpose(0, 2, 3, 5, 4, 1)
        vt = jnp.pad(vt, ((0, 0),) * 5 + ((0, RHD - tc_len),)).reshape(nch, B * 4, RHD, 128).astype(g.act_dtype)
        s0 = wkv0[l].reshape(B, 4, 2, RHD, RHD).transpose(0, 1, 3, 2, 4).reshape(B * 4, RHD, 128)
        yt, s1 = rwkv_scan(B, S, as_rows(r_t), as_rows(w_t), as_rows(k_t), as_rows(kk_t), as_rows(b_t), vt, s0,
                           g.precise)
        y = yt.reshape(nch, B, 4, RHD, 2, RHD)[..., :tc_len].transpose(1, 0, 5, 2, 4, 3).reshape(M, RDIM)
        wkv1 = s1.reshape(B, 4, RHD, 2, RHD).transpose(0, 1, 3, 2, 4).reshape(B, 8, RHD, RHD)
        r_out = rwkv_post(g, y, r_t, k_t, v_t, gate_r, p["rwkv"], ones_bd)
        shift1 = rw3[:, -1]

        ps = p["ssm"]
        if S > 1:
            tail0 = jnp.pad(conv0[l], ((0, 0), (8 - (SCONV - 1), 0), (0, 0)))
            s_out, h_t = ssd_prompt(B, S, xbc, z, seg_a, tail0, ps)
            ssm1 = jnp.swapaxes(h_t, 2, 3)
            conv1 = xbc.reshape(B, S, SCD)[:, S - (SCONV - 1):]
        else:
            c0, c1, c2 = conv0[l][:, 0], conv0[l][:, 1], conv0[l][:, 2]
            act, xdt, dec128 = ssm_step_pre(g, xbc, c0, c1, c2, seg_a, ps)
            xb = jnp.broadcast_to(xdt.reshape(B, SHEADS, SP, 1), (B, SHEADS, SP, SN))
            dec = jnp.broadcast_to(dec128[:, DT_LANE:DT_LANE + SHEADS].reshape(B, SHEADS, 1, 1), (B, SHEADS, 1, SN))
            ssm1, y_s = ssm_step(B, ssm0[l], xb, dec, act[:, SDIM:SDIM + SGROUPS * SN], act[:, SDIM + SGROUPS * SN:])
            s_out = ssm_step_post(g, y_s, act[:, :SDIM], z, ps)
            conv1 = jnp.concatenate([conv0[l][:, 1:], xbc[:, None, :]], axis=1)

        x = merge_branches(g, a_out, r_out, s_out, gate, x, g_a, mw["w_branch"], mw["w_out"])

        if l % 2 == 0:
            w1, w3, w2 = mw["ffn"]
            x = dense_ffn(g, x, p["norm_ffn"], sc_f, sh_f, g_f, w1, w3, w2, w1.shape[1] // 2)
        else:
            w1, w3, w2 = p["moe"]
            x = moe_ffn(g, x, p["norm_ffn"], sc_f, sh_f, g_f, *p["router"], w1, w3, w2)
        for lst, v in zip(outs, (ckv.reshape(B, S, KVL), kpe.reshape(B, S, ROPE), shift1, wkv1, conv1, ssm1)):
            lst.append(v)
    y = final_norm(g, x, norm_final).reshape(B, S, D_MODEL)
    return y, [jnp.stack(v) for v in outs]


def kernel(x_prompt, x_sample, cache_ckv, cache_kpe, state_rwkv_shift, state_rwkv_wkv, state_ssm_conv, state_ssm, page_table, c_prompt, c_sample, w_ada, b_ada, norm_attn, norm_ffn, norm_final, w_in, mla_q_norm, mla_w_uq, mla_kv_norm, mla_w_uk, mla_w_uv, rwkv_mu, rwkv_w0, rwkv_w2, rwkv_a0, rwkv_a2, rwkv_g2, rwkv_k_k, rwkv_k_a, rwkv_r_k, rwkv_ln_w, rwkv_ln_b, ssm_conv_w, ssm_conv_b, ssm_dt_bias, ssm_a_log, ssm_d, ssm_norm, w_branch, w_out, ffn_w1, ffn_w3, ffn_w2, moe_router, moe_router_b, moe_w1, moe_w3, moe_w2):
    W = dict(w_ada=w_ada, b_ada=b_ada, norm_attn=norm_attn, norm_ffn=norm_ffn, w_in=w_in, mla_q_norm=mla_q_norm,
             mla_w_uq=mla_w_uq, mla_kv_norm=mla_kv_norm, mla_w_uk=mla_w_uk, mla_w_uv=mla_w_uv, rwkv_mu=rwkv_mu,
             rwkv_w0=rwkv_w0, rwkv_w2=rwkv_w2, rwkv_a0=rwkv_a0, rwkv_a2=rwkv_a2, rwkv_g2=rwkv_g2, rwkv_k_k=rwkv_k_k,
             rwkv_k_a=rwkv_k_a, rwkv_r_k=rwkv_r_k, rwkv_ln_w=rwkv_ln_w, rwkv_ln_b=rwkv_ln_b, ssm_conv_w=ssm_conv_w,
             ssm_conv_b=ssm_conv_b, ssm_dt_bias=ssm_dt_bias, ssm_a_log=ssm_a_log, ssm_d=ssm_d, ssm_norm=ssm_norm,
             w_branch=w_branch, w_out=w_out, ffn_w1=ffn_w1, ffn_w3=ffn_w3, ffn_w2=ffn_w2, moe_router=moe_router,
             moe_router_b=moe_router_b, moe_w1=moe_w1, moe_w3=moe_w3, moe_w2=moe_w2)
    depth = w_in.shape[0]
    layers = [_pack_layer(l, W) for l in range(depth)]
    bp, sp, _ = x_prompt.shape
    bs, ss, _ = x_sample.shape
    dt = x_prompt.dtype
    y_prompt, (p_ckv, p_kpe, p_shift, p_wkv, p_conv, p_ssm) = _trunk(
        x_prompt, c_prompt, jnp.arange(sp, dtype=jnp.int32), None,
        jnp.zeros((depth, bp, RWKV_IN), dt), jnp.zeros((depth, bp, 8, RHD, RHD), dt),
        jnp.zeros((depth, bp, SCONV - 1, SCD), dt), jnp.zeros((depth, bp, SHEADS, SP, SN), dt), layers, norm_final)
    p_ckv = p_ckv.reshape(depth, bp * sp // PAGE, PAGE, KVL)
    p_kpe = p_kpe.reshape(depth, bp * sp // PAGE, PAGE, ROPE)
    past_len = page_table.shape[1] * PAGE
    pos_s = past_len + jnp.arange(ss, dtype=jnp.int32)
    y_sample, (s_ckv, s_kpe, s_shift, s_wkv, s_conv, s_ssm) = _trunk(
        x_sample, c_sample, pos_s, (cache_ckv, cache_kpe, page_table),
        state_rwkv_shift, state_rwkv_wkv, state_ssm_conv, state_ssm, layers, norm_final)
    return (y_prompt, y_sample, p_ckv, p_kpe, p_shift, p_wkv, p_conv, p_ssm,
            s_ckv, s_kpe, s_shift, s_wkv, s_conv, s_ssm)
```

```python
import functools

import jax
import jax.numpy as jnp
from jax import lax
from jax.experimental import pallas as pl
from jax.experimental.pallas import tpu as pltpu

F32 = jnp.float32
BF16 = jnp.bfloat16

D_MODEL = 1024
DEPTH = 2
PAGE = 128
H = 8
NOPE = 64
ROPE = 32
VD = 64
QL = 256
KVL = 256
ROPE_THETA = 10000.0
MLA_SCALE = (NOPE + ROPE) ** -0.5
LOG2E = 1.4426950408889634
QK = 384
RDIM = 512
RHD = 64
DECAY_LORA = 64
AAA_LORA = 64
GATE_LORA = 128
RWKV_IN = 3 * RDIM + DECAY_LORA + AAA_LORA + GATE_LORA
RWKV_LN_EPS = 64e-5
SHEADS = 8
SP = 64
SDIM = 512
SGROUPS = 2
SN = 128
SCONV = 4
SCHUNK = 128
SCD = SDIM + 2 * SGROUPS * SN
NBRANCH = 3
NE = 8
TOPK = 2
EPS = 1e-6
SEG_A = 768
DT_TILE = 4
DT_LANE = 32
SEG_WIDTHS = (SEG_A, RWKV_IN, SDIM, SCD, NBRANCH * D_MODEL)
W_IN_COLS = sum(SEG_WIDTHS)
VMEM_LIMIT = 56 * 1024 * 1024


def _cparams(*sem):
    return pltpu.CompilerParams(dimension_semantics=sem, vmem_limit_bytes=VMEM_LIMIT)


def _dot(a, b):
    return jnp.dot(a, b, preferred_element_type=F32)


def _dot_nt(a, b):
    return lax.dot_general(a, b, (((1,), (1,)), ((), ())), preferred_element_type=F32)


def _hi_lo(x):
    hi = x.astype(BF16)
    return hi, (x.astype(F32) - hi.astype(F32)).astype(BF16)


def _dot3(a, w, dot=_dot):
    a_hi, a_lo = _hi_lo(a)
    w_hi, w_lo = _hi_lo(w)
    return dot(a_hi, w_hi) + dot(a_lo, w_hi) + dot(a_hi, w_lo)


def _mm(a, w, precise):
    return _dot3(a, w) if precise else _dot(a.astype(BF16), w)


def _split_dot(x, w01):
    hi = x.astype(BF16)
    lo = (x - hi.astype(F32)).astype(BF16)
    return _dot(hi, w01) + _dot(lo, w01)


def _split3_dot_left(w01, x):
    hi = x.astype(BF16)
    r1 = x - hi.astype(F32)
    mid = r1.astype(BF16)
    lo = (r1 - mid.astype(F32)).astype(BF16)
    return _dot(w01, hi) + _dot(w01, mid) + _dot(w01, lo)


def _sigmoid(x):
    return 1.0 / (1.0 + jnp.exp(-x))


def _silu(x):
    return x * _sigmoid(x)


def _softplus(x):
    return jnp.maximum(x, 0.0) + jnp.log(1.0 + jnp.exp(-jnp.abs(x)))


def _rms(x):
    return x * lax.rsqrt(jnp.mean(x * x, axis=-1, keepdims=True) + EPS)


class _Group:
    def __init__(self, B, S, tm):
        self.B, self.S, self.M = B, S, B * S
        self.precise = S == 1
        self.act_dtype = F32 if self.precise else BF16
        if S == 1:
            self.tm = min(tm, self.M)
            self.grid = (1, self.M // self.tm)
        else:
            self.tm = min(tm, S)
            self.grid = (B, S // self.tm)
        self.ns = self.grid[1]

    def rows(self, width, colblock=0):
        ns = self.ns
        return pl.BlockSpec((self.tm, width), lambda b, s: (b * ns + s, colblock))

    def full(self, shape):
        nd = len(shape)
        return pl.BlockSpec(shape, lambda b, s: (0,) * nd)

    def mod_array(self, m):
        return m.reshape(1, self.M, -1) if self.S == 1 else m.reshape(self.B, 1, -1)

    def mod_spec(self, width):
        if self.S == 1:
            return pl.BlockSpec((1, self.tm, width), lambda b, s: (0, s, 0))
        return pl.BlockSpec((1, 1, width), lambda b, s: (b, 0, 0))

    def pos_spec(self, width):
        if self.S == 1:
            return self.rows(width)
        return pl.BlockSpec((self.tm, width), lambda b, s: (s, 0))

    def tmaj_shape(self, width):
        return (self.M, width) if self.S == 1 else (self.S, self.B * width)

    def tmaj_spec(self, width):
        if self.S == 1:
            return self.rows(width)
        return pl.BlockSpec((self.tm, width), lambda b, s: (s, b))


def _ada_kernel(c_ref, w_ref, b_ref, o_ref):
    c = c_ref[...]
    o_ref[...] = _dot3(_silu(c), w_ref[...]) + b_ref[...]


def ada_matmul(c, w, b):
    m, k = c.shape
    n = w.shape[1]
    tn = 1024
    return pl.pallas_call(
        _ada_kernel,
        grid=(n // tn,),
        in_specs=[pl.BlockSpec((m, k), lambda j: (0, 0)), pl.BlockSpec((k, tn), lambda j: (0, j)),
                  pl.BlockSpec((1, tn), lambda j: (0, j))],
        out_specs=pl.BlockSpec((m, tn), lambda j: (0, j)),
        out_shape=jax.ShapeDtypeStruct((m, n), F32),
        compiler_params=_cparams("arbitrary"),
        name="ada_matmul",
    )(c, w, b.reshape(1, n))


def _win_kernel(x_ref, nw_ref, sc_ref, sh_ref, w_ref, *o_refs):
    h = (_rms(x_ref[...]) * nw_ref[...] * (1.0 + sc_ref[0]) + sh_ref[0]).astype(BF16)
    off = 0
    for o in o_refs:
        n = o.shape[-1]
        for c in range(0, n, 256):
            o[:, c:c + 256] = _dot(h, w_ref[:, off + c:off + c + 256])
        off += n


def _win_cols_kernel(x_ref, nw_ref, sc_ref, sh_ref, w_ref, o_ref):
    h = _rms(x_ref[...]) * nw_ref[...] * (1.0 + sc_ref[0]) + sh_ref[0]
    o_ref[...] = _dot3(h, w_ref[...])


def win_project_precise(g, x, nw, sc, sh, w_packed):
    tn = 512
    out = pl.pallas_call(
        _win_cols_kernel,
        grid=(W_IN_COLS // tn,),
        in_specs=[pl.BlockSpec((g.M, D_MODEL), lambda j: (0, 0)), pl.BlockSpec((1, D_MODEL), lambda j: (0, 0)),
                  pl.BlockSpec((1, g.M, D_MODEL), lambda j: (0, 0, 0)),
                  pl.BlockSpec((1, g.M, D_MODEL), lambda j: (0, 0, 0)),
                  pl.BlockSpec((D_MODEL, tn), lambda j: (0, j))],
        out_specs=pl.BlockSpec((g.M, tn), lambda j: (0, j)),
        out_shape=jax.ShapeDtypeStruct((g.M, W_IN_COLS), F32),
        compiler_params=_cparams("parallel"),
        name="win_project_precise",
    )(x, nw.reshape(1, -1), g.mod_array(sc), g.mod_array(sh), w_packed)
    offs = [0]
    for w in SEG_WIDTHS:
        offs.append(offs[-1] + w)
    return [out[:, offs[i]:offs[i + 1]] for i in range(len(SEG_WIDTHS))]


def win_project(g, x, nw, sc, sh, w_packed):
    if g.precise:
        return win_project_precise(g, x, nw, sc, sh, w_packed)
    return pl.pallas_call(
        _win_kernel,
        grid=g.grid,
        in_specs=[g.rows(D_MODEL), g.full((1, D_MODEL)), g.mod_spec(D_MODEL), g.mod_spec(D_MODEL),
                  g.full((D_MODEL, W_IN_COLS))],
        out_specs=[g.rows(w) for w in SEG_WIDTHS],
        out_shape=[jax.ShapeDtypeStruct((g.M, w), F32) for w in SEG_WIDTHS],
        compiler_params=_cparams("parallel", "parallel"),
        name="win_project",
    )(x, nw.reshape(1, -1), g.mod_array(sc), g.mod_array(sh), w_packed)


def _mla_prep_kernel(a_ref, cos_ref, sin_ref, cos8_ref, sin8_ref, qn_ref, kvn_ref, wuq_ref, wuk_ref,
                     qc_ref, kc_ref, ckv_ref, kpe_ref, *, precise):
    a = a_ref[...]
    tm = a.shape[0]
    odt = qc_ref.dtype
    qn = _rms(a[:, 0:QL]) * qn_ref[...]
    qa = _mm(qn, wuq_ref[...], precise)
    q_rope = qa[:, 512:768] * cos8_ref[...] + qa[:, 768:1024] * sin8_ref[...]
    zpad = jnp.zeros((tm, QK - KVL - ROPE), odt)
    qscale = MLA_SCALE if precise else MLA_SCALE * LOG2E
    for h in range(H):
        q_abs = _mm(qa[:, h * NOPE:(h + 1) * NOPE], wuk_ref[h], precise) * qscale
        qc_ref[h, :, 0:KVL] = q_abs.astype(odt)
        qc_ref[h, :, KVL:KVL + ROPE] = (q_rope[:, h * ROPE:(h + 1) * ROPE] * qscale).astype(odt)
        qc_ref[h, :, KVL + ROPE:QK] = zpad
    ckv = _rms(a[:, QL:QL + KVL]) * kvn_ref[...]
    kpe = a[:, 512:544] * cos_ref[...] + a[:, 552:584] * sin_ref[...]
    ckv_ref[...] = ckv
    kpe_ref[...] = kpe
    kc_ref[:, 0:KVL] = ckv.astype(odt)
    kc_ref[:, KVL:KVL + ROPE] = kpe.astype(odt)
    kc_ref[:, KVL + ROPE:QK] = zpad


def mla_prep(g, seg_a, tabs, qn, kvn, wuq, wuk):
    cos, sin, cos8, sin8 = tabs
    ns = g.ns
    return pl.pallas_call(
        functools.partial(_mla_prep_kernel, precise=g.precise),
        grid=g.grid,
        in_specs=[g.rows(SEG_A), g.pos_spec(ROPE), g.pos_spec(ROPE), g.pos_spec(H * ROPE), g.pos_spec(H * ROPE),
                  g.full((1, QL)), g.full((1, KVL)), g.full((QL, 1024)), g.full((H, NOPE, KVL))],
        out_specs=[pl.BlockSpec((H, g.tm, QK), lambda b, s: (0, b * ns + s, 0)), g.rows(QK), g.rows(KVL),
                   g.rows(ROPE)],
        out_shape=[jax.ShapeDtypeStruct((H, g.M, QK), g.act_dtype), jax.ShapeDtypeStruct((g.M, QK), g.act_dtype),
                   jax.ShapeDtypeStruct((g.M, KVL), F32), jax.ShapeDtypeStruct((g.M, ROPE), F32)],
        compiler_params=_cparams("parallel", "parallel"),
        name="mla_prep",
    )(seg_a, cos, sin, cos8, sin8, qn.reshape(1, -1), kvn.reshape(1, -1), wuq, wuk)


NEG = -1e30


def _attn_kernel(q_ref, k_ref, wuv_ref, o_ref, m_scr, l_scr, a_scr, acc_scr, s_scr, p_scr, *, tq, tk):
    qi = pl.program_id(1)
    q = q_ref[...].reshape(H * tq, QK)
    m_scr[...] = jnp.full(m_scr.shape, NEG, F32)
    l_scr[...] = jnp.zeros(l_scr.shape, F32)
    acc_scr[...] = jnp.zeros(acc_scr.shape, F32)
    reps = tk // 128
    wide = lambda a: jnp.concatenate([a] * reps, axis=-1)

    def keys(j):
        return k_ref[pl.ds(pl.multiple_of(j * tk, tk), tk), :]

    def scores(j, slot):
        s_scr[slot] = _dot_nt(q, keys(j))

    def softmax_pv(j, slot, masked):
        if masked:
            visible = (j * tk + lax.broadcasted_iota(jnp.int32, (tq, tk), 1)
                       <= qi * tq + lax.broadcasted_iota(jnp.int32, (tq, tk), 0))
        for h in range(H):
            rs = pl.ds(h * tq, tq)
            s = s_scr[slot, rs, :]
            if masked:
                s = jnp.where(visible, s, NEG)
            m_prev = m_scr[rs, :]
            m_new = jnp.maximum(m_prev, jnp.max(s, axis=-1, keepdims=True))
            alpha = jnp.exp2(m_prev - m_new)
            p = jnp.exp2(s - wide(m_new))
            l_scr[rs, :] = alpha * l_scr[rs, :] + jnp.sum(p, axis=-1, keepdims=True)
            m_scr[rs, :] = m_new
            a_scr[rs, :] = alpha
            p_scr[slot, rs, :] = p.astype(BF16)
        alpha = a_scr[...]
        acc_scr[...] = (acc_scr[...] * jnp.concatenate([alpha] * (KVL // 128), axis=-1)
                        + _dot(p_scr[slot], keys(j)[:, 0:KVL]))

    n_full = (qi * tq) // tk

    def pair(j, last_masked):
        scores(j, 0)
        scores(j + 1, 1)
        softmax_pv(j, 0, False)
        softmax_pv(j + 1, 1, last_masked)

    def body(i, carry):
        pair(2 * i, False)
        return carry

    lax.fori_loop(0, n_full // 2, body, 0)

    @pl.when(n_full % 2 == 1)
    def _():
        pair(n_full - 1, True)

    @pl.when(n_full % 2 == 0)
    def _():
        scores(n_full, 0)
        softmax_pv(n_full, 0, True)
    inv_l = 1.0 / l_scr[...]
    o = acc_scr[...] * jnp.concatenate([inv_l] * (KVL // 128), axis=-1)
    for h in range(H):
        o_ref[:, h * VD:(h + 1) * VD] = _dot(o[h * tq:(h + 1) * tq].astype(BF16), wuv_ref[h]).astype(o_ref.dtype)


def mla_attention(B, S, qc, kc, wuv):
    tq = min(128, S)
    tk = min(256, S)
    nq = S // tq
    rows = H * tq
    return pl.pallas_call(
        functools.partial(_attn_kernel, tq=tq, tk=tk),
        grid=(B, nq),
        in_specs=[pl.BlockSpec((H, tq, QK), lambda b, i: (0, b * nq + i, 0)),
                  pl.BlockSpec((S, QK), lambda b, i: (b, 0)),
                  pl.BlockSpec((H, KVL, VD), lambda b, i: (0, 0, 0))],
        out_specs=pl.BlockSpec((tq, H * VD), lambda b, i: (b * nq + i, 0)),
        out_shape=jax.ShapeDtypeStruct((B * S, H * VD), BF16),
        scratch_shapes=[pltpu.VMEM((rows, 128), F32), pltpu.VMEM((rows, 128), F32), pltpu.VMEM((rows, 128), F32),
                        pltpu.VMEM((rows, KVL), F32), pltpu.VMEM((2, rows, tk), F32),
                        pltpu.VMEM((2, rows, tk), BF16)],
        compiler_params=_cparams("parallel", "parallel"),
        name="mla_attention",
    )(qc, kc, wuv)


PAGES_PER_STEP = 16


def _paged_kernel(pt_ref, q_ref, knew_ref, wuv_ref, *rest):
    pp = PAGES_PER_STEP
    ckv_refs, kpe_refs = rest[:pp], rest[pp:2 * pp]
    o_ref, m_scr, l_scr, acc_scr = rest[2 * pp:]
    j = pl.program_id(1)
    q = q_ref[...]

    @pl.when(j == 0)
    def _():
        m_scr[...] = jnp.full(m_scr.shape, NEG, F32)
        l_scr[...] = jnp.zeros(l_scr.shape, F32)
        acc_scr[...] = jnp.zeros(acc_scr.shape, F32)

    def stack_hi_lo(x):
        hi = x.astype(BF16).astype(F32)
        return jnp.concatenate([hi, x - hi], axis=0).astype(BF16)

    qa2 = stack_hi_lo(q[:, 0:KVL])
    qp2 = stack_hi_lo(q[:, KVL:KVL + ROPE])

    def chain(refs_c, refs_p):
        ckv_hi, ckv_lo = _hi_lo(jnp.concatenate([r[...] for r in refs_c], axis=0))
        kpe_hi, kpe_lo = _hi_lo(jnp.concatenate([r[...] for r in refs_p], axis=0))
        s2 = _dot_nt(qa2, ckv_hi) + _dot_nt(qp2, kpe_hi)
        s = s2[0:H] + s2[H:2 * H] + _dot_nt(qa2[0:H], ckv_lo) + _dot_nt(qp2[0:H], kpe_lo)
        m = jnp.max(s, axis=-1, keepdims=True)
        p = jnp.exp(s - m)
        p2 = stack_hi_lo(p)
        pv2 = _dot(p2, ckv_hi)
        return m, jnp.sum(p, axis=-1, keepdims=True), pv2[0:H] + pv2[H:2 * H] + _dot(p2[0:H], ckv_lo)

    half = pp // 2
    parts = [chain(ckv_refs[i * half:(i + 1) * half], kpe_refs[i * half:(i + 1) * half]) for i in range(2)]
    m_prev = m_scr[...]
    m_new = jnp.maximum(m_prev, jnp.maximum(parts[0][0], parts[1][0]))
    alpha = jnp.exp(m_prev - m_new)
    l = alpha * l_scr[...]
    acc = alpha * acc_scr[...]
    for m_i, l_i, pv_i in parts:
        w_i = jnp.exp(m_i - m_new)
        l = l + w_i * l_i
        acc = acc + w_i * pv_i
    l_scr[...] = l
    acc_scr[...] = acc
    m_scr[...] = m_new

    @pl.when(j == pl.num_programs(1) - 1)
    def _():
        kn = knew_ref[0]
        s_new = jnp.sum(q * kn, axis=-1, keepdims=True)
        m_prev = m_scr[...]
        m_new = jnp.maximum(m_prev, s_new)
        alpha = jnp.exp(m_prev - m_new)
        p_new = jnp.exp(s_new - m_new)
        l = alpha * l_scr[...] + p_new
        acc = alpha * acc_scr[...] + p_new * kn[:, 0:KVL]
        o = acc / l
        for h in range(H):
            o_ref[0, :, h * VD:(h + 1) * VD] = _dot3(o[h:h + 1], wuv_ref[h])


def paged_attention(layer, page_table, qc, kc_new, cache_ckv, cache_kpe, wuv):
    B, n_pages = page_table.shape
    pp = PAGES_PER_STEP
    nsteps = n_pages // pp

    def page_spec(i, width):
        return pl.BlockSpec((None, None, PAGE, width), lambda b, j, pt: (layer, pt[b, j * pp + i], 0, 0))

    grid_spec = pltpu.PrefetchScalarGridSpec(
        num_scalar_prefetch=1,
        grid=(B, nsteps),
        in_specs=[pl.BlockSpec((None, H, QK), lambda b, j, pt: (b, 0, 0)),
                  pl.BlockSpec((1, 1, QK), lambda b, j, pt: (b, 0, 0)),
                  pl.BlockSpec((H, KVL, VD), lambda b, j, pt: (0, 0, 0))]
        + [page_spec(i, KVL) for i in range(pp)] + [page_spec(i, ROPE) for i in range(pp)],
        out_specs=pl.BlockSpec((1, 1, H * VD), lambda b, j, pt: (b, 0, 0)),
        scratch_shapes=[pltpu.VMEM((H, 1), F32), pltpu.VMEM((H, 1), F32), pltpu.VMEM((H, KVL), F32)],
    )
    out = pl.pallas_call(
        _paged_kernel,
        grid_spec=grid_spec,
        out_shape=jax.ShapeDtypeStruct((B, 1, H * VD), F32),
        compiler_params=_cparams("parallel", "arbitrary"),
        name="paged_attention",
    )(page_table, jnp.transpose(qc, (1, 0, 2)), kc_new.reshape(B, 1, QK), wuv, *([cache_ckv] * pp),
      *([cache_kpe] * pp))
    return out.reshape(B, H * VD)


def _rwkv_prep_kernel(rw_ref, prev_ref, mu_ref, w0_ref, a0_ref, kk_ref_, ka_ref, w2_ref, a2_ref, g2_ref, ones_ref,
                      r_o, w_o, k_o, v_o, kk_o, b_o, g_o, *rest, precise, seq):
    rw = rw_ref[...]
    if seq:
        vt_o, carry = rest
        tm = rw.shape[0]

        @pl.when(pl.program_id(1) == 0)
        def _():
            carry[...] = prev_ref[0]

        row8 = lax.broadcasted_iota(jnp.int32, (8, RWKV_IN), 0)
        shifted = pltpu.roll(rw, 1, 0)
        top = jnp.where(row8 == 0, pltpu.roll(carry[...], 1, 0), shifted[0:8])
        prev = jnp.concatenate([top, shifted[8:]], axis=0)
        carry[...] = rw[tm - 8:tm]
    else:
        prev = prev_ref[...]
    xs = rw + (prev - rw) * mu_ref[...]
    r = xs[:, 0:RDIM]
    k = xs[:, RDIM:2 * RDIM]
    v = xs[:, 2 * RDIM:3 * RDIM]
    o = 3 * RDIM
    wl = xs[:, o:o + DECAY_LORA]
    al = xs[:, o + DECAY_LORA:o + DECAY_LORA + AAA_LORA]
    gl = xs[:, o + DECAY_LORA + AAA_LORA:RWKV_IN]
    w = -_softplus(-(w0_ref[...] + _mm(jnp.tanh(wl), w2_ref[...], precise))) - 0.5
    a = _sigmoid(a0_ref[...] + _mm(al, a2_ref[...], precise))
    kk = k * kk_ref_[...]
    kk = kk * lax.rsqrt(jnp.maximum(_split_dot(kk * kk, ones_ref[...]), 1e-24))
    r_o[...] = r
    w_o[...] = jnp.exp(-jnp.exp(w))
    k_o[...] = k * (1.0 + (a - 1.0) * ka_ref[...])
    v_o[...] = v
    kk_o[...] = kk
    b_o[...] = kk * a
    g_o[...] = _mm(_sigmoid(gl), g2_ref[...], precise)
    if seq:
        ri = lax.broadcasted_iota(jnp.int32, (SCAN_W, SCAN_W), 0)
        ci = lax.broadcasted_iota(jnp.int32, (SCAN_W, SCAN_W), 1)
        same_head = (ri // RHD) == (ci // RHD)
        eye_rep = (lax.broadcasted_iota(jnp.int32, (RHD, SCAN_W), 0)
                   == lax.broadcasted_iota(jnp.int32, (RHD, SCAN_W), 1) % RHD).astype(BF16)
        for c in range(tm // RHD):
            for t in range(SCAN_TPB):
                v_c = v[c * RHD:(c + 1) * RHD, t * SCAN_W:(t + 1) * SCAN_W].astype(BF16)
                blockdiag = jnp.where(same_head, jnp.concatenate([v_c] * SCAN_HPT, axis=0), jnp.zeros((), BF16))
                vt_o[c, t] = _dot_nt(eye_rep, blockdiag).astype(vt_o.dtype)


def rwkv_prep(g, rw, prev, p, mw, ones_bd):
    vec = lambda a: a.reshape(1, -1)
    tshape = jax.ShapeDtypeStruct(g.tmaj_shape(RDIM), F32)
    seq = g.S > 1
    out_specs = [g.tmaj_spec(RDIM)] * 6 + [g.rows(RDIM)]
    out_shape = [tshape] * 6 + [jax.ShapeDtypeStruct((g.M, RDIM), F32)]
    scratch = []
    if seq:
        assert g.tm % RHD == 0
        cpt = g.tm // RHD
        prev = jnp.pad(prev[:, None, :], ((0, 0), (7, 0), (0, 0)))
        prev_spec = pl.BlockSpec((1, 8, RWKV_IN), lambda b, s: (b, 0, 0))
        out_specs.append(pl.BlockSpec((cpt, SCAN_TPB, RHD, SCAN_W), lambda b, s: (s, b, 0, 0)))
        out_shape.append(jax.ShapeDtypeStruct((g.S // RHD, g.B * SCAN_TPB, RHD, SCAN_W), BF16))
        scratch = [pltpu.VMEM((8, RWKV_IN), F32)]
    else:
        prev_spec = g.rows(RWKV_IN)
    return pl.pallas_call(
        functools.partial(_rwkv_prep_kernel, precise=g.precise, seq=seq),
        grid=g.grid,
        in_specs=[g.rows(RWKV_IN), prev_spec, g.full((1, RWKV_IN)), g.full((1, RDIM)), g.full((1, RDIM)),
                  g.full((1, RDIM)), g.full((1, RDIM)), g.full((DECAY_LORA, RDIM)), g.full((AAA_LORA, RDIM)),
                  g.full((GATE_LORA, RDIM)), g.full((RDIM, RDIM))],
        out_specs=out_specs,
        out_shape=out_shape,
        scratch_shapes=scratch,
        compiler_params=_cparams("parallel", "arbitrary"),
        name="rwkv_prep",
    )(rw, prev, vec(p["mu"]), vec(p["w0"]), vec(p["a0"]), vec(p["k_k"]), vec(p["k_a"]), mw["w2"], mw["a2"], mw["g2"],
      ones_bd)


SCAN_NB = 16
SCAN_HPT = 4
SCAN_W = SCAN_HPT * RHD
SCAN_TPB = RDIM // SCAN_W
SCAN_GB = 4


def _scan_kernel(r_ref, w_ref, k_ref, kk_ref, b_ref, vt_ref, s0_ref, yt_ref, sout_ref, s_scr, *, tc_len, ng, precise):
    tc = pl.program_id(1)

    def pick(x, w01):
        return _split_dot(x, w01) if precise else _dot(x.astype(BF16), w01)

    @pl.when(tc == 0)
    def _():
        s_scr[...] = s0_ref[...]

    ri = lax.broadcasted_iota(jnp.int32, (SCAN_W, SCAN_W), 0)
    ci = lax.broadcasted_iota(jnp.int32, (SCAN_W, SCAN_W), 1)
    same_head = (ri // RHD) == (ci // RHD)
    ones_bd = same_head.astype(BF16)
    lane = lax.broadcasted_iota(jnp.int32, (RHD, SCAN_W), 1) % RHD
    yt_ref[...] = jnp.zeros(yt_ref.shape, F32)
    gb = min(SCAN_GB, ng)

    def step(tt, carry):
        e_t = (same_head & ((ri % RHD) == tt)).astype(BF16)
        sel = lane == tt

        def rows(ref, g0):
            return jnp.stack([jnp.broadcast_to(ref[tt, pl.ds(g0 + i, 1), :], (RHD, SCAN_W)) for i in range(gb)])

        def issue(g0):
            sa = pick((s_scr[g0:g0 + gb] * rows(kk_ref, g0)).reshape(gb * RHD, SCAN_W), ones_bd)
            vcol = pick(vt_ref[0, g0:g0 + gb].reshape(gb * RHD, SCAN_W), e_t)
            return sa.reshape(gb, RHD, SCAN_W), vcol.reshape(gb, RHD, SCAN_W)

        pend = issue(0)
        y_pend = None
        for g0 in range(0, ng, gb):
            nxt = issue(g0 + gb) if g0 + gb < ng else None
            sa, vcol = pend
            s = s_scr[g0:g0 + gb] * rows(w_ref, g0) - sa * rows(b_ref, g0) + vcol * rows(k_ref, g0)
            s_scr[g0:g0 + gb] = s
            y = pick((s * rows(r_ref, g0)).reshape(gb * RHD, SCAN_W), ones_bd).reshape(gb, RHD, SCAN_W)
            if y_pend is not None:
                gp, yp = y_pend
                yt_ref[0, gp:gp + gb] = jnp.where(sel, yp, yt_ref[0, gp:gp + gb])
            y_pend = (g0, y)
            pend = nxt
        gp, yp = y_pend
        yt_ref[0, gp:gp + gb] = jnp.where(sel, yp, yt_ref[0, gp:gp + gb])
        return carry

    lax.fori_loop(0, tc_len, step, 0)

    @pl.when(tc == pl.num_programs(1) - 1)
    def _():
        sout_ref[...] = s_scr[...]


def rwkv_scan(B, S, r, w, k, kk, b, vt, s0, precise):
    tc_len = min(RHD, S)
    nb = min(SCAN_NB, B)
    ng = nb * SCAN_TPB
    nchunks = S // tc_len
    row_spec = pl.BlockSpec((tc_len, ng, SCAN_W), lambda bg, c: (c, bg, 0))
    st_spec = pl.BlockSpec((ng, RHD, SCAN_W), lambda bg, c: (bg, 0, 0))
    ch_spec = pl.BlockSpec((1, ng, RHD, SCAN_W), lambda bg, c: (c, bg, 0, 0))
    return pl.pallas_call(
        functools.partial(_scan_kernel, tc_len=tc_len, ng=ng, precise=precise),
        grid=(B // nb, nchunks),
        in_specs=[row_spec] * 5 + [ch_spec, st_spec],
        out_specs=[ch_spec, st_spec],
        out_shape=[jax.ShapeDtypeStruct((nchunks, B * SCAN_TPB, RHD, SCAN_W), F32),
                   jax.ShapeDtypeStruct((B * SCAN_TPB, RHD, SCAN_W), F32)],
        scratch_shapes=[pltpu.VMEM((ng, RHD, SCAN_W), F32)],
        compiler_params=_cparams("parallel", "arbitrary"),
        name="rwkv_scan",
    )(r, w, k, kk, b, vt, s0)


def _rwkv_post_kernel(y_ref, r_ref, k_ref, v_ref, g_ref, lnw_ref, lnb_ref, rk_ref, ones_ref, o_ref):
    ones = ones_ref[...]
    y = y_ref[...]
    mu = _split_dot(y, ones) * (1.0 / RHD)
    yc = y - mu
    var = _split_dot(yc * yc, ones) * (1.0 / RHD)
    yn = yc * lax.rsqrt(var + RWKV_LN_EPS) * lnw_ref[...] + lnb_ref[...]
    v = v_ref[...]
    bonus = _split_dot(r_ref[...] * k_ref[...] * rk_ref[...], ones)
    o_ref[...] = ((yn + bonus * v) * g_ref[...]).astype(o_ref.dtype)


def rwkv_post(g, y, r, k, v, gate, p, ones_bd):
    vec = lambda a: a.reshape(1, -1)
    return pl.pallas_call(
        _rwkv_post_kernel,
        grid=g.grid,
        in_specs=[g.rows(RDIM), g.tmaj_spec(RDIM), g.tmaj_spec(RDIM), g.tmaj_spec(RDIM), g.rows(RDIM),
                  g.full((1, RDIM)), g.full((1, RDIM)), g.full((1, RDIM)), g.full((RDIM, RDIM))],
        out_specs=g.rows(RDIM),
        out_shape=jax.ShapeDtypeStruct((g.M, RDIM), g.act_dtype),
        compiler_params=_cparams("parallel", "parallel"),
        name="rwkv_post",
    )(y, r, k, v, gate, vec(p["ln_w"]), vec(p["ln_b"]), vec(p["r_k"]), ones_bd)


def _ssm_gate_norm(y, z, nw):
    y = y * _silu(z)
    gw = SDIM // SGROUPS
    parts = [_rms(y[:, i * gw:(i + 1) * gw]) for i in range(SGROUPS)]
    return jnp.concatenate(parts, axis=-1) * nw


def _ssd_kernel(xbc_ref, z_ref, dt_ref, tail0_ref, cw_ref, cb_ref, dtb_ref, a_ref, dsk_ref, nw_ref,
                o_ref, hout_ref, tail_scr, h_scr, y_scr):
    c = pl.program_id(1)
    L = SCHUNK

    @pl.when(c == 0)
    def _():
        tail_scr[...] = tail0_ref[0]
        h_scr[...] = jnp.zeros(h_scr.shape, F32)

    xbc = xbc_ref[...]
    tail = tail_scr[...]
    row8 = lax.broadcasted_iota(jnp.int32, (8, SCD), 0)
    conv = cb_ref[...] + xbc * cw_ref[SCONV - 1:SCONV, :]
    for sft in range(1, SCONV):
        sh = pltpu.roll(xbc, sft, 0)
        top = jnp.where(row8 < sft, pltpu.roll(tail, sft, 0), sh[0:8])
        sh = jnp.concatenate([top, sh[8:]], axis=0)
        conv = conv + sh * cw_ref[SCONV - 1 - sft:SCONV - sft, :]
    tail_scr[...] = xbc[L - 8:L]
    act = _silu(conv)
    xa = act[:, 0:SDIM]
    lane = lax.broadcasted_iota(jnp.int32, (L, 128), 1)
    dt_valid = (lane >= DT_LANE) & (lane < DT_LANE + SHEADS)
    dtt = jnp.where(dt_valid, _softplus(dt_ref[...] + dtb_ref[...]), 0.0)
    adt = dtt * a_ref[...]
    ri = lax.broadcasted_iota(jnp.int32, (L, L), 0)
    ci = lax.broadcasted_iota(jnp.int32, (L, L), 1)
    causal = ri >= ci
    cs = _split3_dot_left(causal.astype(BF16), adt)
    cs_t = cs.T
    total = cs[L - 1:L, :]
    for grp in range(SGROUPS):
        bm = act[:, SDIM + grp * SN:SDIM + (grp + 1) * SN]
        cm = act[:, SDIM + SGROUPS * SN + grp * SN:SDIM + SGROUPS * SN + (grp + 1) * SN]
        cb = _dot_nt(cm.astype(BF16), bm.astype(BF16))
        for hh in range(SHEADS // SGROUPS):
            h = grp * (SHEADS // SGROUPS) + hh
            ln = DT_LANE + h
            col = cs[:, ln:ln + 1]
            row = cs_t[ln:ln + 1, :]
            tot = total[:, ln:ln + 1]
            lmat = jnp.where(causal, jnp.exp(jnp.where(causal, col - row, 0.0)), 0.0)
            xh = xa[:, h * SP:(h + 1) * SP]
            xdt = (xh * dtt[:, ln:ln + 1]).astype(BF16)
            hprev = h_scr[h]
            y = _dot((cb * lmat).astype(BF16), xdt)
            y = y + _dot((cm * jnp.exp(col)).astype(BF16), hprev.astype(BF16))
            y_scr[:, h * SP:(h + 1) * SP] = y + dsk_ref[:, h * SP:(h + 1) * SP] * xh
            bdec = (bm * jnp.exp(tot - col)).T.astype(BF16)
            h_scr[h] = hprev * jnp.exp(tot) + _dot(bdec, xdt)
    o_ref[...] = _ssm_gate_norm(y_scr[...], z_ref[...], nw_ref[...]).astype(o_ref.dtype)

    @pl.when(c == pl.num_programs(1) - 1)
    def _():
        hout_ref[0] = h_scr[...]


def ssd_prompt(B, S, xbc, z, seg_a, tail0, p):
    nc = S // SCHUNK
    L = SCHUNK
    full = lambda shape: pl.BlockSpec(shape, lambda b, c: (0,) * len(shape))
    return pl.pallas_call(
        _ssd_kernel,
        grid=(B, nc),
        in_specs=[pl.BlockSpec((L, SCD), lambda b, c: (b * nc + c, 0)),
                  pl.BlockSpec((L, SDIM), lambda b, c: (b * nc + c, 0)),
                  pl.BlockSpec((L, 128), lambda b, c: (b * nc + c, DT_TILE)),
                  pl.BlockSpec((1, 8, SCD), lambda b, c: (b, 0, 0)),
                  full((SCONV, SCD)), full((1, SCD)), full((1, 128)), full((1, 128)), full((1, SDIM)),
                  full((1, SDIM))],
        out_specs=[pl.BlockSpec((L, SDIM), lambda b, c: (b * nc + c, 0)),
                   pl.BlockSpec((1, SHEADS, SN, SP), lambda b, c: (b, 0, 0, 0))],
        out_shape=[jax.ShapeDtypeStruct((B * S, SDIM), BF16), jax.ShapeDtypeStruct((B, SHEADS, SN, SP), F32)],
        scratch_shapes=[pltpu.VMEM((8, SCD), F32), pltpu.VMEM((SHEADS, SN, SP), F32), pltpu.VMEM((L, SDIM), F32)],
        compiler_params=_cparams("parallel", "arbitrary"),
        name="ssd_prompt",
    )(xbc, z, seg_a, tail0, p["conv_w"], p["conv_b"].reshape(1, -1), p["dtb128"], p["a128"], p["dskip512"],
      p["ssm_norm"].reshape(1, -1))


def _ssm_step_pre_kernel(xbc_ref, c0_ref, c1_ref, c2_ref, dt_ref, cw_ref, cb_ref, dtb_ref, a_ref,
                         act_ref, xdt_ref, dec_ref):
    conv = (cb_ref[...] + c0_ref[...] * cw_ref[0:1, :] + c1_ref[...] * cw_ref[1:2, :] + c2_ref[...] * cw_ref[2:3, :]
            + xbc_ref[...] * cw_ref[3:4, :])
    act = _silu(conv)
    act_ref[...] = act
    dtt = _softplus(dt_ref[...] + dtb_ref[...])
    dec_ref[...] = jnp.exp(dtt * a_ref[...])
    for h in range(SHEADS):
        xdt_ref[:, h * SP:(h + 1) * SP] = act[:, h * SP:(h + 1) * SP] * dtt[:, DT_LANE + h:DT_LANE + h + 1]


def ssm_step_pre(g, xbc, c0, c1, c2, seg_a, p):
    return pl.pallas_call(
        _ssm_step_pre_kernel,
        grid=g.grid,
        in_specs=[g.rows(SCD)] * 4 + [g.rows(128, DT_TILE), g.full((SCONV, SCD)), g.full((1, SCD)),
                                      g.full((1, 128)), g.full((1, 128))],
        out_specs=[g.rows(SCD), g.rows(SDIM), g.rows(128)],
        out_shape=[jax.ShapeDtypeStruct((g.M, SCD), F32), jax.ShapeDtypeStruct((g.M, SDIM), F32),
                   jax.ShapeDtypeStruct((g.M, 128), F32)],
        compiler_params=_cparams("parallel", "parallel"),
        name="ssm_step_pre",
    )(xbc, c0, c1, c2, seg_a, p["conv_w"], p["conv_b"].reshape(1, -1), p["dtb128"], p["a128"])


SSM_STEP_BT = 8


def _ssm_step_kernel(h0_ref, xb_ref, dec_ref, bm_ref, cm_ref, h1_ref, y_ref):
    bt = SSM_STEP_BT
    rp = lax.broadcasted_iota(jnp.int32, (SP, 128), 0)
    lp = lax.broadcasted_iota(jnp.int32, (SP, 128), 1)
    pick = [lp == rp, lp == rp + SP]
    hpg = SHEADS // SGROUPS
    for i in range(bt):
        for hp in range(SHEADS // 2):
            yrow = jnp.zeros((1, 128), F32)
            for e in range(2):
                h = hp * 2 + e
                grp = h // hpg
                bm = bm_ref[i:i + 1, grp * SN:(grp + 1) * SN]
                cm = cm_ref[i:i + 1, grp * SN:(grp + 1) * SN]
                h1 = h0_ref[i, h] * dec_ref[i, h] + xb_ref[i, h] * bm
                h1_ref[i, h] = h1
                ycol = jnp.sum(h1 * cm, axis=-1, keepdims=True)
                yrow = yrow + jnp.sum(jnp.where(pick[e], ycol, 0.0), axis=0, keepdims=True)
            y_ref[i:i + 1, hp * 128:(hp + 1) * 128] = yrow


def ssm_step(B, layer, h0, xb, dec, bm, cm):
    bt = SSM_STEP_BT
    st = pl.BlockSpec((bt, SHEADS, SP, SN), lambda i: (i, 0, 0, 0))
    return pl.pallas_call(
        _ssm_step_kernel,
        grid=(B // bt,),
        in_specs=[pl.BlockSpec((None, bt, SHEADS, SP, SN), lambda i: (layer, i, 0, 0, 0)), st,
                  pl.BlockSpec((bt, SHEADS, 1, SN), lambda i: (i, 0, 0, 0)),
                  pl.BlockSpec((bt, SGROUPS * SN), lambda i: (i, 0)), pl.BlockSpec((bt, SGROUPS * SN), lambda i: (i, 0))],
        out_specs=[st, pl.BlockSpec((bt, SDIM), lambda i: (i, 0))],
        out_shape=[jax.ShapeDtypeStruct((B, SHEADS, SP, SN), F32), jax.ShapeDtypeStruct((B, SDIM), F32)],
        compiler_params=_cparams("parallel"),
        name="ssm_step",
    )(h0, xb, dec, bm, cm)


def _ssm_step_post_kernel(y_ref, x_ref, z_ref, dsk_ref, nw_ref, o_ref):
    y = y_ref[...] + dsk_ref[...] * x_ref[...]
    o_ref[...] = _ssm_gate_norm(y, z_ref[...], nw_ref[...]).astype(o_ref.dtype)


def ssm_step_post(g, y, act, z, p):
    return pl.pallas_call(
        _ssm_step_post_kernel,
        grid=g.grid,
        in_specs=[g.rows(SDIM), g.rows(SDIM), g.rows(SDIM), g.full((1, SDIM)), g.full((1, SDIM))],
        out_specs=g.rows(SDIM),
        out_shape=jax.ShapeDtypeStruct((g.M, SDIM), g.act_dtype),
        compiler_params=_cparams("parallel", "parallel"),
        name="ssm_step_post",
    )(y, act, z, p["dskip512"], p["ssm_norm"].reshape(1, -1))


def _merge_kernel(a_ref, r_ref, s_ref, gate_ref, x_ref, ga_ref, wb_ref, wo_ref, o_ref, *, precise):
    acc = None
    for i, br in enumerate((a_ref, r_ref, s_ref)):
        t = _sigmoid(gate_ref[:, i * D_MODEL:(i + 1) * D_MODEL]) * _mm(br[...], wb_ref[i], precise)
        acc = t if acc is None else acc + t
    mix = _mm(acc, wo_ref[...], precise)
    o_ref[...] = x_ref[...] + ga_ref[0] * mix


def merge_branches(g, a_out, r_out, s_out, gate, x, ga, wb, wo):
    return pl.pallas_call(
        functools.partial(_merge_kernel, precise=g.precise),
        grid=g.grid,
        in_specs=[g.rows(512), g.rows(512), g.rows(512), g.rows(NBRANCH * D_MODEL), g.rows(D_MODEL),
                  g.mod_spec(D_MODEL), g.full((NBRANCH, 512, D_MODEL)), g.full((D_MODEL, D_MODEL))],
        out_specs=g.rows(D_MODEL),
        out_shape=jax.ShapeDtypeStruct((g.M, D_MODEL), F32),
        compiler_params=_cparams("parallel", "parallel"),
        name="merge_branches",
    )(a_out, r_out, s_out, gate, x, g.mod_array(ga), wb, wo)


def _ffn_kernel(x_ref, nw_ref, sc_ref, sh_ref, gf_ref, w1_ref, w3_ref, w2_ref, o_ref, h_scr, acc_scr, *, precise):
    j = pl.program_id(2)

    @pl.when(j == 0)
    def _():
        h_scr[...] = (_rms(x_ref[...]) * nw_ref[...] * (1.0 + sc_ref[0]) + sh_ref[0]).astype(h_scr.dtype)
        acc_scr[...] = jnp.zeros(acc_scr.shape, F32)

    h = h_scr[...]
    u = _silu(_mm(h, w1_ref[...], precise)) * _mm(h, w3_ref[...], precise)
    acc_scr[...] += _mm(u, w2_ref[...], precise)

    @pl.when(j == pl.num_programs(2) - 1)
    def _():
        o_ref[...] = x_ref[...] + gf_ref[0] * acc_scr[...]


def dense_ffn(g, x, nw, sc, sh, gf, w1, w3, w2, tf):
    dff = w1.shape[1]
    lift = lambda spec: pl.BlockSpec(spec.block_shape, lambda b, s, j, f=spec.index_map: f(b, s))
    return pl.pallas_call(
        functools.partial(_ffn_kernel, precise=g.precise),
        grid=g.grid + (dff // tf,),
        in_specs=[lift(g.rows(D_MODEL)), lift(g.full((1, D_MODEL))), lift(g.mod_spec(D_MODEL)),
                  lift(g.mod_spec(D_MODEL)), lift(g.mod_spec(D_MODEL)),
                  pl.BlockSpec((D_MODEL, tf), lambda b, s, j: (0, j)), pl.BlockSpec((D_MODEL, tf), lambda b, s, j: (0, j)),
                  pl.BlockSpec((tf, D_MODEL), lambda b, s, j: (j, 0))],
        out_specs=lift(g.rows(D_MODEL)),
        out_shape=jax.ShapeDtypeStruct((g.M, D_MODEL), F32),
        scratch_shapes=[pltpu.VMEM((g.tm, D_MODEL), g.act_dtype), pltpu.VMEM((g.tm, D_MODEL), F32)],
        compiler_params=_cparams("parallel", "parallel", "arbitrary"),
        name="dense_ffn",
    )(x, nw.reshape(1, -1), g.mod_array(sc), g.mod_array(sh), g.mod_array(gf), w1, w3, w2)


def _router_kernel(x_ref, nw_ref, sc_ref, sh_ref, wr_ref, rb_ref, h_ref, logit_ref):
    h = _rms(x_ref[...]) * nw_ref[...] * (1.0 + sc_ref[0]) + sh_ref[0]
    h_ref[...] = h.astype(BF16)
    logit_ref[...] = _dot3(h, wr_ref[...]) + rb_ref[...]


def moe_router(g, x, nw, sc, sh, wr, rb):
    return pl.pallas_call(
        _router_kernel,
        grid=g.grid,
        in_specs=[g.rows(D_MODEL), g.full((1, D_MODEL)), g.mod_spec(D_MODEL), g.mod_spec(D_MODEL),
                  g.full((D_MODEL, 128)), g.full((1, 128))],
        out_specs=[g.rows(D_MODEL), g.rows(128)],
        out_shape=[jax.ShapeDtypeStruct((g.M, D_MODEL), BF16), jax.ShapeDtypeStruct((g.M, 128), F32)],
        compiler_params=_cparams("parallel", "parallel"),
        name="moe_router",
    )(x, nw.reshape(1, -1), g.mod_array(sc), g.mod_array(sh), wr, rb)


def _expert_kernel(te_ref, tv_ref, h_ref, w1_ref, w3_ref, w2_ref, o_ref, acc_scr):
    i = pl.program_id(0)
    j = pl.program_id(1)

    @pl.when(j == 0)
    def _():
        acc_scr[...] = jnp.zeros(acc_scr.shape, F32)

    @pl.when(tv_ref[i] > 0)
    def _():
        h = h_ref[...]
        u = (_silu(_dot(h, w1_ref[...])) * _dot(h, w3_ref[...])).astype(BF16)
        acc_scr[...] += _dot(u, w2_ref[...])

    @pl.when(j == pl.num_programs(1) - 1)
    def _():
        o_ref[...] = acc_scr[...]


def expert_ffn(tile_expert, tile_valid, h_sorted, w1, w3, w2, tm, tf):
    rows = h_sorted.shape[0]
    dffe = w1.shape[2]
    grid_spec = pltpu.PrefetchScalarGridSpec(
        num_scalar_prefetch=2,
        grid=(rows // tm, dffe // tf),
        in_specs=[pl.BlockSpec((tm, D_MODEL), lambda i, j, te, tv: (i, 0)),
                  pl.BlockSpec((None, D_MODEL, tf), lambda i, j, te, tv: (te[i], 0, j)),
                  pl.BlockSpec((None, D_MODEL, tf), lambda i, j, te, tv: (te[i], 0, j)),
                  pl.BlockSpec((None, tf, D_MODEL), lambda i, j, te, tv: (te[i], j, 0))],
        out_specs=pl.BlockSpec((tm, D_MODEL), lambda i, j, te, tv: (i, 0)),
        scratch_shapes=[pltpu.VMEM((tm, D_MODEL), F32)],
    )
    return pl.pallas_call(
        _expert_kernel,
        grid_spec=grid_spec,
        out_shape=jax.ShapeDtypeStruct((rows, D_MODEL), F32),
        compiler_params=_cparams("parallel", "arbitrary"),
        name="expert_ffn",
    )(tile_expert, tile_valid, h_sorted, w1, w3, w2)


def _combine_kernel(x_ref, gf_ref, y0_ref, y1_ref, wt_ref, o_ref):
    wt = wt_ref[...]
    f = wt[:, 0:1] * y0_ref[...] + wt[:, 1:2] * y1_ref[...]
    o_ref[...] = x_ref[...] + gf_ref[0] * f


def moe_combine(g, x, gf, y0, y1, wt):
    return pl.pallas_call(
        _combine_kernel,
        grid=g.grid,
        in_specs=[g.rows(D_MODEL), g.mod_spec(D_MODEL), g.rows(D_MODEL), g.rows(D_MODEL), g.rows(128)],
        out_specs=g.rows(D_MODEL),
        out_shape=jax.ShapeDtypeStruct((g.M, D_MODEL), F32),
        compiler_params=_cparams("parallel", "parallel"),
        name="moe_combine",
    )(x, g.mod_array(gf), y0, y1, wt)


def moe_ffn(g, x, nw, sc, sh, gf, wr, rb, w1, w3, w2):
    M = g.M
    h, logits = moe_router(g, x, nw, sc, sh, wr, rb)
    top_v, top_i = lax.top_k(logits[:, :NE], TOPK)
    top_w = jax.nn.softmax(top_v, axis=-1)
    tm = min(512, max(128, M // 4))
    tf = 1792
    flat_e = top_i.reshape(-1)
    onehot = (flat_e[:, None] == jnp.arange(NE)[None, :]).astype(jnp.int32)
    rank = jnp.take_along_axis(jnp.cumsum(onehot, axis=0) - onehot, flat_e[:, None], axis=1)[:, 0]
    counts = jnp.sum(onehot, axis=0)
    padded = ((counts + tm - 1) // tm) * tm
    starts = jnp.cumsum(padded) - padded
    pos = starts[flat_e] + rank
    n_rows = M * TOPK + NE * tm
    row_token = jnp.zeros((n_rows,), jnp.int32).at[pos].set(jnp.arange(M * TOPK, dtype=jnp.int32) // TOPK)
    tile_start = jnp.arange(n_rows // tm, dtype=jnp.int32) * tm
    ends = starts + padded
    tile_expert = jnp.minimum(jnp.sum((tile_start[:, None] >= ends[None, :]).astype(jnp.int32), axis=1), NE - 1)
    tile_valid = (tile_start < ends[NE - 1]).astype(jnp.int32)
    h_sorted = jnp.take(h, row_token, axis=0)
    y_sorted = expert_ffn(tile_expert.astype(jnp.int32), tile_valid, h_sorted, w1, w3, w2, tm, tf)
    pos2 = pos.reshape(M, TOPK)
    y0 = jnp.take(y_sorted, pos2[:, 0], axis=0)
    y1 = jnp.take(y_sorted, pos2[:, 1], axis=0)
    wt = jnp.pad(top_w, ((0, 0), (0, 128 - TOPK)))
    return moe_combine(g, x, gf, y0, y1, wt)


def _final_norm_kernel(x_ref, w_ref, o_ref):
    o_ref[...] = _rms(x_ref[...]) * w_ref[...]


def final_norm(g, x, w):
    return pl.pallas_call(
        _final_norm_kernel,
        grid=g.grid,
        in_specs=[g.rows(D_MODEL), g.full((1, D_MODEL))],
        out_specs=g.rows(D_MODEL),
        out_shape=jax.ShapeDtypeStruct((g.M, D_MODEL), F32),
        compiler_params=_cparams("parallel", "parallel"),
        name="final_norm",
    )(x, w.reshape(1, -1))


def _rot_half_cols(w):
    half = ROPE // 2
    return jnp.concatenate([-w[..., half:], w[..., :half]], axis=-1)


def _pack_layer(l, W):
    p = {}
    w_in = W["w_in"][l]
    o = 0
    q_c, kv_c, kr = w_in[:, 0:QL], w_in[:, QL:QL + KVL], w_in[:, QL + KVL:QL + KVL + ROPE]
    o = QL + KVL + ROPE
    rw = w_in[:, o:o + RWKV_IN]
    o += RWKV_IN
    z = w_in[:, o:o + SDIM]
    o += SDIM
    xbc = w_in[:, o:o + SCD]
    o += SCD
    dt = w_in[:, o:o + SHEADS]
    o += SHEADS
    gate = w_in[:, o:]
    seg_a = jnp.concatenate([q_c, kv_c, kr, dt, _rot_half_cols(kr),
                             jnp.zeros((D_MODEL, SEG_A - (QL + KVL + 2 * ROPE + SHEADS)), F32)], axis=1)
    mw = {}
    mw["w_in"] = jnp.concatenate([seg_a, rw, z, xbc, gate], axis=1)
    wq = W["mla_w_uq"][l].reshape(QL, H, NOPE + ROPE)
    pe = wq[:, :, NOPE:]
    mw["wuq"] = jnp.concatenate([wq[:, :, :NOPE].reshape(QL, H * NOPE), pe.reshape(QL, H * ROPE),
                                 _rot_half_cols(pe).reshape(QL, H * ROPE)], axis=1)
    mw["wuk"] = jnp.transpose(W["mla_w_uk"][l], (1, 2, 0))
    mw["wuv"] = jnp.transpose(W["mla_w_uv"][l], (1, 0, 2))
    mw["w2"], mw["a2"], mw["g2"] = W["rwkv_w2"][l], W["rwkv_a2"][l], W["rwkv_g2"][l]
    mw["w_branch"], mw["w_out"] = W["w_branch"][l], W["w_out"][l]
    if l % 2 == 0:
        mw["ffn"] = tuple(W[n][l // 2] for n in ("ffn_w1", "ffn_w3", "ffn_w2"))
    p["mw"] = mw
    p["mw16"] = jax.tree_util.tree_map(lambda a: a.astype(BF16), mw)
    p["q_norm"], p["kv_norm"] = W["mla_q_norm"][l], W["mla_kv_norm"][l]
    p["rwkv"] = dict(mu=W["rwkv_mu"][l], w0=W["rwkv_w0"][l], a0=W["rwkv_a0"][l], k_k=W["rwkv_k_k"][l],
                     k_a=W["rwkv_k_a"][l], ln_w=W["rwkv_ln_w"][l], ln_b=W["rwkv_ln_b"][l],
                     r_k=W["rwkv_r_k"][l].reshape(-1))
    lanes = jnp.arange(128)
    head_lane = (lanes >= DT_LANE) & (lanes < DT_LANE + SHEADS)
    idx = jnp.clip(lanes - DT_LANE, 0, SHEADS - 1)
    p["ssm"] = dict(conv_w=W["ssm_conv_w"][l], conv_b=W["ssm_conv_b"][l],
                    dtb128=jnp.where(head_lane, W["ssm_dt_bias"][l][idx], 0.0).reshape(1, 128),
                    a128=jnp.where(head_lane, -jnp.exp(W["ssm_a_log"][l][idx]), 0.0).reshape(1, 128),
                    dskip512=jnp.repeat(W["ssm_d"][l], SP).reshape(1, SDIM), ssm_norm=W["ssm_norm"][l])
    p["norm_attn"], p["norm_ffn"] = W["norm_attn"][l], W["norm_ffn"][l]
    p["w_ada"], p["b_ada"] = W["w_ada"][l], W["b_ada"][l]
    if l % 2 == 1:
        wr = jnp.pad(W["moe_router"][l // 2], ((0, 0), (0, 128 - NE)))
        p["router"] = (wr, jnp.pad(W["moe_router_b"][l // 2], (0, 128 - NE)).reshape(1, 128))
        p["moe"] = tuple(W[n][l // 2].astype(BF16) for n in ("moe_w1", "moe_w3", "moe_w2"))
    return p


def _rope_tables(pos):
    half = ROPE // 2
    freq = ROPE_THETA ** (-jnp.arange(half, dtype=F32) / half)
    ang = pos.astype(F32)[:, None] * freq[None, :]
    cos = jnp.concatenate([jnp.cos(ang)] * 2, axis=-1)
    sin = jnp.concatenate([jnp.sin(ang)] * 2, axis=-1)
    return cos, sin, jnp.tile(cos, (1, H)), jnp.tile(sin, (1, H))


def _trunk(x3, c, pos, paged, shift0, wkv0, conv0, ssm0, layers, norm_final):
    B, S, _ = x3.shape
    M = B * S
    g = _Group(B, S, 512)
    g_win = _Group(B, S, 256)
    x = x3.reshape(M, D_MODEL)
    tabs = _rope_tables(pos if S > 1 else jnp.broadcast_to(pos, (M,)))
    ri = jnp.arange(RDIM)
    ones_bd = ((ri[:, None] // RHD) == (ri[None, :] // RHD)).astype(BF16)
    outs = [[] for _ in range(6)]
    for l, p in enumerate(layers):
        mw = p["mw"] if g.precise else p["mw16"]
        ada = ada_matmul(c, p["w_ada"], p["b_ada"])
        sh_a, sc_a, g_a, sh_f, sc_f, g_f = jnp.split(ada, 6, axis=-1)
        seg_a, rw, z, xbc, gate = win_project(g_win, x, p["norm_attn"], sc_a, sh_a, mw["w_in"])

        qc, kc, ckv, kpe = mla_prep(g, seg_a, tabs, p["q_norm"], p["kv_norm"], mw["wuq"], mw["wuk"])
        if paged is None:
            a_out = mla_attention(B, S, qc, kc, mw["wuv"])
        else:
            cache_ckv, cache_kpe, page_table = paged
            a_out = paged_attention(l, page_table, qc, kc, cache_ckv, cache_kpe, mw["wuv"])

        rw3 = rw.reshape(B, S, RWKV_IN)
        tc_len = min(RHD, S)
        nch = S // tc_len
        tpb, hpt = SCAN_TPB, SCAN_HPT
        as_rows = lambda a: a.reshape(S, B * tpb, SCAN_W)
        if S > 1:
            r_t, w_t, k_t, v_t, kk_t, b_t, gate_r, vt = rwkv_prep(g, rw, shift0[l], p["rwkv"], mw, ones_bd)
        else:
            r_t, w_t, k_t, v_t, kk_t, b_t, gate_r = rwkv_prep(g, rw, shift0[l], p["rwkv"], mw, ones_bd)
            vt = v_t.reshape(nch, tc_len, B, tpb, hpt, RHD).transpose(0, 2, 3, 5, 4, 1)
            vt = jnp.pad(vt, ((0, 0),) * 5 + ((0, RHD - tc_len),)).reshape(nch, B * tpb, RHD, SCAN_W)
        s0 = wkv0[l].reshape(B, tpb, hpt, RHD, RHD).transpose(0, 1, 3, 2, 4).reshape(B * tpb, RHD, SCAN_W)
        yt, s1 = rwkv_scan(B, S, as_rows(r_t), as_rows(w_t), as_rows(k_t), as_rows(kk_t), as_rows(b_t), vt, s0,
                           g.precise)
        y = yt.reshape(nch, B, tpb, RHD, hpt, RHD)[..., :tc_len].transpose(1, 0, 5, 2, 4, 3).reshape(M, RDIM)
        wkv1 = s1.reshape(B, tpb, RHD, hpt, RHD).transpose(0, 1, 3, 2, 4).reshape(B, tpb * hpt, RHD, RHD)
        r_out = rwkv_post(g, y, r_t, k_t, v_t, gate_r, p["rwkv"], ones_bd)
        shift1 = rw3[:, -1]

        ps = p["ssm"]
        if S > 1:
            tail0 = jnp.pad(conv0[l], ((0, 0), (8 - (SCONV - 1), 0), (0, 0)))
            s_out, h_t = ssd_prompt(B, S, xbc, z, seg_a, tail0, ps)
            ssm1 = jnp.swapaxes(h_t, 2, 3)
            conv1 = xbc.reshape(B, S, SCD)[:, S - (SCONV - 1):]
        else:
            c0, c1, c2 = conv0[l][:, 0], conv0[l][:, 1], conv0[l][:, 2]
            act, xdt, dec128 = ssm_step_pre(g, xbc, c0, c1, c2, seg_a, ps)
            xb = jnp.broadcast_to(xdt.reshape(B, SHEADS, SP, 1), (B, SHEADS, SP, SN))
            dec = jnp.broadcast_to(dec128[:, DT_LANE:DT_LANE + SHEADS].reshape(B, SHEADS, 1, 1), (B, SHEADS, 1, SN))
            ssm1, y_s = ssm_step(B, l, ssm0, xb, dec, act[:, SDIM:SDIM + SGROUPS * SN], act[:, SDIM + SGROUPS * SN:])
            s_out = ssm_step_post(g, y_s, act[:, :SDIM], z, ps)
            conv1 = jnp.concatenate([conv0[l][:, 1:], xbc[:, None, :]], axis=1)

        x = merge_branches(g, a_out, r_out, s_out, gate, x, g_a, mw["w_branch"], mw["w_out"])

        if l % 2 == 0:
            w1, w3, w2 = mw["ffn"]
            x = dense_ffn(g, x, p["norm_ffn"], sc_f, sh_f, g_f, w1, w3, w2, w1.shape[1] // 2)
        else:
            w1, w3, w2 = p["moe"]
            x = moe_ffn(g, x, p["norm_ffn"], sc_f, sh_f, g_f, *p["router"], w1, w3, w2)
        for lst, v in zip(outs, (ckv.reshape(B, S, KVL), kpe.reshape(B, S, ROPE), shift1, wkv1, conv1, ssm1)):
            lst.append(v)
    y = final_norm(g, x, norm_final).reshape(B, S, D_MODEL)
    return y, [jnp.stack(v) for v in outs]


def kernel(x_prompt, x_sample, cache_ckv, cache_kpe, state_rwkv_shift, state_rwkv_wkv, state_ssm_conv, state_ssm, page_table, c_prompt, c_sample, w_ada, b_ada, norm_attn, norm_ffn, norm_final, w_in, mla_q_norm, mla_w_uq, mla_kv_norm, mla_w_uk, mla_w_uv, rwkv_mu, rwkv_w0, rwkv_w2, rwkv_a0, rwkv_a2, rwkv_g2, rwkv_k_k, rwkv_k_a, rwkv_r_k, rwkv_ln_w, rwkv_ln_b, ssm_conv_w, ssm_conv_b, ssm_dt_bias, ssm_a_log, ssm_d, ssm_norm, w_branch, w_out, ffn_w1, ffn_w3, ffn_w2, moe_router, moe_router_b, moe_w1, moe_w3, moe_w2):
    W = dict(w_ada=w_ada, b_ada=b_ada, norm_attn=norm_attn, norm_ffn=norm_ffn, w_in=w_in, mla_q_norm=mla_q_norm,
             mla_w_uq=mla_w_uq, mla_kv_norm=mla_kv_norm, mla_w_uk=mla_w_uk, mla_w_uv=mla_w_uv, rwkv_mu=rwkv_mu,
             rwkv_w0=rwkv_w0, rwkv_w2=rwkv_w2, rwkv_a0=rwkv_a0, rwkv_a2=rwkv_a2, rwkv_g2=rwkv_g2, rwkv_k_k=rwkv_k_k,
             rwkv_k_a=rwkv_k_a, rwkv_r_k=rwkv_r_k, rwkv_ln_w=rwkv_ln_w, rwkv_ln_b=rwkv_ln_b, ssm_conv_w=ssm_conv_w,
             ssm_conv_b=ssm_conv_b, ssm_dt_bias=ssm_dt_bias, ssm_a_log=ssm_a_log, ssm_d=ssm_d, ssm_norm=ssm_norm,
             w_branch=w_branch, w_out=w_out, ffn_w1=ffn_w1, ffn_w3=ffn_w3, ffn_w2=ffn_w2, moe_router=moe_router,
             moe_router_b=moe_router_b, moe_w1=moe_w1, moe_w3=moe_w3, moe_w2=moe_w2)
    depth = w_in.shape[0]
    layers = [_pack_layer(l, W) for l in range(depth)]
    bp, sp, _ = x_prompt.shape
    bs, ss, _ = x_sample.shape
    dt = x_prompt.dtype
    y_prompt, (p_ckv, p_kpe, p_shift, p_wkv, p_conv, p_ssm) = _trunk(
        x_prompt, c_prompt, jnp.arange(sp, dtype=jnp.int32), None,
        jnp.zeros((depth, bp, RWKV_IN), dt), jnp.zeros((depth, bp, 8, RHD, RHD), dt),
        jnp.zeros((depth, bp, SCONV - 1, SCD), dt), jnp.zeros((depth, bp, SHEADS, SP, SN), dt), layers, norm_final)
    p_ckv = p_ckv.reshape(depth, bp * sp // PAGE, PAGE, KVL)
    p_kpe = p_kpe.reshape(depth, bp * sp // PAGE, PAGE, ROPE)
    past_len = page_table.shape[1] * PAGE
    pos_s = past_len + jnp.arange(ss, dtype=jnp.int32)
    y_sample, (s_ckv, s_kpe, s_shift, s_wkv, s_conv, s_ssm) = _trunk(
        x_sample, c_sample, pos_s, (cache_ckv, cache_kpe, page_table),
        state_rwkv_shift, state_rwkv_wkv, state_ssm_conv, state_ssm, layers, norm_final)
    return (y_prompt, y_sample, p_ckv, p_kpe, p_shift, p_wkv, p_conv, p_ssm,
            s_ckv, s_kpe, s_shift, s_wkv, s_conv, s_ssm)
```

```python
import functools

import jax
import jax.numpy as jnp
from jax import lax
from jax.experimental import pallas as pl
from jax.experimental.pallas import tpu as pltpu

F32 = jnp.float32
BF16 = jnp.bfloat16

D_MODEL = 1024
DEPTH = 2
PAGE = 128
H = 8
NOPE = 64
ROPE = 32
VD = 64
QL = 256
KVL = 256
ROPE_THETA = 10000.0
MLA_SCALE = (NOPE + ROPE) ** -0.5
LOG2E = 1.4426950408889634
QK = 384
RDIM = 512
RHD = 64
DECAY_LORA = 64
AAA_LORA = 64
GATE_LORA = 128
RWKV_IN = 3 * RDIM + DECAY_LORA + AAA_LORA + GATE_LORA
RWKV_LN_EPS = 64e-5
SHEADS = 8
SP = 64
SDIM = 512
SGROUPS = 2
SN = 128
SCONV = 4
SCHUNK = 128
SCD = SDIM + 2 * SGROUPS * SN
NBRANCH = 3
NE = 8
TOPK = 2
EPS = 1e-6
SEG_A = 768
DT_TILE = 4
DT_LANE = 32
SEG_WIDTHS = (SEG_A, RWKV_IN, SDIM, SCD, NBRANCH * D_MODEL)
W_IN_COLS = sum(SEG_WIDTHS)
VMEM_LIMIT = 56 * 1024 * 1024


def _cparams(*sem):
    return pltpu.CompilerParams(dimension_semantics=sem, vmem_limit_bytes=VMEM_LIMIT)


def _dot(a, b):
    return jnp.dot(a, b, preferred_element_type=F32)


def _dot_nt(a, b):
    return lax.dot_general(a, b, (((1,), (1,)), ((), ())), preferred_element_type=F32)


def _hi_lo(x):
    hi = x.astype(BF16)
    return hi, (x.astype(F32) - hi.astype(F32)).astype(BF16)


def _dot3(a, w, dot=_dot):
    a_hi, a_lo = _hi_lo(a)
    w_hi, w_lo = _hi_lo(w)
    return dot(a_hi, w_hi) + dot(a_lo, w_hi) + dot(a_hi, w_lo)


def _mm(a, w, precise):
    return _dot3(a, w) if precise else _dot(a.astype(BF16), w)


def _split_dot(x, w01):
    hi = x.astype(BF16)
    lo = (x - hi.astype(F32)).astype(BF16)
    return _dot(hi, w01) + _dot(lo, w01)


def _split3_dot_left(w01, x):
    hi = x.astype(BF16)
    r1 = x - hi.astype(F32)
    mid = r1.astype(BF16)
    lo = (r1 - mid.astype(F32)).astype(BF16)
    return _dot(w01, hi) + _dot(w01, mid) + _dot(w01, lo)


def _sigmoid(x):
    return 1.0 / (1.0 + jnp.exp(-x))


def _silu(x):
    return x * _sigmoid(x)


def _softplus(x):
    return jnp.maximum(x, 0.0) + jnp.log(1.0 + jnp.exp(-jnp.abs(x)))


def _rms(x):
    return x * lax.rsqrt(jnp.mean(x * x, axis=-1, keepdims=True) + EPS)


class _Group:
    def __init__(self, B, S, tm):
        self.B, self.S, self.M = B, S, B * S
        self.precise = S == 1
        self.act_dtype = F32 if self.precise else BF16
        if S == 1:
            self.tm = min(tm, self.M)
            self.grid = (1, self.M // self.tm)
        else:
            self.tm = min(tm, S)
            self.grid = (B, S // self.tm)
        self.ns = self.grid[1]

    def rows(self, width, colblock=0):
        ns = self.ns
        return pl.BlockSpec((self.tm, width), lambda b, s: (b * ns + s, colblock))

    def full(self, shape):
        nd = len(shape)
        return pl.BlockSpec(shape, lambda b, s: (0,) * nd)

    def mod_array(self, m):
        return m.reshape(1, self.M, -1) if self.S == 1 else m.reshape(self.B, 1, -1)

    def mod_spec(self, width):
        if self.S == 1:
            return pl.BlockSpec((1, self.tm, width), lambda b, s: (0, s, 0))
        return pl.BlockSpec((1, 1, width), lambda b, s: (b, 0, 0))

    def pos_spec(self, width):
        if self.S == 1:
            return self.rows(width)
        return pl.BlockSpec((self.tm, width), lambda b, s: (s, 0))

    def tmaj_shape(self, width):
        return (self.M, width) if self.S == 1 else (self.S, self.B * width)

    def tmaj_spec(self, width):
        if self.S == 1:
            return self.rows(width)
        return pl.BlockSpec((self.tm, width), lambda b, s: (s, b))


def _ada_kernel(c_ref, w_ref, b_ref, o_ref):
    c = c_ref[...]
    o_ref[...] = _dot3(_silu(c), w_ref[...]) + b_ref[...]


def ada_matmul(c, w, b):
    m, k = c.shape
    n = w.shape[1]
    tn = 1024
    return pl.pallas_call(
        _ada_kernel,
        grid=(n // tn,),
        in_specs=[pl.BlockSpec((m, k), lambda j: (0, 0)), pl.BlockSpec((k, tn), lambda j: (0, j)),
                  pl.BlockSpec((1, tn), lambda j: (0, j))],
        out_specs=pl.BlockSpec((m, tn), lambda j: (0, j)),
        out_shape=jax.ShapeDtypeStruct((m, n), F32),
        compiler_params=_cparams("arbitrary"),
        name="ada_matmul",
    )(c, w, b.reshape(1, n))


def _win_kernel(x_ref, nw_ref, sc_ref, sh_ref, w_ref, *o_refs):
    h = (_rms(x_ref[...]) * nw_ref[...] * (1.0 + sc_ref[0]) + sh_ref[0]).astype(BF16)
    off = 0
    for o in o_refs:
        n = o.shape[-1]
        for c in range(0, n, 256):
            o[:, c:c + 256] = _dot(h, w_ref[:, off + c:off + c + 256])
        off += n


def _win_cols_kernel(x_ref, nw_ref, sc_ref, sh_ref, w_ref, o_ref):
    h = _rms(x_ref[...]) * nw_ref[...] * (1.0 + sc_ref[0]) + sh_ref[0]
    o_ref[...] = _dot3(h, w_ref[...])


def win_project_precise(g, x, nw, sc, sh, w_packed):
    tn = 512
    out = pl.pallas_call(
        _win_cols_kernel,
        grid=(W_IN_COLS // tn,),
        in_specs=[pl.BlockSpec((g.M, D_MODEL), lambda j: (0, 0)), pl.BlockSpec((1, D_MODEL), lambda j: (0, 0)),
                  pl.BlockSpec((1, g.M, D_MODEL), lambda j: (0, 0, 0)),
                  pl.BlockSpec((1, g.M, D_MODEL), lambda j: (0, 0, 0)),
                  pl.BlockSpec((D_MODEL, tn), lambda j: (0, j))],
        out_specs=pl.BlockSpec((g.M, tn), lambda j: (0, j)),
        out_shape=jax.ShapeDtypeStruct((g.M, W_IN_COLS), F32),
        compiler_params=_cparams("parallel"),
        name="win_project_precise",
    )(x, nw.reshape(1, -1), g.mod_array(sc), g.mod_array(sh), w_packed)
    offs = [0]
    for w in SEG_WIDTHS:
        offs.append(offs[-1] + w)
    return [out[:, offs[i]:offs[i + 1]] for i in range(len(SEG_WIDTHS))]


def win_project(g, x, nw, sc, sh, w_packed):
    if g.precise:
        return win_project_precise(g, x, nw, sc, sh, w_packed)
    return pl.pallas_call(
        _win_kernel,
        grid=g.grid,
        in_specs=[g.rows(D_MODEL), g.full((1, D_MODEL)), g.mod_spec(D_MODEL), g.mod_spec(D_MODEL),
                  g.full((D_MODEL, W_IN_COLS))],
        out_specs=[g.rows(w) for w in SEG_WIDTHS],
        out_shape=[jax.ShapeDtypeStruct((g.M, w), F32) for w in SEG_WIDTHS],
        compiler_params=_cparams("parallel", "parallel"),
        name="win_project",
    )(x, nw.reshape(1, -1), g.mod_array(sc), g.mod_array(sh), w_packed)


def _mla_prep_kernel(a_ref, cos_ref, sin_ref, cos8_ref, sin8_ref, qn_ref, kvn_ref, wuq_ref, wuk_ref,
                     qc_ref, kc_ref, ckv_ref, kpe_ref, *, precise):
    a = a_ref[...]
    tm = a.shape[0]
    odt = qc_ref.dtype
    qn = _rms(a[:, 0:QL]) * qn_ref[...]
    qa = _mm(qn, wuq_ref[...], precise)
    q_rope = qa[:, 512:768] * cos8_ref[...] + qa[:, 768:1024] * sin8_ref[...]
    zpad = jnp.zeros((tm, QK - KVL - ROPE), odt)
    qscale = MLA_SCALE if precise else MLA_SCALE * LOG2E
    for h in range(H):
        q_abs = _mm(qa[:, h * NOPE:(h + 1) * NOPE], wuk_ref[h], precise) * qscale
        qc_ref[h, :, 0:KVL] = q_abs.astype(odt)
        qc_ref[h, :, KVL:KVL + ROPE] = (q_rope[:, h * ROPE:(h + 1) * ROPE] * qscale).astype(odt)
        qc_ref[h, :, KVL + ROPE:QK] = zpad
    ckv = _rms(a[:, QL:QL + KVL]) * kvn_ref[...]
    kpe = a[:, 512:544] * cos_ref[...] + a[:, 552:584] * sin_ref[...]
    ckv_ref[...] = ckv
    kpe_ref[...] = kpe
    kc_ref[:, 0:KVL] = ckv.astype(odt)
    kc_ref[:, KVL:KVL + ROPE] = kpe.astype(odt)
    kc_ref[:, KVL + ROPE:QK] = zpad


def mla_prep(g, seg_a, tabs, qn, kvn, wuq, wuk):
    cos, sin, cos8, sin8 = tabs
    ns = g.ns
    return pl.pallas_call(
        functools.partial(_mla_prep_kernel, precise=g.precise),
        grid=g.grid,
        in_specs=[g.rows(SEG_A), g.pos_spec(ROPE), g.pos_spec(ROPE), g.pos_spec(H * ROPE), g.pos_spec(H * ROPE),
                  g.full((1, QL)), g.full((1, KVL)), g.full((QL, 1024)), g.full((H, NOPE, KVL))],
        out_specs=[pl.BlockSpec((H, g.tm, QK), lambda b, s: (0, b * ns + s, 0)), g.rows(QK), g.rows(KVL),
                   g.rows(ROPE)],
        out_shape=[jax.ShapeDtypeStruct((H, g.M, QK), g.act_dtype), jax.ShapeDtypeStruct((g.M, QK), g.act_dtype),
                   jax.ShapeDtypeStruct((g.M, KVL), F32), jax.ShapeDtypeStruct((g.M, ROPE), F32)],
        compiler_params=_cparams("parallel", "parallel"),
        name="mla_prep",
    )(seg_a, cos, sin, cos8, sin8, qn.reshape(1, -1), kvn.reshape(1, -1), wuq, wuk)


NEG = -1e30


def _attn_kernel(q_ref, k_ref, wuv_ref, o_ref, m_scr, l_scr, a_scr, acc_scr, s_scr, p_scr, *, tq, tk):
    qi = pl.program_id(1)
    q = q_ref[...].reshape(H * tq, QK)
    m_scr[...] = jnp.full(m_scr.shape, NEG, F32)
    l_scr[...] = jnp.zeros(l_scr.shape, F32)
    acc_scr[...] = jnp.zeros(acc_scr.shape, F32)
    reps = tk // 128
    wide = lambda a: jnp.concatenate([a] * reps, axis=-1)

    def keys(j):
        return k_ref[pl.ds(pl.multiple_of(j * tk, tk), tk), :]

    def scores(j, slot):
        s_scr[slot] = _dot_nt(q, keys(j))

    def softmax_pv(j, slot, masked):
        if masked:
            visible = (j * tk + lax.broadcasted_iota(jnp.int32, (tq, tk), 1)
                       <= qi * tq + lax.broadcasted_iota(jnp.int32, (tq, tk), 0))
        for h in range(H):
            rs = pl.ds(h * tq, tq)
            s = s_scr[slot, rs, :]
            if masked:
                s = jnp.where(visible, s, NEG)
            m_prev = m_scr[rs, :]
            m_new = jnp.maximum(m_prev, jnp.max(s, axis=-1, keepdims=True))
            alpha = jnp.exp2(m_prev - m_new)
            p = jnp.exp2(s - wide(m_new))
            l_scr[rs, :] = alpha * l_scr[rs, :] + jnp.sum(p, axis=-1, keepdims=True)
            m_scr[rs, :] = m_new
            a_scr[rs, :] = alpha
            p_scr[slot, rs, :] = p.astype(BF16)
        alpha = a_scr[...]
        acc_scr[...] = (acc_scr[...] * jnp.concatenate([alpha] * (KVL // 128), axis=-1)
                        + _dot(p_scr[slot], keys(j)[:, 0:KVL]))

    n_full = (qi * tq) // tk

    def pair(j, last_masked):
        scores(j, 0)
        scores(j + 1, 1)
        softmax_pv(j, 0, False)
        softmax_pv(j + 1, 1, last_masked)

    def body(i, carry):
        pair(2 * i, False)
        return carry

    lax.fori_loop(0, n_full // 2, body, 0)

    @pl.when(n_full % 2 == 1)
    def _():
        pair(n_full - 1, True)

    @pl.when(n_full % 2 == 0)
    def _():
        scores(n_full, 0)
        softmax_pv(n_full, 0, True)
    inv_l = 1.0 / l_scr[...]
    o = acc_scr[...] * jnp.concatenate([inv_l] * (KVL // 128), axis=-1)
    for h in range(H):
        o_ref[:, h * VD:(h + 1) * VD] = _dot(o[h * tq:(h + 1) * tq].astype(BF16), wuv_ref[h]).astype(o_ref.dtype)


def mla_attention(B, S, qc, kc, wuv):
    tq = min(128, S)
    tk = min(256, S)
    nq = S // tq
    rows = H * tq
    return pl.pallas_call(
        functools.partial(_attn_kernel, tq=tq, tk=tk),
        grid=(B, nq),
        in_specs=[pl.BlockSpec((H, tq, QK), lambda b, i: (0, b * nq + i, 0)),
                  pl.BlockSpec((S, QK), lambda b, i: (b, 0)),
                  pl.BlockSpec((H, KVL, VD), lambda b, i: (0, 0, 0))],
        out_specs=pl.BlockSpec((tq, H * VD), lambda b, i: (b * nq + i, 0)),
        out_shape=jax.ShapeDtypeStruct((B * S, H * VD), BF16),
        scratch_shapes=[pltpu.VMEM((rows, 128), F32), pltpu.VMEM((rows, 128), F32), pltpu.VMEM((rows, 128), F32),
                        pltpu.VMEM((rows, KVL), F32), pltpu.VMEM((2, rows, tk), F32),
                        pltpu.VMEM((2, rows, tk), BF16)],
        compiler_params=_cparams("parallel", "parallel"),
        name="mla_attention",
    )(qc, kc, wuv)


PAGES_PER_STEP = 16


def _paged_kernel(pt_ref, q_ref, knew_ref, wuv_ref, *rest):
    pp = PAGES_PER_STEP
    ckv_refs, kpe_refs = rest[:pp], rest[pp:2 * pp]
    o_ref, m_scr, l_scr, acc_scr = rest[2 * pp:]
    j = pl.program_id(1)
    q = q_ref[...]

    @pl.when(j == 0)
    def _():
        m_scr[...] = jnp.full(m_scr.shape, NEG, F32)
        l_scr[...] = jnp.zeros(l_scr.shape, F32)
        acc_scr[...] = jnp.zeros(acc_scr.shape, F32)

    def stack_hi_lo(x):
        hi = x.astype(BF16).astype(F32)
        return jnp.concatenate([hi, x - hi], axis=0).astype(BF16)

    qa2 = stack_hi_lo(q[:, 0:KVL])
    qp2 = stack_hi_lo(q[:, KVL:KVL + ROPE])

    def chain(refs_c, refs_p):
        ckv_hi, ckv_lo = _hi_lo(jnp.concatenate([r[...] for r in refs_c], axis=0))
        kpe_hi, kpe_lo = _hi_lo(jnp.concatenate([r[...] for r in refs_p], axis=1))
        s2 = _dot_nt(qa2, ckv_hi) + _dot(qp2, kpe_hi)
        s = s2[0:H] + s2[H:2 * H] + _dot_nt(qa2[0:H], ckv_lo) + _dot(qp2[0:H], kpe_lo)
        m = jnp.max(s, axis=-1, keepdims=True)
        p = jnp.exp(s - m)
        p2 = stack_hi_lo(p)
        pv2 = _dot(p2, ckv_hi)
        return m, jnp.sum(p, axis=-1, keepdims=True), pv2[0:H] + pv2[H:2 * H] + _dot(p2[0:H], ckv_lo)

    half = pp // 2
    parts = [chain(ckv_refs[i * half:(i + 1) * half], kpe_refs[i * half:(i + 1) * half]) for i in range(2)]
    m_prev = m_scr[...]
    m_new = jnp.maximum(m_prev, jnp.maximum(parts[0][0], parts[1][0]))
    alpha = jnp.exp(m_prev - m_new)
    l = alpha * l_scr[...]
    acc = alpha * acc_scr[...]
    for m_i, l_i, pv_i in parts:
        w_i = jnp.exp(m_i - m_new)
        l = l + w_i * l_i
        acc = acc + w_i * pv_i
    l_scr[...] = l
    acc_scr[...] = acc
    m_scr[...] = m_new

    @pl.when(j == pl.num_programs(1) - 1)
    def _():
        kn = knew_ref[0]
        s_new = jnp.sum(q * kn, axis=-1, keepdims=True)
        m_prev = m_scr[...]
        m_new = jnp.maximum(m_prev, s_new)
        alpha = jnp.exp(m_prev - m_new)
        p_new = jnp.exp(s_new - m_new)
        l = alpha * l_scr[...] + p_new
        acc = alpha * acc_scr[...] + p_new * kn[:, 0:KVL]
        o = acc / l
        for h in range(H):
            o_ref[0, :, h * VD:(h + 1) * VD] = _dot3(o[h:h + 1], wuv_ref[h])


def paged_attention(layer, page_table, qc, kc_new, cache_ckv, cache_kpe_t, wuv):
    B, n_pages = page_table.shape
    pp = PAGES_PER_STEP
    nsteps = n_pages // pp

    def page_spec(i, shape):
        return pl.BlockSpec((None, None) + shape, lambda b, j, pt: (layer, pt[b, j * pp + i], 0, 0))

    grid_spec = pltpu.PrefetchScalarGridSpec(
        num_scalar_prefetch=1,
        grid=(B, nsteps),
        in_specs=[pl.BlockSpec((None, H, QK), lambda b, j, pt: (b, 0, 0)),
                  pl.BlockSpec((1, 1, QK), lambda b, j, pt: (b, 0, 0)),
                  pl.BlockSpec((H, KVL, VD), lambda b, j, pt: (0, 0, 0))]
        + [page_spec(i, (PAGE, KVL)) for i in range(pp)] + [page_spec(i, (ROPE, PAGE)) for i in range(pp)],
        out_specs=pl.BlockSpec((1, 1, H * VD), lambda b, j, pt: (b, 0, 0)),
        scratch_shapes=[pltpu.VMEM((H, 1), F32), pltpu.VMEM((H, 1), F32), pltpu.VMEM((H, KVL), F32)],
    )
    out = pl.pallas_call(
        _paged_kernel,
        grid_spec=grid_spec,
        out_shape=jax.ShapeDtypeStruct((B, 1, H * VD), F32),
        compiler_params=_cparams("parallel", "arbitrary"),
        name="paged_attention",
    )(page_table, jnp.transpose(qc, (1, 0, 2)), kc_new.reshape(B, 1, QK), wuv, *([cache_ckv] * pp),
      *([cache_kpe_t] * pp))
    return out.reshape(B, H * VD)


def _rwkv_prep_kernel(rw_ref, prev_ref, mu_ref, w0_ref, a0_ref, kk_ref_, ka_ref, w2_ref, a2_ref, g2_ref, ones_ref,
                      r_o, w_o, k_o, v_o, kk_o, b_o, g_o, *rest, precise, seq):
    rw = rw_ref[...]
    if seq:
        vt_o, carry = rest
        tm = rw.shape[0]

        @pl.when(pl.program_id(1) == 0)
        def _():
            carry[...] = prev_ref[0]

        row8 = lax.broadcasted_iota(jnp.int32, (8, RWKV_IN), 0)
        shifted = pltpu.roll(rw, 1, 0)
        top = jnp.where(row8 == 0, pltpu.roll(carry[...], 1, 0), shifted[0:8])
        prev = jnp.concatenate([top, shifted[8:]], axis=0)
        carry[...] = rw[tm - 8:tm]
    else:
        prev = prev_ref[...]
    xs = rw + (prev - rw) * mu_ref[...]
    r = xs[:, 0:RDIM]
    k = xs[:, RDIM:2 * RDIM]
    v = xs[:, 2 * RDIM:3 * RDIM]
    o = 3 * RDIM
    wl = xs[:, o:o + DECAY_LORA]
    al = xs[:, o + DECAY_LORA:o + DECAY_LORA + AAA_LORA]
    gl = xs[:, o + DECAY_LORA + AAA_LORA:RWKV_IN]
    w = -_softplus(-(w0_ref[...] + _mm(jnp.tanh(wl), w2_ref[...], precise))) - 0.5
    a = _sigmoid(a0_ref[...] + _mm(al, a2_ref[...], precise))
    kk = k * kk_ref_[...]
    kk = kk * lax.rsqrt(jnp.maximum(_split_dot(kk * kk, ones_ref[...]), 1e-24))
    r_o[...] = r
    w_o[...] = jnp.exp(-jnp.exp(w))
    k_o[...] = k * (1.0 + (a - 1.0) * ka_ref[...])
    v_o[...] = v
    kk_o[...] = kk
    b_o[...] = kk * a
    g_o[...] = _mm(_sigmoid(gl), g2_ref[...], precise)
    if seq:
        ri = lax.broadcasted_iota(jnp.int32, (SCAN_W, SCAN_W), 0)
        ci = lax.broadcasted_iota(jnp.int32, (SCAN_W, SCAN_W), 1)
        same_head = (ri // RHD) == (ci // RHD)
        eye_rep = (lax.broadcasted_iota(jnp.int32, (RHD, SCAN_W), 0)
                   == lax.broadcasted_iota(jnp.int32, (RHD, SCAN_W), 1) % RHD).astype(BF16)
        for c in range(tm // RHD):
            for t in range(SCAN_TPB):
                v_c = v[c * RHD:(c + 1) * RHD, t * SCAN_W:(t + 1) * SCAN_W].astype(BF16)
                blockdiag = jnp.where(same_head, jnp.concatenate([v_c] * SCAN_HPT, axis=0), jnp.zeros((), BF16))
                vt_o[c, t] = _dot_nt(eye_rep, blockdiag).astype(vt_o.dtype)


def rwkv_prep(g, rw, prev, p, mw, ones_bd):
    vec = lambda a: a.reshape(1, -1)
    tshape = jax.ShapeDtypeStruct(g.tmaj_shape(RDIM), F32)
    seq = g.S > 1
    out_specs = [g.tmaj_spec(RDIM)] * 6 + [g.rows(RDIM)]
    out_shape = [tshape] * 6 + [jax.ShapeDtypeStruct((g.M, RDIM), F32)]
    scratch = []
    if seq:
        assert g.tm % RHD == 0
        cpt = g.tm // RHD
        prev = jnp.pad(prev[:, None, :], ((0, 0), (7, 0), (0, 0)))
        prev_spec = pl.BlockSpec((1, 8, RWKV_IN), lambda b, s: (b, 0, 0))
        out_specs.append(pl.BlockSpec((cpt, SCAN_TPB, RHD, SCAN_W), lambda b, s: (s, b, 0, 0)))
        out_shape.append(jax.ShapeDtypeStruct((g.S // RHD, g.B * SCAN_TPB, RHD, SCAN_W), BF16))
        scratch = [pltpu.VMEM((8, RWKV_IN), F32)]
    else:
        prev_spec = g.rows(RWKV_IN)
    return pl.pallas_call(
        functools.partial(_rwkv_prep_kernel, precise=g.precise, seq=seq),
        grid=g.grid,
        in_specs=[g.rows(RWKV_IN), prev_spec, g.full((1, RWKV_IN)), g.full((1, RDIM)), g.full((1, RDIM)),
                  g.full((1, RDIM)), g.full((1, RDIM)), g.full((DECAY_LORA, RDIM)), g.full((AAA_LORA, RDIM)),
                  g.full((GATE_LORA, RDIM)), g.full((RDIM, RDIM))],
        out_specs=out_specs,
        out_shape=out_shape,
        scratch_shapes=scratch,
        compiler_params=_cparams("parallel", "arbitrary"),
        name="rwkv_prep",
    )(rw, prev, vec(p["mu"]), vec(p["w0"]), vec(p["a0"]), vec(p["k_k"]), vec(p["k_a"]), mw["w2"], mw["a2"], mw["g2"],
      ones_bd)


SCAN_NB = 16
SCAN_HPT = 4
SCAN_W = SCAN_HPT * RHD
SCAN_TPB = RDIM // SCAN_W
SCAN_GB = 8


def _scan_kernel(r_ref, w_ref, k_ref, kk_ref, b_ref, vt_ref, s0_ref, yt_ref, sout_ref, s_scr, *, tc_len, ng, precise):
    tc = pl.program_id(1)

    def pick(x, w01):
        return _split_dot(x, w01) if precise else _dot(x.astype(BF16), w01)

    @pl.when(tc == 0)
    def _():
        s_scr[...] = s0_ref[...]

    ri = lax.broadcasted_iota(jnp.int32, (SCAN_W, SCAN_W), 0)
    ci = lax.broadcasted_iota(jnp.int32, (SCAN_W, SCAN_W), 1)
    same_head = (ri // RHD) == (ci // RHD)
    ones_bd = same_head.astype(BF16)
    lane = lax.broadcasted_iota(jnp.int32, (RHD, SCAN_W), 1) % RHD
    yt_ref[...] = jnp.zeros(yt_ref.shape, F32)
    gb = min(SCAN_GB, ng)

    def step(tt, carry):
        e_t = (same_head & ((ri % RHD) == tt)).astype(BF16)
        sel = lane == tt

        def rows(ref, g0):
            return jnp.stack([jnp.broadcast_to(ref[tt, pl.ds(g0 + i, 1), :], (RHD, SCAN_W)) for i in range(gb)])

        def issue(g0):
            sa = pick((s_scr[g0:g0 + gb] * rows(kk_ref, g0)).reshape(gb * RHD, SCAN_W), ones_bd)
            vcol = pick(vt_ref[0, g0:g0 + gb].reshape(gb * RHD, SCAN_W), e_t)
            return sa.reshape(gb, RHD, SCAN_W), vcol.reshape(gb, RHD, SCAN_W)

        pend = issue(0)
        y_pend = None
        for g0 in range(0, ng, gb):
            nxt = issue(g0 + gb) if g0 + gb < ng else None
            sa, vcol = pend
            s = s_scr[g0:g0 + gb] * rows(w_ref, g0) - sa * rows(b_ref, g0) + vcol * rows(k_ref, g0)
            s_scr[g0:g0 + gb] = s
            y = pick((s * rows(r_ref, g0)).reshape(gb * RHD, SCAN_W), ones_bd).reshape(gb, RHD, SCAN_W)
            if y_pend is not None:
                gp, yp = y_pend
                yt_ref[0, gp:gp + gb] = jnp.where(sel, yp, yt_ref[0, gp:gp + gb])
            y_pend = (g0, y)
            pend = nxt
        gp, yp = y_pend
        yt_ref[0, gp:gp + gb] = jnp.where(sel, yp, yt_ref[0, gp:gp + gb])
        return carry

    lax.fori_loop(0, tc_len, step, 0, unroll=2 if tc_len % 2 == 0 else 1)

    @pl.when(tc == pl.num_programs(1) - 1)
    def _():
        sout_ref[...] = s_scr[...]


def rwkv_scan(B, S, r, w, k, kk, b, vt, s0, precise):
    tc_len = min(RHD, S)
    nb = min(SCAN_NB, B)
    ng = nb * SCAN_TPB
    nchunks = S // tc_len
    row_spec = pl.BlockSpec((tc_len, ng, SCAN_W), lambda bg, c: (c, bg, 0))
    st_spec = pl.BlockSpec((ng, RHD, SCAN_W), lambda bg, c: (bg, 0, 0))
    ch_spec = pl.BlockSpec((1, ng, RHD, SCAN_W), lambda bg, c: (c, bg, 0, 0))
    return pl.pallas_call(
        functools.partial(_scan_kernel, tc_len=tc_len, ng=ng, precise=precise),
        grid=(B // nb, nchunks),
        in_specs=[row_spec] * 5 + [ch_spec, st_spec],
        out_specs=[ch_spec, st_spec],
        out_shape=[jax.ShapeDtypeStruct((nchunks, B * SCAN_TPB, RHD, SCAN_W), F32),
                   jax.ShapeDtypeStruct((B * SCAN_TPB, RHD, SCAN_W), F32)],
        scratch_shapes=[pltpu.VMEM((ng, RHD, SCAN_W), F32)],
        compiler_params=_cparams("parallel", "arbitrary"),
        name="rwkv_scan",
    )(r, w, k, kk, b, vt, s0)


def _rwkv_post_kernel(y_ref, r_ref, k_ref, v_ref, g_ref, lnw_ref, lnb_ref, rk_ref, ones_ref, o_ref):
    ones = ones_ref[...]
    y = y_ref[...]
    mu = _split_dot(y, ones) * (1.0 / RHD)
    yc = y - mu
    var = _split_dot(yc * yc, ones) * (1.0 / RHD)
    yn = yc * lax.rsqrt(var + RWKV_LN_EPS) * lnw_ref[...] + lnb_ref[...]
    v = v_ref[...]
    bonus = _split_dot(r_ref[...] * k_ref[...] * rk_ref[...], ones)
    o_ref[...] = ((yn + bonus * v) * g_ref[...]).astype(o_ref.dtype)


def rwkv_post(g, y, r, k, v, gate, p, ones_bd):
    vec = lambda a: a.reshape(1, -1)
    return pl.pallas_call(
        _rwkv_post_kernel,
        grid=g.grid,
        in_specs=[g.rows(RDIM), g.tmaj_spec(RDIM), g.tmaj_spec(RDIM), g.tmaj_spec(RDIM), g.rows(RDIM),
                  g.full((1, RDIM)), g.full((1, RDIM)), g.full((1, RDIM)), g.full((RDIM, RDIM))],
        out_specs=g.rows(RDIM),
        out_shape=jax.ShapeDtypeStruct((g.M, RDIM), g.act_dtype),
        compiler_params=_cparams("parallel", "parallel"),
        name="rwkv_post",
    )(y, r, k, v, gate, vec(p["ln_w"]), vec(p["ln_b"]), vec(p["r_k"]), ones_bd)


def _ssm_gate_norm(y, z, nw):
    y = y * _silu(z)
    gw = SDIM // SGROUPS
    parts = [_rms(y[:, i * gw:(i + 1) * gw]) for i in range(SGROUPS)]
    return jnp.concatenate(parts, axis=-1) * nw


def _ssd_kernel(xbc_ref, z_ref, dt_ref, tail0_ref, cw_ref, cb_ref, dtb_ref, a_ref, dsk_ref, nw_ref,
                o_ref, hout_ref, tail_scr, h_scr, y_scr):
    c = pl.program_id(1)
    L = SCHUNK

    @pl.when(c == 0)
    def _():
        tail_scr[...] = tail0_ref[0]
        h_scr[...] = jnp.zeros(h_scr.shape, F32)

    xbc = xbc_ref[...]
    tail = tail_scr[...]
    row8 = lax.broadcasted_iota(jnp.int32, (8, SCD), 0)
    conv = cb_ref[...] + xbc * cw_ref[SCONV - 1:SCONV, :]
    for sft in range(1, SCONV):
        sh = pltpu.roll(xbc, sft, 0)
        top = jnp.where(row8 < sft, pltpu.roll(tail, sft, 0), sh[0:8])
        sh = jnp.concatenate([top, sh[8:]], axis=0)
        conv = conv + sh * cw_ref[SCONV - 1 - sft:SCONV - sft, :]
    tail_scr[...] = xbc[L - 8:L]
    act = _silu(conv)
    xa = act[:, 0:SDIM]
    lane = lax.broadcasted_iota(jnp.int32, (L, 128), 1)
    dt_valid = (lane >= DT_LANE) & (lane < DT_LANE + SHEADS)
    dtt = jnp.where(dt_valid, _softplus(dt_ref[...] + dtb_ref[...]), 0.0)
    adt = dtt * a_ref[...]
    ri = lax.broadcasted_iota(jnp.int32, (L, L), 0)
    ci = lax.broadcasted_iota(jnp.int32, (L, L), 1)
    causal = ri >= ci
    cs = _split3_dot_left(causal.astype(BF16), adt)
    cs_t = cs.T
    total = cs[L - 1:L, :]
    for grp in range(SGROUPS):
        bm = act[:, SDIM + grp * SN:SDIM + (grp + 1) * SN]
        cm = act[:, SDIM + SGROUPS * SN + grp * SN:SDIM + SGROUPS * SN + (grp + 1) * SN]
        cb = _dot_nt(cm.astype(BF16), bm.astype(BF16))
        for hh in range(SHEADS // SGROUPS):
            h = grp * (SHEADS // SGROUPS) + hh
            ln = DT_LANE + h
            col = cs[:, ln:ln + 1]
            row = cs_t[ln:ln + 1, :]
            tot = total[:, ln:ln + 1]
            lmat = jnp.where(causal, jnp.exp(jnp.where(causal, col - row, 0.0)), 0.0)
            xh = xa[:, h * SP:(h + 1) * SP]
            xdt = (xh * dtt[:, ln:ln + 1]).astype(BF16)
            hprev = h_scr[h]
            y = _dot((cb * lmat).astype(BF16), xdt)
            y = y + _dot((cm * jnp.exp(col)).astype(BF16), hprev.astype(BF16))
            y_scr[:, h * SP:(h + 1) * SP] = y + dsk_ref[:, h * SP:(h + 1) * SP] * xh
            bdec = (bm * jnp.exp(tot - col)).T.astype(BF16)
            h_scr[h] = hprev * jnp.exp(tot) + _dot(bdec, xdt)
    o_ref[...] = _ssm_gate_norm(y_scr[...], z_ref[...], nw_ref[...]).astype(o_ref.dtype)

    @pl.when(c == pl.num_programs(1) - 1)
    def _():
        hout_ref[0] = h_scr[...]


def ssd_prompt(B, S, xbc, z, seg_a, tail0, p):
    nc = S // SCHUNK
    L = SCHUNK
    full = lambda shape: pl.BlockSpec(shape, lambda b, c: (0,) * len(shape))
    return pl.pallas_call(
        _ssd_kernel,
        grid=(B, nc),
        in_specs=[pl.BlockSpec((L, SCD), lambda b, c: (b * nc + c, 0)),
                  pl.BlockSpec((L, SDIM), lambda b, c: (b * nc + c, 0)),
                  pl.BlockSpec((L, 128), lambda b, c: (b * nc + c, DT_TILE)),
                  pl.BlockSpec((1, 8, SCD), lambda b, c: (b, 0, 0)),
                  full((SCONV, SCD)), full((1, SCD)), full((1, 128)), full((1, 128)), full((1, SDIM)),
                  full((1, SDIM))],
        out_specs=[pl.BlockSpec((L, SDIM), lambda b, c: (b * nc + c, 0)),
                   pl.BlockSpec((1, SHEADS, SN, SP), lambda b, c: (b, 0, 0, 0))],
        out_shape=[jax.ShapeDtypeStruct((B * S, SDIM), BF16), jax.ShapeDtypeStruct((B, SHEADS, SN, SP), F32)],
        scratch_shapes=[pltpu.VMEM((8, SCD), F32), pltpu.VMEM((SHEADS, SN, SP), F32), pltpu.VMEM((L, SDIM), F32)],
        compiler_params=_cparams("parallel", "arbitrary"),
        name="ssd_prompt",
    )(xbc, z, seg_a, tail0, p["conv_w"], p["conv_b"].reshape(1, -1), p["dtb128"], p["a128"], p["dskip512"],
      p["ssm_norm"].reshape(1, -1))


def _ssm_step_pre_kernel(xbc_ref, c0_ref, c1_ref, c2_ref, dt_ref, cw_ref, cb_ref, dtb_ref, a_ref,
                         act_ref, xdt_ref, dec_ref):
    conv = (cb_ref[...] + c0_ref[...] * cw_ref[0:1, :] + c1_ref[...] * cw_ref[1:2, :] + c2_ref[...] * cw_ref[2:3, :]
            + xbc_ref[...] * cw_ref[3:4, :])
    act = _silu(conv)
    act_ref[...] = act
    dtt = _softplus(dt_ref[...] + dtb_ref[...])
    dec_ref[...] = jnp.exp(dtt * a_ref[...])
    for h in range(SHEADS):
        xdt_ref[:, h * SP:(h + 1) * SP] = act[:, h * SP:(h + 1) * SP] * dtt[:, DT_LANE + h:DT_LANE + h + 1]


def ssm_step_pre(g, xbc, c0, c1, c2, seg_a, p):
    return pl.pallas_call(
        _ssm_step_pre_kernel,
        grid=g.grid,
        in_specs=[g.rows(SCD)] * 4 + [g.rows(128, DT_TILE), g.full((SCONV, SCD)), g.full((1, SCD)),
                                      g.full((1, 128)), g.full((1, 128))],
        out_specs=[g.rows(SCD), g.rows(SDIM), g.rows(128)],
        out_shape=[jax.ShapeDtypeStruct((g.M, SCD), F32), jax.ShapeDtypeStruct((g.M, SDIM), F32),
                   jax.ShapeDtypeStruct((g.M, 128), F32)],
        compiler_params=_cparams("parallel", "parallel"),
        name="ssm_step_pre",
    )(xbc, c0, c1, c2, seg_a, p["conv_w"], p["conv_b"].reshape(1, -1), p["dtb128"], p["a128"])


SSM_STEP_BT = 8


def _ssm_step_kernel(h0_ref, xb_ref, dec_ref, bm_ref, cm_ref, h1_ref, y_ref):
    bt = SSM_STEP_BT
    rp = lax.broadcasted_iota(jnp.int32, (SP, 128), 0)
    lp = lax.broadcasted_iota(jnp.int32, (SP, 128), 1)
    pick = [lp == rp, lp == rp + SP]
    hpg = SHEADS // SGROUPS
    for i in range(bt):
        for hp in range(SHEADS // 2):
            yrow = jnp.zeros((1, 128), F32)
            for e in range(2):
                h = hp * 2 + e
                grp = h // hpg
                bm = bm_ref[i:i + 1, grp * SN:(grp + 1) * SN]
                cm = cm_ref[i:i + 1, grp * SN:(grp + 1) * SN]
                h1 = h0_ref[i, h] * dec_ref[i, h] + xb_ref[i, h] * bm
                h1_ref[i, h] = h1
                ycol = jnp.sum(h1 * cm, axis=-1, keepdims=True)
                yrow = yrow + jnp.sum(jnp.where(pick[e], ycol, 0.0), axis=0, keepdims=True)
            y_ref[i:i + 1, hp * 128:(hp + 1) * 128] = yrow


def ssm_step(B, layer, h0, xb, dec, bm, cm):
    bt = SSM_STEP_BT
    st = pl.BlockSpec((bt, SHEADS, SP, SN), lambda i: (i, 0, 0, 0))
    return pl.pallas_call(
        _ssm_step_kernel,
        grid=(B // bt,),
        in_specs=[pl.BlockSpec((None, bt, SHEADS, SP, SN), lambda i: (layer, i, 0, 0, 0)), st,
                  pl.BlockSpec((bt, SHEADS, 1, SN), lambda i: (i, 0, 0, 0)),
                  pl.BlockSpec((bt, SGROUPS * SN), lambda i: (i, 0)), pl.BlockSpec((bt, SGROUPS * SN), lambda i: (i, 0))],
        out_specs=[st, pl.BlockSpec((bt, SDIM), lambda i: (i, 0))],
        out_shape=[jax.ShapeDtypeStruct((B, SHEADS, SP, SN), F32), jax.ShapeDtypeStruct((B, SDIM), F32)],
        compiler_params=_cparams("parallel"),
        name="ssm_step",
    )(h0, xb, dec, bm, cm)


def _ssm_step_post_kernel(y_ref, x_ref, z_ref, dsk_ref, nw_ref, o_ref):
    y = y_ref[...] + dsk_ref[...] * x_ref[...]
    o_ref[...] = _ssm_gate_norm(y, z_ref[...], nw_ref[...]).astype(o_ref.dtype)


def ssm_step_post(g, y, act, z, p):
    return pl.pallas_call(
        _ssm_step_post_kernel,
        grid=g.grid,
        in_specs=[g.rows(SDIM), g.rows(SDIM), g.rows(SDIM), g.full((1, SDIM)), g.full((1, SDIM))],
        out_specs=g.rows(SDIM),
        out_shape=jax.ShapeDtypeStruct((g.M, SDIM), g.act_dtype),
        compiler_params=_cparams("parallel", "parallel"),
        name="ssm_step_post",
    )(y, act, z, p["dskip512"], p["ssm_norm"].reshape(1, -1))


def _merge_kernel(a_ref, r_ref, s_ref, gate_ref, x_ref, ga_ref, wb_ref, wo_ref, o_ref, *, precise):
    acc = None
    for i, br in enumerate((a_ref, r_ref, s_ref)):
        t = _sigmoid(gate_ref[:, i * D_MODEL:(i + 1) * D_MODEL]) * _mm(br[...], wb_ref[i], precise)
        acc = t if acc is None else acc + t
    mix = _mm(acc, wo_ref[...], precise)
    o_ref[...] = x_ref[...] + ga_ref[0] * mix


def merge_branches(g, a_out, r_out, s_out, gate, x, ga, wb, wo):
    return pl.pallas_call(
        functools.partial(_merge_kernel, precise=g.precise),
        grid=g.grid,
        in_specs=[g.rows(512), g.rows(512), g.rows(512), g.rows(NBRANCH * D_MODEL), g.rows(D_MODEL),
                  g.mod_spec(D_MODEL), g.full((NBRANCH, 512, D_MODEL)), g.full((D_MODEL, D_MODEL))],
        out_specs=g.rows(D_MODEL),
        out_shape=jax.ShapeDtypeStruct((g.M, D_MODEL), F32),
        compiler_params=_cparams("parallel", "parallel"),
        name="merge_branches",
    )(a_out, r_out, s_out, gate, x, g.mod_array(ga), wb, wo)


def _ffn_kernel(x_ref, nw_ref, sc_ref, sh_ref, gf_ref, w1_ref, w3_ref, w2_ref, o_ref, h_scr, acc_scr, *, precise):
    j = pl.program_id(2)

    @pl.when(j == 0)
    def _():
        h_scr[...] = (_rms(x_ref[...]) * nw_ref[...] * (1.0 + sc_ref[0]) + sh_ref[0]).astype(h_scr.dtype)
        acc_scr[...] = jnp.zeros(acc_scr.shape, F32)

    h = h_scr[...]
    u = _silu(_mm(h, w1_ref[...], precise)) * _mm(h, w3_ref[...], precise)
    acc_scr[...] += _mm(u, w2_ref[...], precise)

    @pl.when(j == pl.num_programs(2) - 1)
    def _():
        o_ref[...] = x_ref[...] + gf_ref[0] * acc_scr[...]


def dense_ffn(g, x, nw, sc, sh, gf, w1, w3, w2, tf):
    dff = w1.shape[1]
    lift = lambda spec: pl.BlockSpec(spec.block_shape, lambda b, s, j, f=spec.index_map: f(b, s))
    return pl.pallas_call(
        functools.partial(_ffn_kernel, precise=g.precise),
        grid=g.grid + (dff // tf,),
        in_specs=[lift(g.rows(D_MODEL)), lift(g.full((1, D_MODEL))), lift(g.mod_spec(D_MODEL)),
                  lift(g.mod_spec(D_MODEL)), lift(g.mod_spec(D_MODEL)),
                  pl.BlockSpec((D_MODEL, tf), lambda b, s, j: (0, j)), pl.BlockSpec((D_MODEL, tf), lambda b, s, j: (0, j)),
                  pl.BlockSpec((tf, D_MODEL), lambda b, s, j: (j, 0))],
        out_specs=lift(g.rows(D_MODEL)),
        out_shape=jax.ShapeDtypeStruct((g.M, D_MODEL), F32),
        scratch_shapes=[pltpu.VMEM((g.tm, D_MODEL), g.act_dtype), pltpu.VMEM((g.tm, D_MODEL), F32)],
        compiler_params=_cparams("parallel", "parallel", "arbitrary"),
        name="dense_ffn",
    )(x, nw.reshape(1, -1), g.mod_array(sc), g.mod_array(sh), g.mod_array(gf), w1, w3, w2)


def _router_kernel(x_ref, nw_ref, sc_ref, sh_ref, wr_ref, rb_ref, h_ref, logit_ref):
    h = _rms(x_ref[...]) * nw_ref[...] * (1.0 + sc_ref[0]) + sh_ref[0]
    h_ref[...] = h.astype(BF16)
    logit_ref[...] = _dot3(h, wr_ref[...]) + rb_ref[...]


def moe_router(g, x, nw, sc, sh, wr, rb):
    return pl.pallas_call(
        _router_kernel,
        grid=g.grid,
        in_specs=[g.rows(D_MODEL), g.full((1, D_MODEL)), g.mod_spec(D_MODEL), g.mod_spec(D_MODEL),
                  g.full((D_MODEL, 128)), g.full((1, 128))],
        out_specs=[g.rows(D_MODEL), g.rows(128)],
        out_shape=[jax.ShapeDtypeStruct((g.M, D_MODEL), BF16), jax.ShapeDtypeStruct((g.M, 128), F32)],
        compiler_params=_cparams("parallel", "parallel"),
        name="moe_router",
    )(x, nw.reshape(1, -1), g.mod_array(sc), g.mod_array(sh), wr, rb)


def _expert_kernel(te_ref, tv_ref, h_ref, w1_ref, w3_ref, w2_ref, o_ref, acc_scr):
    i = pl.program_id(0)
    j = pl.program_id(1)

    @pl.when(j == 0)
    def _():
        acc_scr[...] = jnp.zeros(acc_scr.shape, F32)

    @pl.when(tv_ref[i] > 0)
    def _():
        h = h_ref[...]
        u = (_silu(_dot(h, w1_ref[...])) * _dot(h, w3_ref[...])).astype(BF16)
        acc_scr[...] += _dot(u, w2_ref[...])

    @pl.when(j == pl.num_programs(1) - 1)
    def _():
        o_ref[...] = acc_scr[...]


def expert_ffn(tile_expert, tile_valid, h_sorted, w1, w3, w2, tm, tf):
    rows = h_sorted.shape[0]
    dffe = w1.shape[2]
    grid_spec = pltpu.PrefetchScalarGridSpec(
        num_scalar_prefetch=2,
        grid=(rows // tm, dffe // tf),
        in_specs=[pl.BlockSpec((tm, D_MODEL), lambda i, j, te, tv: (i, 0)),
                  pl.BlockSpec((None, D_MODEL, tf), lambda i, j, te, tv: (te[i], 0, j)),
                  pl.BlockSpec((None, D_MODEL, tf), lambda i, j, te, tv: (te[i], 0, j)),
                  pl.BlockSpec((None, tf, D_MODEL), lambda i, j, te, tv: (te[i], j, 0))],
        out_specs=pl.BlockSpec((tm, D_MODEL), lambda i, j, te, tv: (i, 0)),
        scratch_shapes=[pltpu.VMEM((tm, D_MODEL), F32)],
    )
    return pl.pallas_call(
        _expert_kernel,
        grid_spec=grid_spec,
        out_shape=jax.ShapeDtypeStruct((rows, D_MODEL), F32),
        compiler_params=_cparams("parallel", "arbitrary"),
        name="expert_ffn",
    )(tile_expert, tile_valid, h_sorted, w1, w3, w2)


def _combine_kernel(x_ref, gf_ref, y0_ref, y1_ref, wt_ref, o_ref):
    wt = wt_ref[...]
    f = wt[:, 0:1] * y0_ref[...] + wt[:, 1:2] * y1_ref[...]
    o_ref[...] = x_ref[...] + gf_ref[0] * f


def moe_combine(g, x, gf, y0, y1, wt):
    return pl.pallas_call(
        _combine_kernel,
        grid=g.grid,
        in_specs=[g.rows(D_MODEL), g.mod_spec(D_MODEL), g.rows(D_MODEL), g.rows(D_MODEL), g.rows(128)],
        out_specs=g.rows(D_MODEL),
        out_shape=jax.ShapeDtypeStruct((g.M, D_MODEL), F32),
        compiler_params=_cparams("parallel", "parallel"),
        name="moe_combine",
    )(x, g.mod_array(gf), y0, y1, wt)


def moe_ffn(g, x, nw, sc, sh, gf, wr, rb, w1, w3, w2):
    M = g.M
    h, logits = moe_router(g, x, nw, sc, sh, wr, rb)
    top_v, top_i = lax.top_k(logits[:, :NE], TOPK)
    top_w = jax.nn.softmax(top_v, axis=-1)
    tm = min(512, max(128, M // 4))
    tf = 1792
    flat_e = top_i.reshape(-1)
    onehot = (flat_e[:, None] == jnp.arange(NE)[None, :]).astype(jnp.int32)
    rank = jnp.take_along_axis(jnp.cumsum(onehot, axis=0) - onehot, flat_e[:, None], axis=1)[:, 0]
    counts = jnp.sum(onehot, axis=0)
    padded = ((counts + tm - 1) // tm) * tm
    starts = jnp.cumsum(padded) - padded
    pos = starts[flat_e] + rank
    n_rows = M * TOPK + NE * tm
    row_token = jnp.zeros((n_rows,), jnp.int32).at[pos].set(jnp.arange(M * TOPK, dtype=jnp.int32) // TOPK)
    tile_start = jnp.arange(n_rows // tm, dtype=jnp.int32) * tm
    ends = starts + padded
    tile_expert = jnp.minimum(jnp.sum((tile_start[:, None] >= ends[None, :]).astype(jnp.int32), axis=1), NE - 1)
    tile_valid = (tile_start < ends[NE - 1]).astype(jnp.int32)
    h_sorted = jnp.take(h, row_token, axis=0)
    y_sorted = expert_ffn(tile_expert.astype(jnp.int32), tile_valid, h_sorted, w1, w3, w2, tm, tf)
    pos2 = pos.reshape(M, TOPK)
    y0 = jnp.take(y_sorted, pos2[:, 0], axis=0)
    y1 = jnp.take(y_sorted, pos2[:, 1], axis=0)
    wt = jnp.pad(top_w, ((0, 0), (0, 128 - TOPK)))
    return moe_combine(g, x, gf, y0, y1, wt)


def _final_norm_kernel(x_ref, w_ref, o_ref):
    o_ref[...] = _rms(x_ref[...]) * w_ref[...]


def final_norm(g, x, w):
    return pl.pallas_call(
        _final_norm_kernel,
        grid=g.grid,
        in_specs=[g.rows(D_MODEL), g.full((1, D_MODEL))],
        out_specs=g.rows(D_MODEL),
        out_shape=jax.ShapeDtypeStruct((g.M, D_MODEL), F32),
        compiler_params=_cparams("parallel", "parallel"),
        name="final_norm",
    )(x, w.reshape(1, -1))


def _rot_half_cols(w):
    half = ROPE // 2
    return jnp.concatenate([-w[..., half:], w[..., :half]], axis=-1)


def _pack_layer(l, W):
    p = {}
    w_in = W["w_in"][l]
    o = 0
    q_c, kv_c, kr = w_in[:, 0:QL], w_in[:, QL:QL + KVL], w_in[:, QL + KVL:QL + KVL + ROPE]
    o = QL + KVL + ROPE
    rw = w_in[:, o:o + RWKV_IN]
    o += RWKV_IN
    z = w_in[:, o:o + SDIM]
    o += SDIM
    xbc = w_in[:, o:o + SCD]
    o += SCD
    dt = w_in[:, o:o + SHEADS]
    o += SHEADS
    gate = w_in[:, o:]
    seg_a = jnp.concatenate([q_c, kv_c, kr, dt, _rot_half_cols(kr),
                             jnp.zeros((D_MODEL, SEG_A - (QL + KVL + 2 * ROPE + SHEADS)), F32)], axis=1)
    mw = {}
    mw["w_in"] = jnp.concatenate([seg_a, rw, z, xbc, gate], axis=1)
    wq = W["mla_w_uq"][l].reshape(QL, H, NOPE + ROPE)
    pe = wq[:, :, NOPE:]
    mw["wuq"] = jnp.concatenate([wq[:, :, :NOPE].reshape(QL, H * NOPE), pe.reshape(QL, H * ROPE),
                                 _rot_half_cols(pe).reshape(QL, H * ROPE)], axis=1)
    mw["wuk"] = jnp.transpose(W["mla_w_uk"][l], (1, 2, 0))
    mw["wuv"] = jnp.transpose(W["mla_w_uv"][l], (1, 0, 2))
    mw["w2"], mw["a2"], mw["g2"] = W["rwkv_w2"][l], W["rwkv_a2"][l], W["rwkv_g2"][l]
    mw["w_branch"], mw["w_out"] = W["w_branch"][l], W["w_out"][l]
    if l % 2 == 0:
        mw["ffn"] = tuple(W[n][l // 2] for n in ("ffn_w1", "ffn_w3", "ffn_w2"))
    p["mw"] = mw
    p["mw16"] = jax.tree_util.tree_map(lambda a: a.astype(BF16), mw)
    p["q_norm"], p["kv_norm"] = W["mla_q_norm"][l], W["mla_kv_norm"][l]
    p["rwkv"] = dict(mu=W["rwkv_mu"][l], w0=W["rwkv_w0"][l], a0=W["rwkv_a0"][l], k_k=W["rwkv_k_k"][l],
                     k_a=W["rwkv_k_a"][l], ln_w=W["rwkv_ln_w"][l], ln_b=W["rwkv_ln_b"][l],
                     r_k=W["rwkv_r_k"][l].reshape(-1))
    lanes = jnp.arange(128)
    head_lane = (lanes >= DT_LANE) & (lanes < DT_LANE + SHEADS)
    idx = jnp.clip(lanes - DT_LANE, 0, SHEADS - 1)
    p["ssm"] = dict(conv_w=W["ssm_conv_w"][l], conv_b=W["ssm_conv_b"][l],
                    dtb128=jnp.where(head_lane, W["ssm_dt_bias"][l][idx], 0.0).reshape(1, 128),
                    a128=jnp.where(head_lane, -jnp.exp(W["ssm_a_log"][l][idx]), 0.0).reshape(1, 128),
                    dskip512=jnp.repeat(W["ssm_d"][l], SP).reshape(1, SDIM), ssm_norm=W["ssm_norm"][l])
    p["norm_attn"], p["norm_ffn"] = W["norm_attn"][l], W["norm_ffn"][l]
    p["w_ada"], p["b_ada"] = W["w_ada"][l], W["b_ada"][l]
    if l % 2 == 1:
        wr = jnp.pad(W["moe_router"][l // 2], ((0, 0), (0, 128 - NE)))
        p["router"] = (wr, jnp.pad(W["moe_router_b"][l // 2], (0, 128 - NE)).reshape(1, 128))
        p["moe"] = tuple(W[n][l // 2].astype(BF16) for n in ("moe_w1", "moe_w3", "moe_w2"))
    return p


def _rope_tables(pos):
    half = ROPE // 2
    freq = ROPE_THETA ** (-jnp.arange(half, dtype=F32) / half)
    ang = pos.astype(F32)[:, None] * freq[None, :]
    cos = jnp.concatenate([jnp.cos(ang)] * 2, axis=-1)
    sin = jnp.concatenate([jnp.sin(ang)] * 2, axis=-1)
    return cos, sin, jnp.tile(cos, (1, H)), jnp.tile(sin, (1, H))


def _trunk(x3, c, pos, paged, shift0, wkv0, conv0, ssm0, layers, norm_final):
    B, S, _ = x3.shape
    M = B * S
    g = _Group(B, S, 512)
    g_win = _Group(B, S, 256)
    x = x3.reshape(M, D_MODEL)
    tabs = _rope_tables(pos if S > 1 else jnp.broadcast_to(pos, (M,)))
    ri = jnp.arange(RDIM)
    ones_bd = ((ri[:, None] // RHD) == (ri[None, :] // RHD)).astype(BF16)
    outs = [[] for _ in range(6)]
    for l, p in enumerate(layers):
        mw = p["mw"] if g.precise else p["mw16"]
        ada = ada_matmul(c, p["w_ada"], p["b_ada"])
        sh_a, sc_a, g_a, sh_f, sc_f, g_f = jnp.split(ada, 6, axis=-1)
        seg_a, rw, z, xbc, gate = win_project(g_win, x, p["norm_attn"], sc_a, sh_a, mw["w_in"])

        qc, kc, ckv, kpe = mla_prep(g, seg_a, tabs, p["q_norm"], p["kv_norm"], mw["wuq"], mw["wuk"])
        if paged is None:
            a_out = mla_attention(B, S, qc, kc, mw["wuv"])
        else:
            cache_ckv, cache_kpe_t, page_table = paged
            a_out = paged_attention(l, page_table, qc, kc, cache_ckv, cache_kpe_t, mw["wuv"])

        rw3 = rw.reshape(B, S, RWKV_IN)
        tc_len = min(RHD, S)
        nch = S // tc_len
        tpb, hpt = SCAN_TPB, SCAN_HPT
        as_rows = lambda a: a.reshape(S, B * tpb, SCAN_W)
        if S > 1:
            r_t, w_t, k_t, v_t, kk_t, b_t, gate_r, vt = rwkv_prep(g, rw, shift0[l], p["rwkv"], mw, ones_bd)
        else:
            r_t, w_t, k_t, v_t, kk_t, b_t, gate_r = rwkv_prep(g, rw, shift0[l], p["rwkv"], mw, ones_bd)
            vt = v_t.reshape(nch, tc_len, B, tpb, hpt, RHD).transpose(0, 2, 3, 5, 4, 1)
            vt = jnp.pad(vt, ((0, 0),) * 5 + ((0, RHD - tc_len),)).reshape(nch, B * tpb, RHD, SCAN_W)
        s0 = wkv0[l].reshape(B, tpb, hpt, RHD, RHD).transpose(0, 1, 3, 2, 4).reshape(B * tpb, RHD, SCAN_W)
        yt, s1 = rwkv_scan(B, S, as_rows(r_t), as_rows(w_t), as_rows(k_t), as_rows(kk_t), as_rows(b_t), vt, s0,
                           g.precise)
        y = yt.reshape(nch, B, tpb, RHD, hpt, RHD)[..., :tc_len].transpose(1, 0, 5, 2, 4, 3).reshape(M, RDIM)
        wkv1 = s1.reshape(B, tpb, RHD, hpt, RHD).transpose(0, 1, 3, 2, 4).reshape(B, tpb * hpt, RHD, RHD)
        r_out = rwkv_post(g, y, r_t, k_t, v_t, gate_r, p["rwkv"], ones_bd)
        shift1 = rw3[:, -1]

        ps = p["ssm"]
        if S > 1:
            tail0 = jnp.pad(conv0[l], ((0, 0), (8 - (SCONV - 1), 0), (0, 0)))
            s_out, h_t = ssd_prompt(B, S, xbc, z, seg_a, tail0, ps)
            ssm1 = jnp.swapaxes(h_t, 2, 3)
            conv1 = xbc.reshape(B, S, SCD)[:, S - (SCONV - 1):]
        else:
            c0, c1, c2 = conv0[l][:, 0], conv0[l][:, 1], conv0[l][:, 2]
            act, xdt, dec128 = ssm_step_pre(g, xbc, c0, c1, c2, seg_a, ps)
            xb = jnp.broadcast_to(xdt.reshape(B, SHEADS, SP, 1), (B, SHEADS, SP, SN))
            dec = jnp.broadcast_to(dec128[:, DT_LANE:DT_LANE + SHEADS].reshape(B, SHEADS, 1, 1), (B, SHEADS, 1, SN))
            ssm1, y_s = ssm_step(B, l, ssm0, xb, dec, act[:, SDIM:SDIM + SGROUPS * SN], act[:, SDIM + SGROUPS * SN:])
            s_out = ssm_step_post(g, y_s, act[:, :SDIM], z, ps)
            conv1 = jnp.concatenate([conv0[l][:, 1:], xbc[:, None, :]], axis=1)

        x = merge_branches(g, a_out, r_out, s_out, gate, x, g_a, mw["w_branch"], mw["w_out"])

        if l % 2 == 0:
            w1, w3, w2 = mw["ffn"]
            x = dense_ffn(g, x, p["norm_ffn"], sc_f, sh_f, g_f, w1, w3, w2, w1.shape[1] // 2)
        else:
            w1, w3, w2 = p["moe"]
            x = moe_ffn(g, x, p["norm_ffn"], sc_f, sh_f, g_f, *p["router"], w1, w3, w2)
        for lst, v in zip(outs, (ckv.reshape(B, S, KVL), kpe.reshape(B, S, ROPE), shift1, wkv1, conv1, ssm1)):
            lst.append(v)
    y = final_norm(g, x, norm_final).reshape(B, S, D_MODEL)
    return y, [jnp.stack(v) for v in outs]


def kernel(x_prompt, x_sample, cache_ckv, cache_kpe, state_rwkv_shift, state_rwkv_wkv, state_ssm_conv, state_ssm, page_table, c_prompt, c_sample, w_ada, b_ada, norm_attn, norm_ffn, norm_final, w_in, mla_q_norm, mla_w_uq, mla_kv_norm, mla_w_uk, mla_w_uv, rwkv_mu, rwkv_w0, rwkv_w2, rwkv_a0, rwkv_a2, rwkv_g2, rwkv_k_k, rwkv_k_a, rwkv_r_k, rwkv_ln_w, rwkv_ln_b, ssm_conv_w, ssm_conv_b, ssm_dt_bias, ssm_a_log, ssm_d, ssm_norm, w_branch, w_out, ffn_w1, ffn_w3, ffn_w2, moe_router, moe_router_b, moe_w1, moe_w3, moe_w2):
    W = dict(w_ada=w_ada, b_ada=b_ada, norm_attn=norm_attn, norm_ffn=norm_ffn, w_in=w_in, mla_q_norm=mla_q_norm,
             mla_w_uq=mla_w_uq, mla_kv_norm=mla_kv_norm, mla_w_uk=mla_w_uk, mla_w_uv=mla_w_uv, rwkv_mu=rwkv_mu,
             rwkv_w0=rwkv_w0, rwkv_w2=rwkv_w2, rwkv_a0=rwkv_a0, rwkv_a2=rwkv_a2, rwkv_g2=rwkv_g2, rwkv_k_k=rwkv_k_k,
             rwkv_k_a=rwkv_k_a, rwkv_r_k=rwkv_r_k, rwkv_ln_w=rwkv_ln_w, rwkv_ln_b=rwkv_ln_b, ssm_conv_w=ssm_conv_w,
             ssm_conv_b=ssm_conv_b, ssm_dt_bias=ssm_dt_bias, ssm_a_log=ssm_a_log, ssm_d=ssm_d, ssm_norm=ssm_norm,
             w_branch=w_branch, w_out=w_out, ffn_w1=ffn_w1, ffn_w3=ffn_w3, ffn_w2=ffn_w2, moe_router=moe_router,
             moe_router_b=moe_router_b, moe_w1=moe_w1, moe_w3=moe_w3, moe_w2=moe_w2)
    depth = w_in.shape[0]
    layers = [_pack_layer(l, W) for l in range(depth)]
    bp, sp, _ = x_prompt.shape
    bs, ss, _ = x_sample.shape
    dt = x_prompt.dtype
    y_prompt, (p_ckv, p_kpe, p_shift, p_wkv, p_conv, p_ssm) = _trunk(
        x_prompt, c_prompt, jnp.arange(sp, dtype=jnp.int32), None,
        jnp.zeros((depth, bp, RWKV_IN), dt), jnp.zeros((depth, bp, 8, RHD, RHD), dt),
        jnp.zeros((depth, bp, SCONV - 1, SCD), dt), jnp.zeros((depth, bp, SHEADS, SP, SN), dt), layers, norm_final)
    p_ckv = p_ckv.reshape(depth, bp * sp // PAGE, PAGE, KVL)
    p_kpe = p_kpe.reshape(depth, bp * sp // PAGE, PAGE, ROPE)
    past_len = page_table.shape[1] * PAGE
    pos_s = past_len + jnp.arange(ss, dtype=jnp.int32)
    y_sample, (s_ckv, s_kpe, s_shift, s_wkv, s_conv, s_ssm) = _trunk(
        x_sample, c_sample, pos_s, (cache_ckv, jnp.swapaxes(cache_kpe, 2, 3), page_table),
        state_rwkv_shift, state_rwkv_wkv, state_ssm_conv, state_ssm, layers, norm_final)
    return (y_prompt, y_sample, p_ckv, p_kpe, p_shift, p_wkv, p_conv, p_ssm,
            s_ckv, s_kpe, s_shift, s_wkv, s_conv, s_ssm)
```

```python
import functools

import jax
import jax.numpy as jnp
from jax import lax
from jax.experimental import pallas as pl
from jax.experimental.pallas import tpu as pltpu

F32 = jnp.float32
BF16 = jnp.bfloat16

D_MODEL = 1024
DEPTH = 2
PAGE = 128
H = 8
NOPE = 64
ROPE = 32
VD = 64
QL = 256
KVL = 256
ROPE_THETA = 10000.0
MLA_SCALE = (NOPE + ROPE) ** -0.5
LOG2E = 1.4426950408889634
QK = 384
RDIM = 512
RHD = 64
DECAY_LORA = 64
AAA_LORA = 64
GATE_LORA = 128
RWKV_IN = 3 * RDIM + DECAY_LORA + AAA_LORA + GATE_LORA
RWKV_LN_EPS = 64e-5
SHEADS = 8
SP = 64
SDIM = 512
SGROUPS = 2
SN = 128
SCONV = 4
SCHUNK = 128
SCD = SDIM + 2 * SGROUPS * SN
NBRANCH = 3
NE = 8
TOPK = 2
EPS = 1e-6
SEG_A = 768
DT_TILE = 4
DT_LANE = 32
SEG_WIDTHS = (SEG_A, RWKV_IN, SDIM, SCD, NBRANCH * D_MODEL)
W_IN_COLS = sum(SEG_WIDTHS)
VMEM_LIMIT = 56 * 1024 * 1024


def _cparams(*sem):
    return pltpu.CompilerParams(dimension_semantics=sem, vmem_limit_bytes=VMEM_LIMIT)


def _dot(a, b):
    return jnp.dot(a, b, preferred_element_type=F32)


def _dot_nt(a, b):
    return lax.dot_general(a, b, (((1,), (1,)), ((), ())), preferred_element_type=F32)


def _hi_lo(x):
    hi = x.astype(BF16)
    return hi, (x.astype(F32) - hi.astype(F32)).astype(BF16)


def _dot3(a, w, dot=_dot):
    a_hi, a_lo = _hi_lo(a)
    w_hi, w_lo = _hi_lo(w)
    return dot(a_hi, w_hi) + dot(a_lo, w_hi) + dot(a_hi, w_lo)


def _mm(a, w, precise):
    return _dot3(a, w) if precise else _dot(a.astype(BF16), w)


def _split_dot(x, w01):
    hi = x.astype(BF16)
    lo = (x - hi.astype(F32)).astype(BF16)
    return _dot(hi, w01) + _dot(lo, w01)


def _split3_dot_left(w01, x):
    hi = x.astype(BF16)
    r1 = x - hi.astype(F32)
    mid = r1.astype(BF16)
    lo = (r1 - mid.astype(F32)).astype(BF16)
    return _dot(w01, hi) + _dot(w01, mid) + _dot(w01, lo)


def _sigmoid(x):
    return 1.0 / (1.0 + jnp.exp(-x))


def _silu(x):
    return x * _sigmoid(x)


def _softplus(x):
    return jnp.maximum(x, 0.0) + jnp.log(1.0 + jnp.exp(-jnp.abs(x)))


def _rms(x):
    return x * lax.rsqrt(jnp.mean(x * x, axis=-1, keepdims=True) + EPS)


class _Group:
    def __init__(self, B, S, tm):
        self.B, self.S, self.M = B, S, B * S
        self.precise = S == 1
        self.act_dtype = F32 if self.precise else BF16
        if S == 1:
            self.tm = min(tm, self.M)
            self.grid = (1, self.M // self.tm)
        else:
            self.tm = min(tm, S)
            self.grid = (B, S // self.tm)
        self.ns = self.grid[1]

    def rows(self, width, colblock=0):
        ns = self.ns
        return pl.BlockSpec((self.tm, width), lambda b, s: (b * ns + s, colblock))

    def full(self, shape):
        nd = len(shape)
        return pl.BlockSpec(shape, lambda b, s: (0,) * nd)

    def mod_array(self, m):
        return m.reshape(1, self.M, -1) if self.S == 1 else m.reshape(self.B, 1, -1)

    def mod_spec(self, width):
        if self.S == 1:
            return pl.BlockSpec((1, self.tm, width), lambda b, s: (0, s, 0))
        return pl.BlockSpec((1, 1, width), lambda b, s: (b, 0, 0))

    def pos_spec(self, width):
        if self.S == 1:
            return self.rows(width)
        return pl.BlockSpec((self.tm, width), lambda b, s: (s, 0))

    def tmaj_shape(self, width):
        return (self.M, width) if self.S == 1 else (self.S, self.B * width)

    def tmaj_spec(self, width):
        if self.S == 1:
            return self.rows(width)
        return pl.BlockSpec((self.tm, width), lambda b, s: (s, b))


def _ada_kernel(c_ref, w_ref, b_ref, o_ref):
    c = c_ref[...]
    o_ref[...] = _dot3(_silu(c), w_ref[...]) + b_ref[...]


def ada_matmul(c, w, b):
    m, k = c.shape
    n = w.shape[1]
    tn = 1024
    return pl.pallas_call(
        _ada_kernel,
        grid=(n // tn,),
        in_specs=[pl.BlockSpec((m, k), lambda j: (0, 0)), pl.BlockSpec((k, tn), lambda j: (0, j)),
                  pl.BlockSpec((1, tn), lambda j: (0, j))],
        out_specs=pl.BlockSpec((m, tn), lambda j: (0, j)),
        out_shape=jax.ShapeDtypeStruct((m, n), F32),
        compiler_params=_cparams("arbitrary"),
        name="ada_matmul",
    )(c, w, b.reshape(1, n))


def _win_kernel(x_ref, nw_ref, sc_ref, sh_ref, w_ref, *o_refs):
    h = (_rms(x_ref[...]) * nw_ref[...] * (1.0 + sc_ref[0]) + sh_ref[0]).astype(BF16)
    off = 0
    for o in o_refs:
        n = o.shape[-1]
        for c in range(0, n, 256):
            o[:, c:c + 256] = _dot(h, w_ref[:, off + c:off + c + 256])
        off += n


def _win_cols_kernel(x_ref, nw_ref, sc_ref, sh_ref, w_ref, o_ref):
    h = _rms(x_ref[...]) * nw_ref[...] * (1.0 + sc_ref[0]) + sh_ref[0]
    o_ref[...] = _dot3(h, w_ref[...])


def win_project_precise(g, x, nw, sc, sh, w_packed):
    tn = 512
    out = pl.pallas_call(
        _win_cols_kernel,
        grid=(W_IN_COLS // tn,),
        in_specs=[pl.BlockSpec((g.M, D_MODEL), lambda j: (0, 0)), pl.BlockSpec((1, D_MODEL), lambda j: (0, 0)),
                  pl.BlockSpec((1, g.M, D_MODEL), lambda j: (0, 0, 0)),
                  pl.BlockSpec((1, g.M, D_MODEL), lambda j: (0, 0, 0)),
                  pl.BlockSpec((D_MODEL, tn), lambda j: (0, j))],
        out_specs=pl.BlockSpec((g.M, tn), lambda j: (0, j)),
        out_shape=jax.ShapeDtypeStruct((g.M, W_IN_COLS), F32),
        compiler_params=_cparams("parallel"),
        name="win_project_precise",
    )(x, nw.reshape(1, -1), g.mod_array(sc), g.mod_array(sh), w_packed)
    offs = [0]
    for w in SEG_WIDTHS:
        offs.append(offs[-1] + w)
    return [out[:, offs[i]:offs[i + 1]] for i in range(len(SEG_WIDTHS))]


def win_project(g, x, nw, sc, sh, w_packed):
    if g.precise:
        return win_project_precise(g, x, nw, sc, sh, w_packed)
    return pl.pallas_call(
        _win_kernel,
        grid=g.grid,
        in_specs=[g.rows(D_MODEL), g.full((1, D_MODEL)), g.mod_spec(D_MODEL), g.mod_spec(D_MODEL),
                  g.full((D_MODEL, W_IN_COLS))],
        out_specs=[g.rows(w) for w in SEG_WIDTHS],
        out_shape=[jax.ShapeDtypeStruct((g.M, w), F32) for w in SEG_WIDTHS],
        compiler_params=_cparams("parallel", "parallel"),
        name="win_project",
    )(x, nw.reshape(1, -1), g.mod_array(sc), g.mod_array(sh), w_packed)


def _mla_prep_kernel(a_ref, cos_ref, sin_ref, cos8_ref, sin8_ref, qn_ref, kvn_ref, wuq_ref, wuk_ref,
                     qc_ref, kc_ref, ckv_ref, kpe_ref, *, precise):
    a = a_ref[...]
    tm = a.shape[0]
    odt = qc_ref.dtype
    qn = _rms(a[:, 0:QL]) * qn_ref[...]
    qa = _mm(qn, wuq_ref[...], precise)
    q_rope = qa[:, 512:768] * cos8_ref[...] + qa[:, 768:1024] * sin8_ref[...]
    zpad = jnp.zeros((tm, QK - KVL - ROPE), odt)
    qscale = MLA_SCALE if precise else MLA_SCALE * LOG2E
    for h in range(H):
        q_abs = _mm(qa[:, h * NOPE:(h + 1) * NOPE], wuk_ref[h], precise) * qscale
        qc_ref[h, :, 0:KVL] = q_abs.astype(odt)
        qc_ref[h, :, KVL:KVL + ROPE] = (q_rope[:, h * ROPE:(h + 1) * ROPE] * qscale).astype(odt)
        qc_ref[h, :, KVL + ROPE:QK] = zpad
    ckv = _rms(a[:, QL:QL + KVL]) * kvn_ref[...]
    kpe = a[:, 512:544] * cos_ref[...] + a[:, 552:584] * sin_ref[...]
    ckv_ref[...] = ckv
    kpe_ref[...] = kpe
    kc_ref[:, 0:KVL] = ckv.astype(odt)
    kc_ref[:, KVL:KVL + ROPE] = kpe.astype(odt)
    kc_ref[:, KVL + ROPE:QK] = zpad


def mla_prep(g, seg_a, tabs, qn, kvn, wuq, wuk):
    cos, sin, cos8, sin8 = tabs
    ns = g.ns
    return pl.pallas_call(
        functools.partial(_mla_prep_kernel, precise=g.precise),
        grid=g.grid,
        in_specs=[g.rows(SEG_A), g.pos_spec(ROPE), g.pos_spec(ROPE), g.pos_spec(H * ROPE), g.pos_spec(H * ROPE),
                  g.full((1, QL)), g.full((1, KVL)), g.full((QL, 1024)), g.full((H, NOPE, KVL))],
        out_specs=[pl.BlockSpec((H, g.tm, QK), lambda b, s: (0, b * ns + s, 0)), g.rows(QK), g.rows(KVL),
                   g.rows(ROPE)],
        out_shape=[jax.ShapeDtypeStruct((H, g.M, QK), g.act_dtype), jax.ShapeDtypeStruct((g.M, QK), g.act_dtype),
                   jax.ShapeDtypeStruct((g.M, KVL), F32), jax.ShapeDtypeStruct((g.M, ROPE), F32)],
        compiler_params=_cparams("parallel", "parallel"),
        name="mla_prep",
    )(seg_a, cos, sin, cos8, sin8, qn.reshape(1, -1), kvn.reshape(1, -1), wuq, wuk)


NEG = -1e30


def _attn_kernel(q_ref, k_ref, wuv_ref, o_ref, m_scr, l_scr, a_scr, acc_scr, s_scr, p_scr, *, tq, tk):
    qi = pl.program_id(1)
    q = q_ref[...].reshape(H * tq, QK)
    m_scr[...] = jnp.full(m_scr.shape, NEG, F32)
    l_scr[...] = jnp.zeros(l_scr.shape, F32)
    acc_scr[...] = jnp.zeros(acc_scr.shape, F32)
    reps = tk // 128
    wide = lambda a: jnp.concatenate([a] * reps, axis=-1)

    def keys(j):
        return k_ref[pl.ds(pl.multiple_of(j * tk, tk), tk), :]

    def scores(j, slot):
        s_scr[slot] = _dot_nt(q, keys(j))

    def softmax_pv(j, slot, masked):
        if masked:
            visible = (j * tk + lax.broadcasted_iota(jnp.int32, (tq, tk), 1)
                       <= qi * tq + lax.broadcasted_iota(jnp.int32, (tq, tk), 0))
        for h in range(H):
            rs = pl.ds(h * tq, tq)
            s = s_scr[slot, rs, :]
            if masked:
                s = jnp.where(visible, s, NEG)
            m_prev = m_scr[rs, :]
            m_new = jnp.maximum(m_prev, jnp.max(s, axis=-1, keepdims=True))
            alpha = jnp.exp2(m_prev - m_new)
            p = jnp.exp2(s - wide(m_new))
            l_scr[rs, :] = alpha * l_scr[rs, :] + jnp.sum(p, axis=-1, keepdims=True)
            m_scr[rs, :] = m_new
            a_scr[rs, :] = alpha
            p_scr[slot, rs, :] = p.astype(BF16)
        alpha = a_scr[...]
        acc_scr[...] = (acc_scr[...] * jnp.concatenate([alpha] * (KVL // 128), axis=-1)
                        + _dot(p_scr[slot], keys(j)[:, 0:KVL]))

    n_full = (qi * tq) // tk

    scores(0, 0)

    def body(i, carry):
        j = 2 * i
        scores(j + 1, 1)
        softmax_pv(j, 0, False)
        scores(j + 2, 0)
        softmax_pv(j + 1, 1, False)
        return carry

    lax.fori_loop(0, n_full // 2, body, 0)

    @pl.when(n_full % 2 == 0)
    def _():
        softmax_pv(n_full, 0, True)

    @pl.when(n_full % 2 == 1)
    def _():
        scores(n_full, 1)
        softmax_pv(n_full - 1, 0, False)
        softmax_pv(n_full, 1, True)
    inv_l = 1.0 / l_scr[...]
    o = acc_scr[...] * jnp.concatenate([inv_l] * (KVL // 128), axis=-1)
    for h in range(H):
        o_ref[:, h * VD:(h + 1) * VD] = _dot(o[h * tq:(h + 1) * tq].astype(BF16), wuv_ref[h]).astype(o_ref.dtype)


def mla_attention(B, S, qc, kc, wuv):
    tq = min(128, S)
    tk = min(256, S)
    nq = S // tq
    rows = H * tq
    return pl.pallas_call(
        functools.partial(_attn_kernel, tq=tq, tk=tk),
        grid=(B, nq),
        in_specs=[pl.BlockSpec((H, tq, QK), lambda b, i: (0, b * nq + i, 0)),
                  pl.BlockSpec((S, QK), lambda b, i: (b, 0)),
                  pl.BlockSpec((H, KVL, VD), lambda b, i: (0, 0, 0))],
        out_specs=pl.BlockSpec((tq, H * VD), lambda b, i: (b * nq + i, 0)),
        out_shape=jax.ShapeDtypeStruct((B * S, H * VD), BF16),
        scratch_shapes=[pltpu.VMEM((rows, 128), F32), pltpu.VMEM((rows, 128), F32), pltpu.VMEM((rows, 128), F32),
                        pltpu.VMEM((rows, KVL), F32), pltpu.VMEM((2, rows, tk), F32),
                        pltpu.VMEM((2, rows, tk), BF16)],
        compiler_params=_cparams("parallel", "parallel"),
        name="mla_attention",
    )(qc, kc, wuv)


PAGES_PER_STEP = 16


def _paged_kernel(pt_ref, q_ref, knew_ref, wuv_ref, *rest):
    pp = PAGES_PER_STEP
    ckv_refs, kpe_refs = rest[:pp], rest[pp:2 * pp]
    o_ref, m_scr, l_scr, acc_scr = rest[2 * pp:]
    j = pl.program_id(1)
    q = q_ref[...]

    @pl.when(j == 0)
    def _():
        m_scr[...] = jnp.full(m_scr.shape, NEG, F32)
        l_scr[...] = jnp.zeros(l_scr.shape, F32)
        acc_scr[...] = jnp.zeros(acc_scr.shape, F32)

    def stack_hi_lo(x):
        hi = x.astype(BF16).astype(F32)
        return jnp.concatenate([hi, x - hi], axis=0).astype(BF16)

    qa2 = stack_hi_lo(q[:, 0:KVL])
    qp2 = stack_hi_lo(q[:, KVL:KVL + ROPE])

    def chain(refs_c, refs_p):
        ckv_hi, ckv_lo = _hi_lo(jnp.concatenate([r[...] for r in refs_c], axis=0))
        kpe_hi, kpe_lo = _hi_lo(jnp.concatenate([r[...] for r in refs_p], axis=1))
        s2 = _dot_nt(qa2, ckv_hi) + _dot(qp2, kpe_hi)
        s = s2[0:H] + s2[H:2 * H] + _dot_nt(qa2[0:H], ckv_lo) + _dot(qp2[0:H], kpe_lo)
        m = jnp.max(s, axis=-1, keepdims=True)
        p = jnp.exp(s - m)
        p2 = stack_hi_lo(p)
        pv2 = _dot(p2, ckv_hi)
        return m, jnp.sum(p, axis=-1, keepdims=True), pv2[0:H] + pv2[H:2 * H] + _dot(p2[0:H], ckv_lo)

    half = pp // 2
    parts = [chain(ckv_refs[i * half:(i + 1) * half], kpe_refs[i * half:(i + 1) * half]) for i in range(2)]
    m_prev = m_scr[...]
    m_new = jnp.maximum(m_prev, jnp.maximum(parts[0][0], parts[1][0]))
    alpha = jnp.exp(m_prev - m_new)
    l = alpha * l_scr[...]
    acc = alpha * acc_scr[...]
    for m_i, l_i, pv_i in parts:
        w_i = jnp.exp(m_i - m_new)
        l = l + w_i * l_i
        acc = acc + w_i * pv_i
    l_scr[...] = l
    acc_scr[...] = acc
    m_scr[...] = m_new

    @pl.when(j == pl.num_programs(1) - 1)
    def _():
        kn = knew_ref[0]
        s_new = jnp.sum(q * kn, axis=-1, keepdims=True)
        m_prev = m_scr[...]
        m_new = jnp.maximum(m_prev, s_new)
        alpha = jnp.exp(m_prev - m_new)
        p_new = jnp.exp(s_new - m_new)
        l = alpha * l_scr[...] + p_new
        acc = alpha * acc_scr[...] + p_new * kn[:, 0:KVL]
        o = acc / l
        for h in range(H):
            o_ref[0, :, h * VD:(h + 1) * VD] = _dot3(o[h:h + 1], wuv_ref[h])


def paged_attention(layer, page_table, qc, kc_new, cache_ckv, cache_kpe_t, wuv):
    B, n_pages = page_table.shape
    pp = PAGES_PER_STEP
    nsteps = n_pages // pp

    def page_spec(i, shape):
        return pl.BlockSpec((None, None) + shape, lambda b, j, pt: (layer, pt[b, j * pp + i], 0, 0))

    grid_spec = pltpu.PrefetchScalarGridSpec(
        num_scalar_prefetch=1,
        grid=(B, nsteps),
        in_specs=[pl.BlockSpec((None, H, QK), lambda b, j, pt: (b, 0, 0)),
                  pl.BlockSpec((1, 1, QK), lambda b, j, pt: (b, 0, 0)),
                  pl.BlockSpec((H, KVL, VD), lambda b, j, pt: (0, 0, 0))]
        + [page_spec(i, (PAGE, KVL)) for i in range(pp)] + [page_spec(i, (ROPE, PAGE)) for i in range(pp)],
        out_specs=pl.BlockSpec((1, 1, H * VD), lambda b, j, pt: (b, 0, 0)),
        scratch_shapes=[pltpu.VMEM((H, 1), F32), pltpu.VMEM((H, 1), F32), pltpu.VMEM((H, KVL), F32)],
    )
    out = pl.pallas_call(
        _paged_kernel,
        grid_spec=grid_spec,
        out_shape=jax.ShapeDtypeStruct((B, 1, H * VD), F32),
        compiler_params=_cparams("parallel", "arbitrary"),
        name="paged_attention",
    )(page_table, jnp.transpose(qc, (1, 0, 2)), kc_new.reshape(B, 1, QK), wuv, *([cache_ckv] * pp),
      *([cache_kpe_t] * pp))
    return out.reshape(B, H * VD)


def _rwkv_prep_kernel(rw_ref, prev_ref, mu_ref, w0_ref, a0_ref, kk_ref_, ka_ref, w2_ref, a2_ref, g2_ref, ones_ref,
                      r_o, w_o, k_o, v_o, kk_o, b_o, g_o, *rest, precise, seq):
    rw = rw_ref[...]
    if seq:
        vt_o, carry = rest
        tm = rw.shape[0]

        @pl.when(pl.program_id(1) == 0)
        def _():
            carry[...] = prev_ref[0]

        row8 = lax.broadcasted_iota(jnp.int32, (8, RWKV_IN), 0)
        shifted = pltpu.roll(rw, 1, 0)
        top = jnp.where(row8 == 0, pltpu.roll(carry[...], 1, 0), shifted[0:8])
        prev = jnp.concatenate([top, shifted[8:]], axis=0)
        carry[...] = rw[tm - 8:tm]
    else:
        prev = prev_ref[...]
    xs = rw + (prev - rw) * mu_ref[...]
    r = xs[:, 0:RDIM]
    k = xs[:, RDIM:2 * RDIM]
    v = xs[:, 2 * RDIM:3 * RDIM]
    o = 3 * RDIM
    wl = xs[:, o:o + DECAY_LORA]
    al = xs[:, o + DECAY_LORA:o + DECAY_LORA + AAA_LORA]
    gl = xs[:, o + DECAY_LORA + AAA_LORA:RWKV_IN]
    w = -_softplus(-(w0_ref[...] + _mm(jnp.tanh(wl), w2_ref[...], precise))) - 0.5
    a = _sigmoid(a0_ref[...] + _mm(al, a2_ref[...], precise))
    kk = k * kk_ref_[...]
    kk = kk * lax.rsqrt(jnp.maximum(_split_dot(kk * kk, ones_ref[...]), 1e-24))
    r_o[...] = r
    w_o[...] = jnp.exp(-jnp.exp(w))
    k_o[...] = k * (1.0 + (a - 1.0) * ka_ref[...])
    v_o[...] = v
    kk_o[...] = kk
    b_o[...] = kk * a
    g_o[...] = _mm(_sigmoid(gl), g2_ref[...], precise)
    if seq:
        ri = lax.broadcasted_iota(jnp.int32, (SCAN_W, SCAN_W), 0)
        ci = lax.broadcasted_iota(jnp.int32, (SCAN_W, SCAN_W), 1)
        same_head = (ri // RHD) == (ci // RHD)
        eye_rep = (lax.broadcasted_iota(jnp.int32, (RHD, SCAN_W), 0)
                   == lax.broadcasted_iota(jnp.int32, (RHD, SCAN_W), 1) % RHD).astype(BF16)
        for c in range(tm // RHD):
            for t in range(SCAN_TPB):
                v_c = v[c * RHD:(c + 1) * RHD, t * SCAN_W:(t + 1) * SCAN_W].astype(BF16)
                blockdiag = jnp.where(same_head, jnp.concatenate([v_c] * SCAN_HPT, axis=0), jnp.zeros((), BF16))
                vt_o[c, t] = _dot_nt(eye_rep, blockdiag).astype(vt_o.dtype)


def rwkv_prep(g, rw, prev, p, mw, ones_bd):
    vec = lambda a: a.reshape(1, -1)
    tshape = jax.ShapeDtypeStruct(g.tmaj_shape(RDIM), F32)
    seq = g.S > 1
    out_specs = [g.tmaj_spec(RDIM)] * 6 + [g.rows(RDIM)]
    out_shape = [tshape] * 6 + [jax.ShapeDtypeStruct((g.M, RDIM), F32)]
    scratch = []
    if seq:
        assert g.tm % RHD == 0
        cpt = g.tm // RHD
        prev = jnp.pad(prev[:, None, :], ((0, 0), (7, 0), (0, 0)))
        prev_spec = pl.BlockSpec((1, 8, RWKV_IN), lambda b, s: (b, 0, 0))
        out_specs.append(pl.BlockSpec((cpt, SCAN_TPB, RHD, SCAN_W), lambda b, s: (s, b, 0, 0)))
        out_shape.append(jax.ShapeDtypeStruct((g.S // RHD, g.B * SCAN_TPB, RHD, SCAN_W), BF16))
        scratch = [pltpu.VMEM((8, RWKV_IN), F32)]
    else:
        prev_spec = g.rows(RWKV_IN)
    return pl.pallas_call(
        functools.partial(_rwkv_prep_kernel, precise=g.precise, seq=seq),
        grid=g.grid,
        in_specs=[g.rows(RWKV_IN), prev_spec, g.full((1, RWKV_IN)), g.full((1, RDIM)), g.full((1, RDIM)),
                  g.full((1, RDIM)), g.full((1, RDIM)), g.full((DECAY_LORA, RDIM)), g.full((AAA_LORA, RDIM)),
                  g.full((GATE_LORA, RDIM)), g.full((RDIM, RDIM))],
        out_specs=out_specs,
        out_shape=out_shape,
        scratch_shapes=scratch,
        compiler_params=_cparams("parallel", "arbitrary"),
        name="rwkv_prep",
    )(rw, prev, vec(p["mu"]), vec(p["w0"]), vec(p["a0"]), vec(p["k_k"]), vec(p["k_a"]), mw["w2"], mw["a2"], mw["g2"],
      ones_bd)


SCAN_NB = 16
SCAN_HPT = 4
SCAN_W = SCAN_HPT * RHD
SCAN_TPB = RDIM // SCAN_W
SCAN_GB = 8
SCAN_YSUB = 128 // SCAN_HPT


def _scan_kernel(r_ref, w_ref, k_ref, kk_ref, b_ref, vt_ref, s0_ref, yt_ref, sout_ref, s_scr, *, tc_len, ng, precise):
    tc = pl.program_id(1)

    def pick(x, w01):
        return _split_dot(x, w01) if precise else _dot(x.astype(BF16), w01)

    @pl.when(tc == 0)
    def _():
        s_scr[...] = s0_ref[...]

    ri = lax.broadcasted_iota(jnp.int32, (SCAN_W, SCAN_W), 0)
    ci = lax.broadcasted_iota(jnp.int32, (SCAN_W, SCAN_W), 1)
    same_head = (ri // RHD) == (ci // RHD)
    ones_bd = same_head.astype(BF16)
    yri = lax.broadcasted_iota(jnp.int32, (SCAN_W, 128), 0)
    yci = lax.broadcasted_iota(jnp.int32, (SCAN_W, 128), 1)
    yt_ref[...] = jnp.zeros(yt_ref.shape, F32)
    gb = min(SCAN_GB, ng)
    sub = min(SCAN_YSUB, tc_len)

    def step(tt, carry, part):
        tg = part * sub + tt
        e_t = (same_head & ((ri % RHD) == tg)).astype(BF16)
        y_t = ((yri // RHD) * SCAN_YSUB + tt == yci).astype(BF16)
        ylanes = pl.ds(part * 128, 128)

        def rows(ref, g0):
            return jnp.stack([jnp.broadcast_to(ref[pl.ds(tg, 1), pl.ds((g0 + i) * SCAN_W, SCAN_W)], (RHD, SCAN_W))
                              for i in range(gb)])

        def issue(g0):
            sa = pick((s_scr[g0:g0 + gb] * rows(kk_ref, g0)).reshape(gb * RHD, SCAN_W), ones_bd)
            vcol = pick(vt_ref[0, g0:g0 + gb].reshape(gb * RHD, SCAN_W), e_t)
            return sa.reshape(gb, RHD, SCAN_W), vcol.reshape(gb, RHD, SCAN_W)

        pend = issue(0)
        y_pend = None
        for g0 in range(0, ng, gb):
            nxt = issue(g0 + gb) if g0 + gb < ng else None
            sa, vcol = pend
            s = s_scr[g0:g0 + gb] * rows(w_ref, g0) - sa * rows(b_ref, g0) + vcol * rows(k_ref, g0)
            s_scr[g0:g0 + gb] = s
            y = pick((s * rows(r_ref, g0)).reshape(gb * RHD, SCAN_W), y_t).reshape(gb, RHD, 128)
            if y_pend is not None:
                gp, yp = y_pend
                yt_ref[0, gp:gp + gb, :, ylanes] += yp
            y_pend = (g0, y)
            pend = nxt
        gp, yp = y_pend
        yt_ref[0, gp:gp + gb, :, ylanes] += yp
        return carry

    for part in range(tc_len // sub):
        lax.fori_loop(0, sub, functools.partial(step, part=part), 0, unroll=2 if sub % 2 == 0 else 1)

    @pl.when(tc == pl.num_programs(1) - 1)
    def _():
        sout_ref[...] = s_scr[...]


def rwkv_scan(B, S, r, w, k, kk, b, vt, s0, precise):
    tc_len = min(RHD, S)
    nb = min(SCAN_NB, B)
    ng = nb * SCAN_TPB
    nchunks = S // tc_len
    row_spec = pl.BlockSpec((tc_len, nb * RDIM), lambda bg, c: (c, bg))
    st_spec = pl.BlockSpec((ng, RHD, SCAN_W), lambda bg, c: (bg, 0, 0))
    ch_spec = pl.BlockSpec((1, ng, RHD, SCAN_W), lambda bg, c: (c, bg, 0, 0))
    return pl.pallas_call(
        functools.partial(_scan_kernel, tc_len=tc_len, ng=ng, precise=precise),
        grid=(B // nb, nchunks),
        in_specs=[row_spec] * 5 + [ch_spec, st_spec],
        out_specs=[ch_spec, st_spec],
        out_shape=[jax.ShapeDtypeStruct((nchunks, B * SCAN_TPB, RHD, SCAN_W), F32),
                   jax.ShapeDtypeStruct((B * SCAN_TPB, RHD, SCAN_W), F32)],
        scratch_shapes=[pltpu.VMEM((ng, RHD, SCAN_W), F32)],
        compiler_params=_cparams("parallel", "arbitrary"),
        name="rwkv_scan",
    )(r, w, k, kk, b, vt, s0)


def _rwkv_post_kernel(y_ref, r_ref, k_ref, v_ref, g_ref, lnw_ref, lnb_ref, rk_ref, ones_ref, o_ref):
    ones = ones_ref[...]
    y = y_ref[...]
    mu = _split_dot(y, ones) * (1.0 / RHD)
    yc = y - mu
    var = _split_dot(yc * yc, ones) * (1.0 / RHD)
    yn = yc * lax.rsqrt(var + RWKV_LN_EPS) * lnw_ref[...] + lnb_ref[...]
    v = v_ref[...]
    bonus = _split_dot(r_ref[...] * k_ref[...] * rk_ref[...], ones)
    o_ref[...] = ((yn + bonus * v) * g_ref[...]).astype(o_ref.dtype)


def rwkv_post(g, y, r, k, v, gate, p, ones_bd):
    vec = lambda a: a.reshape(1, -1)
    return pl.pallas_call(
        _rwkv_post_kernel,
        grid=g.grid,
        in_specs=[g.rows(RDIM), g.tmaj_spec(RDIM), g.tmaj_spec(RDIM), g.tmaj_spec(RDIM), g.rows(RDIM),
                  g.full((1, RDIM)), g.full((1, RDIM)), g.full((1, RDIM)), g.full((RDIM, RDIM))],
        out_specs=g.rows(RDIM),
        out_shape=jax.ShapeDtypeStruct((g.M, RDIM), g.act_dtype),
        compiler_params=_cparams("parallel", "parallel"),
        name="rwkv_post",
    )(y, r, k, v, gate, vec(p["ln_w"]), vec(p["ln_b"]), vec(p["r_k"]), ones_bd)


def _ssm_gate_norm(y, z, nw):
    y = y * _silu(z)
    gw = SDIM // SGROUPS
    parts = [_rms(y[:, i * gw:(i + 1) * gw]) for i in range(SGROUPS)]
    return jnp.concatenate(parts, axis=-1) * nw


def _ssd_kernel(xbc_ref, z_ref, dt_ref, tail0_ref, cw_ref, cb_ref, dtb_ref, a_ref, dsk_ref, nw_ref,
                o_ref, hout_ref, tail_scr, h_scr, y_scr):
    c = pl.program_id(1)
    L = SCHUNK

    @pl.when(c == 0)
    def _():
        tail_scr[...] = tail0_ref[0]
        h_scr[...] = jnp.zeros(h_scr.shape, F32)

    xbc = xbc_ref[...]
    tail = tail_scr[...]
    row8 = lax.broadcasted_iota(jnp.int32, (8, SCD), 0)
    conv = cb_ref[...] + xbc * cw_ref[SCONV - 1:SCONV, :]
    for sft in range(1, SCONV):
        sh = pltpu.roll(xbc, sft, 0)
        top = jnp.where(row8 < sft, pltpu.roll(tail, sft, 0), sh[0:8])
        sh = jnp.concatenate([top, sh[8:]], axis=0)
        conv = conv + sh * cw_ref[SCONV - 1 - sft:SCONV - sft, :]
    tail_scr[...] = xbc[L - 8:L]
    act = _silu(conv)
    xa = act[:, 0:SDIM]
    lane = lax.broadcasted_iota(jnp.int32, (L, 128), 1)
    dt_valid = (lane >= DT_LANE) & (lane < DT_LANE + SHEADS)
    dtt = jnp.where(dt_valid, _softplus(dt_ref[...] + dtb_ref[...]), 0.0)
    adt = dtt * a_ref[...]
    ri = lax.broadcasted_iota(jnp.int32, (L, L), 0)
    ci = lax.broadcasted_iota(jnp.int32, (L, L), 1)
    causal = ri >= ci
    cs = _split3_dot_left(causal.astype(BF16), adt)
    cs_t = cs.T
    total = cs[L - 1:L, :]
    for grp in range(SGROUPS):
        bm = act[:, SDIM + grp * SN:SDIM + (grp + 1) * SN]
        cm = act[:, SDIM + SGROUPS * SN + grp * SN:SDIM + SGROUPS * SN + (grp + 1) * SN]
        cb = _dot_nt(cm.astype(BF16), bm.astype(BF16))
        for hh in range(SHEADS // SGROUPS):
            h = grp * (SHEADS // SGROUPS) + hh
            ln = DT_LANE + h
            col = cs[:, ln:ln + 1]
            row = cs_t[ln:ln + 1, :]
            tot = total[:, ln:ln + 1]
            lmat = jnp.where(causal, jnp.exp(jnp.where(causal, col - row, 0.0)), 0.0)
            xh = xa[:, h * SP:(h + 1) * SP]
            xdt = (xh * dtt[:, ln:ln + 1]).astype(BF16)
            hprev = h_scr[h]
            y = _dot((cb * lmat).astype(BF16), xdt)
            y = y + _dot((cm * jnp.exp(col)).astype(BF16), hprev.astype(BF16))
            y_scr[:, h * SP:(h + 1) * SP] = y + dsk_ref[:, h * SP:(h + 1) * SP] * xh
            bdec = (bm * jnp.exp(tot - col)).T.astype(BF16)
            h_scr[h] = hprev * jnp.exp(tot) + _dot(bdec, xdt)
    o_ref[...] = _ssm_gate_norm(y_scr[...], z_ref[...], nw_ref[...]).astype(o_ref.dtype)

    @pl.when(c == pl.num_programs(1) - 1)
    def _():
        hout_ref[0] = h_scr[...]


def ssd_prompt(B, S, xbc, z, seg_a, tail0, p):
    nc = S // SCHUNK
    L = SCHUNK
    full = lambda shape: pl.BlockSpec(shape, lambda b, c: (0,) * len(shape))
    return pl.pallas_call(
        _ssd_kernel,
        grid=(B, nc),
        in_specs=[pl.BlockSpec((L, SCD), lambda b, c: (b * nc + c, 0)),
                  pl.BlockSpec((L, SDIM), lambda b, c: (b * nc + c, 0)),
                  pl.BlockSpec((L, 128), lambda b, c: (b * nc + c, DT_TILE)),
                  pl.BlockSpec((1, 8, SCD), lambda b, c: (b, 0, 0)),
                  full((SCONV, SCD)), full((1, SCD)), full((1, 128)), full((1, 128)), full((1, SDIM)),
                  full((1, SDIM))],
        out_specs=[pl.BlockSpec((L, SDIM), lambda b, c: (b * nc + c, 0)),
                   pl.BlockSpec((1, SHEADS, SN, SP), lambda b, c: (b, 0, 0, 0))],
        out_shape=[jax.ShapeDtypeStruct((B * S, SDIM), BF16), jax.ShapeDtypeStruct((B, SHEADS, SN, SP), F32)],
        scratch_shapes=[pltpu.VMEM((8, SCD), F32), pltpu.VMEM((SHEADS, SN, SP), F32), pltpu.VMEM((L, SDIM), F32)],
        compiler_params=_cparams("parallel", "arbitrary"),
        name="ssd_prompt",
    )(xbc, z, seg_a, tail0, p["conv_w"], p["conv_b"].reshape(1, -1), p["dtb128"], p["a128"], p["dskip512"],
      p["ssm_norm"].reshape(1, -1))


def _ssm_step_pre_kernel(xbc_ref, c0_ref, c1_ref, c2_ref, dt_ref, cw_ref, cb_ref, dtb_ref, a_ref,
                         act_ref, xdt_ref, dec_ref):
    conv = (cb_ref[...] + c0_ref[...] * cw_ref[0:1, :] + c1_ref[...] * cw_ref[1:2, :] + c2_ref[...] * cw_ref[2:3, :]
            + xbc_ref[...] * cw_ref[3:4, :])
    act = _silu(conv)
    act_ref[...] = act
    dtt = _softplus(dt_ref[...] + dtb_ref[...])
    dec_ref[...] = jnp.exp(dtt * a_ref[...])
    for h in range(SHEADS):
        xdt_ref[:, h * SP:(h + 1) * SP] = act[:, h * SP:(h + 1) * SP] * dtt[:, DT_LANE + h:DT_LANE + h + 1]


def ssm_step_pre(g, xbc, c0, c1, c2, seg_a, p):
    return pl.pallas_call(
        _ssm_step_pre_kernel,
        grid=g.grid,
        in_specs=[g.rows(SCD)] * 4 + [g.rows(128, DT_TILE), g.full((SCONV, SCD)), g.full((1, SCD)),
                                      g.full((1, 128)), g.full((1, 128))],
        out_specs=[g.rows(SCD), g.rows(SDIM), g.rows(128)],
        out_shape=[jax.ShapeDtypeStruct((g.M, SCD), F32), jax.ShapeDtypeStruct((g.M, SDIM), F32),
                   jax.ShapeDtypeStruct((g.M, 128), F32)],
        compiler_params=_cparams("parallel", "parallel"),
        name="ssm_step_pre",
    )(xbc, c0, c1, c2, seg_a, p["conv_w"], p["conv_b"].reshape(1, -1), p["dtb128"], p["a128"])


SSM_STEP_BT = 8


def _ssm_step_kernel(h0_ref, xb_ref, dec_ref, bm_ref, cm_ref, h1_ref, y_ref):
    bt = SSM_STEP_BT
    rp = lax.broadcasted_iota(jnp.int32, (SP, 128), 0)
    lp = lax.broadcasted_iota(jnp.int32, (SP, 128), 1)
    pick = [lp == rp, lp == rp + SP]
    hpg = SHEADS // SGROUPS
    for i in range(bt):
        for hp in range(SHEADS // 2):
            yrow = jnp.zeros((1, 128), F32)
            for e in range(2):
                h = hp * 2 + e
                grp = h // hpg
                bm = bm_ref[i:i + 1, grp * SN:(grp + 1) * SN]
                cm = cm_ref[i:i + 1, grp * SN:(grp + 1) * SN]
                h1 = h0_ref[i, h] * dec_ref[i, h] + xb_ref[i, h] * bm
                h1_ref[i, h] = h1
                ycol = jnp.sum(h1 * cm, axis=-1, keepdims=True)
                yrow = yrow + jnp.sum(jnp.where(pick[e], ycol, 0.0), axis=0, keepdims=True)
            y_ref[i:i + 1, hp * 128:(hp + 1) * 128] = yrow


def ssm_step(B, layer, h0, xb, dec, bm, cm):
    bt = SSM_STEP_BT
    st = pl.BlockSpec((bt, SHEADS, SP, SN), lambda i: (i, 0, 0, 0))
    return pl.pallas_call(
        _ssm_step_kernel,
        grid=(B // bt,),
        in_specs=[pl.BlockSpec((None, bt, SHEADS, SP, SN), lambda i: (layer, i, 0, 0, 0)), st,
                  pl.BlockSpec((bt, SHEADS, 1, SN), lambda i: (i, 0, 0, 0)),
                  pl.BlockSpec((bt, SGROUPS * SN), lambda i: (i, 0)), pl.BlockSpec((bt, SGROUPS * SN), lambda i: (i, 0))],
        out_specs=[st, pl.BlockSpec((bt, SDIM), lambda i: (i, 0))],
        out_shape=[jax.ShapeDtypeStruct((B, SHEADS, SP, SN), F32), jax.ShapeDtypeStruct((B, SDIM), F32)],
        compiler_params=_cparams("parallel"),
        name="ssm_step",
    )(h0, xb, dec, bm, cm)


def _ssm_step_post_kernel(y_ref, x_ref, z_ref, dsk_ref, nw_ref, o_ref):
    y = y_ref[...] + dsk_ref[...] * x_ref[...]
    o_ref[...] = _ssm_gate_norm(y, z_ref[...], nw_ref[...]).astype(o_ref.dtype)


def ssm_step_post(g, y, act, z, p):
    return pl.pallas_call(
        _ssm_step_post_kernel,
        grid=g.grid,
        in_specs=[g.rows(SDIM), g.rows(SDIM), g.rows(SDIM), g.full((1, SDIM)), g.full((1, SDIM))],
        out_specs=g.rows(SDIM),
        out_shape=jax.ShapeDtypeStruct((g.M, SDIM), g.act_dtype),
        compiler_params=_cparams("parallel", "parallel"),
        name="ssm_step_post",
    )(y, act, z, p["dskip512"], p["ssm_norm"].reshape(1, -1))


def _merge_kernel(a_ref, r_ref, s_ref, gate_ref, x_ref, ga_ref, wb_ref, wo_ref, o_ref, *, precise):
    acc = None
    for i, br in enumerate((a_ref, r_ref, s_ref)):
        t = _sigmoid(gate_ref[:, i * D_MODEL:(i + 1) * D_MODEL]) * _mm(br[...], wb_ref[i], precise)
        acc = t if acc is None else acc + t
    mix = _mm(acc, wo_ref[...], precise)
    o_ref[...] = x_ref[...] + ga_ref[0] * mix


def merge_branches(g, a_out, r_out, s_out, gate, x, ga, wb, wo):
    return pl.pallas_call(
        functools.partial(_merge_kernel, precise=g.precise),
        grid=g.grid,
        in_specs=[g.rows(512), g.rows(512), g.rows(512), g.rows(NBRANCH * D_MODEL), g.rows(D_MODEL),
                  g.mod_spec(D_MODEL), g.full((NBRANCH, 512, D_MODEL)), g.full((D_MODEL, D_MODEL))],
        out_specs=g.rows(D_MODEL),
        out_shape=jax.ShapeDtypeStruct((g.M, D_MODEL), F32),
        compiler_params=_cparams("parallel", "parallel"),
        name="merge_branches",
    )(a_out, r_out, s_out, gate, x, g.mod_array(ga), wb, wo)


def _ffn_kernel(x_ref, nw_ref, sc_ref, sh_ref, gf_ref, w1_ref, w3_ref, w2_ref, o_ref, h_scr, acc_scr, *, precise):
    j = pl.program_id(2)

    @pl.when(j == 0)
    def _():
        h_scr[...] = (_rms(x_ref[...]) * nw_ref[...] * (1.0 + sc_ref[0]) + sh_ref[0]).astype(h_scr.dtype)
        acc_scr[...] = jnp.zeros(acc_scr.shape, F32)

    h = h_scr[...]
    u = _silu(_mm(h, w1_ref[...], precise)) * _mm(h, w3_ref[...], precise)
    acc_scr[...] += _mm(u, w2_ref[...], precise)

    @pl.when(j == pl.num_programs(2) - 1)
    def _():
        o_ref[...] = x_ref[...] + gf_ref[0] * acc_scr[...]


def dense_ffn(g, x, nw, sc, sh, gf, w1, w3, w2, tf):
    dff = w1.shape[1]
    lift = lambda spec: pl.BlockSpec(spec.block_shape, lambda b, s, j, f=spec.index_map: f(b, s))
    return pl.pallas_call(
        functools.partial(_ffn_kernel, precise=g.precise),
        grid=g.grid + (dff // tf,),
        in_specs=[lift(g.rows(D_MODEL)), lift(g.full((1, D_MODEL))), lift(g.mod_spec(D_MODEL)),
                  lift(g.mod_spec(D_MODEL)), lift(g.mod_spec(D_MODEL)),
                  pl.BlockSpec((D_MODEL, tf), lambda b, s, j: (0, j)), pl.BlockSpec((D_MODEL, tf), lambda b, s, j: (0, j)),
                  pl.BlockSpec((tf, D_MODEL), lambda b, s, j: (j, 0))],
        out_specs=lift(g.rows(D_MODEL)),
        out_shape=jax.ShapeDtypeStruct((g.M, D_MODEL), F32),
        scratch_shapes=[pltpu.VMEM((g.tm, D_MODEL), g.act_dtype), pltpu.VMEM((g.tm, D_MODEL), F32)],
        compiler_params=_cparams("parallel", "parallel", "arbitrary"),
        name="dense_ffn",
    )(x, nw.reshape(1, -1), g.mod_array(sc), g.mod_array(sh), g.mod_array(gf), w1, w3, w2)


def _router_kernel(x_ref, nw_ref, sc_ref, sh_ref, wr_ref, rb_ref, h_ref, logit_ref):
    h = _rms(x_ref[...]) * nw_ref[...] * (1.0 + sc_ref[0]) + sh_ref[0]
    h_ref[...] = h.astype(BF16)
    logit_ref[...] = _dot3(h, wr_ref[...]) + rb_ref[...]


def moe_router(g, x, nw, sc, sh, wr, rb):
    return pl.pallas_call(
        _router_kernel,
        grid=g.grid,
        in_specs=[g.rows(D_MODEL), g.full((1, D_MODEL)), g.mod_spec(D_MODEL), g.mod_spec(D_MODEL),
                  g.full((D_MODEL, 128)), g.full((1, 128))],
        out_specs=[g.rows(D_MODEL), g.rows(128)],
        out_shape=[jax.ShapeDtypeStruct((g.M, D_MODEL), BF16), jax.ShapeDtypeStruct((g.M, 128), F32)],
        compiler_params=_cparams("parallel", "parallel"),
        name="moe_router",
    )(x, nw.reshape(1, -1), g.mod_array(sc), g.mod_array(sh), wr, rb)


def _expert_kernel(te_ref, tv_ref, h_ref, w1_ref, w3_ref, w2_ref, o_ref, acc_scr):
    i = pl.program_id(0)
    j = pl.program_id(1)

    @pl.when(j == 0)
    def _():
        acc_scr[...] = jnp.zeros(acc_scr.shape, F32)

    @pl.when(tv_ref[i] > 0)
    def _():
        h = h_ref[...]
        u = (_silu(_dot(h, w1_ref[...])) * _dot(h, w3_ref[...])).astype(BF16)
        acc_scr[...] += _dot(u, w2_ref[...])

    @pl.when(j == pl.num_programs(1) - 1)
    def _():
        o_ref[...] = acc_scr[...]


def expert_ffn(tile_expert, tile_valid, h_sorted, w1, w3, w2, tm, tf):
    rows = h_sorted.shape[0]
    dffe = w1.shape[2]
    grid_spec = pltpu.PrefetchScalarGridSpec(
        num_scalar_prefetch=2,
        grid=(rows // tm, dffe // tf),
        in_specs=[pl.BlockSpec((tm, D_MODEL), lambda i, j, te, tv: (i, 0)),
                  pl.BlockSpec((None, D_MODEL, tf), lambda i, j, te, tv: (te[i], 0, j)),
                  pl.BlockSpec((None, D_MODEL, tf), lambda i, j, te, tv: (te[i], 0, j)),
                  pl.BlockSpec((None, tf, D_MODEL), lambda i, j, te, tv: (te[i], j, 0))],
        out_specs=pl.BlockSpec((tm, D_MODEL), lambda i, j, te, tv: (i, 0)),
        scratch_shapes=[pltpu.VMEM((tm, D_MODEL), F32)],
    )
    return pl.pallas_call(
        _expert_kernel,
        grid_spec=grid_spec,
        out_shape=jax.ShapeDtypeStruct((rows, D_MODEL), F32),
        compiler_params=_cparams("parallel", "arbitrary"),
        name="expert_ffn",
    )(tile_expert, tile_valid, h_sorted, w1, w3, w2)


def _combine_kernel(x_ref, gf_ref, y0_ref, y1_ref, wt_ref, o_ref):
    wt = wt_ref[...]
    f = wt[:, 0:1] * y0_ref[...] + wt[:, 1:2] * y1_ref[...]
    o_ref[...] = x_ref[...] + gf_ref[0] * f


def moe_combine(g, x, gf, y0, y1, wt):
    return pl.pallas_call(
        _combine_kernel,
        grid=g.grid,
        in_specs=[g.rows(D_MODEL), g.mod_spec(D_MODEL), g.rows(D_MODEL), g.rows(D_MODEL), g.rows(128)],
        out_specs=g.rows(D_MODEL),
        out_shape=jax.ShapeDtypeStruct((g.M, D_MODEL), F32),
        compiler_params=_cparams("parallel", "parallel"),
        name="moe_combine",
    )(x, g.mod_array(gf), y0, y1, wt)


def moe_ffn(g, x, nw, sc, sh, gf, wr, rb, w1, w3, w2):
    M = g.M
    h, logits = moe_router(g, x, nw, sc, sh, wr, rb)
    top_v, top_i = lax.top_k(logits[:, :NE], TOPK)
    top_w = jax.nn.softmax(top_v, axis=-1)
    tm = min(512, max(128, M // 4))
    tf = 1792
    flat_e = top_i.reshape(-1)
    onehot = (flat_e[:, None] == jnp.arange(NE)[None, :]).astype(jnp.int32)
    rank = jnp.take_along_axis(jnp.cumsum(onehot, axis=0) - onehot, flat_e[:, None], axis=1)[:, 0]
    counts = jnp.sum(onehot, axis=0)
    padded = ((counts + tm - 1) // tm) * tm
    starts = jnp.cumsum(padded) - padded
    pos = starts[flat_e] + rank
    n_rows = M * TOPK + NE * tm
    row_token = jnp.zeros((n_rows,), jnp.int32).at[pos].set(jnp.arange(M * TOPK, dtype=jnp.int32) // TOPK)
    tile_start = jnp.arange(n_rows // tm, dtype=jnp.int32) * tm
    ends = starts + padded
    tile_expert = jnp.minimum(jnp.sum((tile_start[:, None] >= ends[None, :]).astype(jnp.int32), axis=1), NE - 1)
    tile_valid = (tile_start < ends[NE - 1]).astype(jnp.int32)
    h_sorted = h.at[row_token].get(mode="promise_in_bounds")
    y_sorted = expert_ffn(tile_expert.astype(jnp.int32), tile_valid, h_sorted, w1, w3, w2, tm, tf)
    pos2 = pos.reshape(M, TOPK)
    y0 = y_sorted.at[pos2[:, 0]].get(mode="promise_in_bounds")
    y1 = y_sorted.at[pos2[:, 1]].get(mode="promise_in_bounds")
    wt = jnp.pad(top_w, ((0, 0), (0, 128 - TOPK)))
    return moe_combine(g, x, gf, y0, y1, wt)


def _final_norm_kernel(x_ref, w_ref, o_ref):
    o_ref[...] = _rms(x_ref[...]) * w_ref[...]


def final_norm(g, x, w):
    return pl.pallas_call(
        _final_norm_kernel,
        grid=g.grid,
        in_specs=[g.rows(D_MODEL), g.full((1, D_MODEL))],
        out_specs=g.rows(D_MODEL),
        out_shape=jax.ShapeDtypeStruct((g.M, D_MODEL), F32),
        compiler_params=_cparams("parallel", "parallel"),
        name="final_norm",
    )(x, w.reshape(1, -1))


def _rot_half_cols(w):
    half = ROPE // 2
    return jnp.concatenate([-w[..., half:], w[..., :half]], axis=-1)


def _pack_layer(l, W):
    p = {}
    w_in = W["w_in"][l]
    o = 0
    q_c, kv_c, kr = w_in[:, 0:QL], w_in[:, QL:QL + KVL], w_in[:, QL + KVL:QL + KVL + ROPE]
    o = QL + KVL + ROPE
    rw = w_in[:, o:o + RWKV_IN]
    o += RWKV_IN
    z = w_in[:, o:o + SDIM]
    o += SDIM
    xbc = w_in[:, o:o + SCD]
    o += SCD
    dt = w_in[:, o:o + SHEADS]
    o += SHEADS
    gate = w_in[:, o:]
    seg_a = jnp.concatenate([q_c, kv_c, kr, dt, _rot_half_cols(kr),
                             jnp.zeros((D_MODEL, SEG_A - (QL + KVL + 2 * ROPE + SHEADS)), F32)], axis=1)
    mw = {}
    mw["w_in"] = jnp.concatenate([seg_a, rw, z, xbc, gate], axis=1)
    wq = W["mla_w_uq"][l].reshape(QL, H, NOPE + ROPE)
    pe = wq[:, :, NOPE:]
    mw["wuq"] = jnp.concatenate([wq[:, :, :NOPE].reshape(QL, H * NOPE), pe.reshape(QL, H * ROPE),
                                 _rot_half_cols(pe).reshape(QL, H * ROPE)], axis=1)
    mw["wuk"] = jnp.transpose(W["mla_w_uk"][l], (1, 2, 0))
    mw["wuv"] = jnp.transpose(W["mla_w_uv"][l], (1, 0, 2))
    mw["w2"], mw["a2"], mw["g2"] = W["rwkv_w2"][l], W["rwkv_a2"][l], W["rwkv_g2"][l]
    mw["w_branch"], mw["w_out"] = W["w_branch"][l], W["w_out"][l]
    if l % 2 == 0:
        mw["ffn"] = tuple(W[n][l // 2] for n in ("ffn_w1", "ffn_w3", "ffn_w2"))
    p["mw"] = mw
    p["mw16"] = jax.tree_util.tree_map(lambda a: a.astype(BF16), mw)
    p["q_norm"], p["kv_norm"] = W["mla_q_norm"][l], W["mla_kv_norm"][l]
    p["rwkv"] = dict(mu=W["rwkv_mu"][l], w0=W["rwkv_w0"][l], a0=W["rwkv_a0"][l], k_k=W["rwkv_k_k"][l],
                     k_a=W["rwkv_k_a"][l], ln_w=W["rwkv_ln_w"][l], ln_b=W["rwkv_ln_b"][l],
                     r_k=W["rwkv_r_k"][l].reshape(-1))
    lanes = jnp.arange(128)
    head_lane = (lanes >= DT_LANE) & (lanes < DT_LANE + SHEADS)
    idx = jnp.clip(lanes - DT_LANE, 0, SHEADS - 1)
    p["ssm"] = dict(conv_w=W["ssm_conv_w"][l], conv_b=W["ssm_conv_b"][l],
                    dtb128=jnp.where(head_lane, W["ssm_dt_bias"][l][idx], 0.0).reshape(1, 128),
                    a128=jnp.where(head_lane, -jnp.exp(W["ssm_a_log"][l][idx]), 0.0).reshape(1, 128),
                    dskip512=jnp.repeat(W["ssm_d"][l], SP).reshape(1, SDIM), ssm_norm=W["ssm_norm"][l])
    p["norm_attn"], p["norm_ffn"] = W["norm_attn"][l], W["norm_ffn"][l]
    p["w_ada"], p["b_ada"] = W["w_ada"][l], W["b_ada"][l]
    if l % 2 == 1:
        wr = jnp.pad(W["moe_router"][l // 2], ((0, 0), (0, 128 - NE)))
        p["router"] = (wr, jnp.pad(W["moe_router_b"][l // 2], (0, 128 - NE)).reshape(1, 128))
        p["moe"] = tuple(W[n][l // 2].astype(BF16) for n in ("moe_w1", "moe_w3", "moe_w2"))
    return p


def _rope_tables(pos):
    half = ROPE // 2
    freq = ROPE_THETA ** (-jnp.arange(half, dtype=F32) / half)
    ang = pos.astype(F32)[:, None] * freq[None, :]
    cos = jnp.concatenate([jnp.cos(ang)] * 2, axis=-1)
    sin = jnp.concatenate([jnp.sin(ang)] * 2, axis=-1)
    return cos, sin, jnp.tile(cos, (1, H)), jnp.tile(sin, (1, H))


def _trunk(x3, c, pos, paged, shift0, wkv0, conv0, ssm0, layers, norm_final):
    B, S, _ = x3.shape
    M = B * S
    g = _Group(B, S, 512)
    g_win = _Group(B, S, 256)
    x = x3.reshape(M, D_MODEL)
    tabs = _rope_tables(pos if S > 1 else jnp.broadcast_to(pos, (M,)))
    ri = jnp.arange(RDIM)
    ones_bd = ((ri[:, None] // RHD) == (ri[None, :] // RHD)).astype(BF16)
    outs = [[] for _ in range(6)]
    for l, p in enumerate(layers):
        mw = p["mw"] if g.precise else p["mw16"]
        ada = ada_matmul(c, p["w_ada"], p["b_ada"])
        sh_a, sc_a, g_a, sh_f, sc_f, g_f = jnp.split(ada, 6, axis=-1)
        seg_a, rw, z, xbc, gate = win_project(g_win, x, p["norm_attn"], sc_a, sh_a, mw["w_in"])

        qc, kc, ckv, kpe = mla_prep(g, seg_a, tabs, p["q_norm"], p["kv_norm"], mw["wuq"], mw["wuk"])
        if paged is None:
            a_out = mla_attention(B, S, qc, kc, mw["wuv"])
        else:
            cache_ckv, cache_kpe_t, page_table = paged
            a_out = paged_attention(l, page_table, qc, kc, cache_ckv, cache_kpe_t, mw["wuv"])

        rw3 = rw.reshape(B, S, RWKV_IN)
        tc_len = min(RHD, S)
        nch = S // tc_len
        tpb, hpt = SCAN_TPB, SCAN_HPT
        as_rows = lambda a: a.reshape(S, B * RDIM)
        if S > 1:
            r_t, w_t, k_t, v_t, kk_t, b_t, gate_r, vt = rwkv_prep(g, rw, shift0[l], p["rwkv"], mw, ones_bd)
        else:
            r_t, w_t, k_t, v_t, kk_t, b_t, gate_r = rwkv_prep(g, rw, shift0[l], p["rwkv"], mw, ones_bd)
            vt = v_t.reshape(nch, tc_len, B, tpb, hpt, RHD).transpose(0, 2, 3, 5, 4, 1)
            vt = jnp.pad(vt, ((0, 0),) * 5 + ((0, RHD - tc_len),)).reshape(nch, B * tpb, RHD, SCAN_W)
        s0 = wkv0[l].reshape(B, tpb, hpt, RHD, RHD).transpose(0, 1, 3, 2, 4).reshape(B * tpb, RHD, SCAN_W)
        yt, s1 = rwkv_scan(B, S, as_rows(r_t), as_rows(w_t), as_rows(k_t), as_rows(kk_t), as_rows(b_t), vt, s0,
                           g.precise)
        ysub = min(SCAN_YSUB, tc_len)
        y = yt.reshape(nch, B, tpb, RHD, SCAN_W // 128, hpt, SCAN_YSUB)[:, :, :, :, :tc_len // ysub, :, :ysub]
        y = y.transpose(1, 0, 4, 6, 2, 5, 3).reshape(M, RDIM)
        wkv1 = s1.reshape(B, tpb, RHD, hpt, RHD).transpose(0, 1, 3, 2, 4).reshape(B, tpb * hpt, RHD, RHD)
        r_out = rwkv_post(g, y, r_t, k_t, v_t, gate_r, p["rwkv"], ones_bd)
        shift1 = rw3[:, -1]

        ps = p["ssm"]
        if S > 1:
            tail0 = jnp.pad(conv0[l], ((0, 0), (8 - (SCONV - 1), 0), (0, 0)))
            s_out, h_t = ssd_prompt(B, S, xbc, z, seg_a, tail0, ps)
            ssm1 = jnp.swapaxes(h_t, 2, 3)
            conv1 = xbc.reshape(B, S, SCD)[:, S - (SCONV - 1):]
        else:
            c0, c1, c2 = conv0[l][:, 0], conv0[l][:, 1], conv0[l][:, 2]
            act, xdt, dec128 = ssm_step_pre(g, xbc, c0, c1, c2, seg_a, ps)
            xb = jnp.broadcast_to(xdt.reshape(B, SHEADS, SP, 1), (B, SHEADS, SP, SN))
            dec = jnp.broadcast_to(dec128[:, DT_LANE:DT_LANE + SHEADS].reshape(B, SHEADS, 1, 1), (B, SHEADS, 1, SN))
            ssm1, y_s = ssm_step(B, l, ssm0, xb, dec, act[:, SDIM:SDIM + SGROUPS * SN], act[:, SDIM + SGROUPS * SN:])
            s_out = ssm_step_post(g, y_s, act[:, :SDIM], z, ps)
            conv1 = jnp.concatenate([conv0[l][:, 1:], xbc[:, None, :]], axis=1)

        x = merge_branches(g, a_out, r_out, s_out, gate, x, g_a, mw["w_branch"], mw["w_out"])

        if l % 2 == 0:
            w1, w3, w2 = mw["ffn"]
            x = dense_ffn(g, x, p["norm_ffn"], sc_f, sh_f, g_f, w1, w3, w2, w1.shape[1] // 2)
        else:
            w1, w3, w2 = p["moe"]
            x = moe_ffn(g, x, p["norm_ffn"], sc_f, sh_f, g_f, *p["router"], w1, w3, w2)
        for lst, v in zip(outs, (ckv.reshape(B, S, KVL), kpe.reshape(B, S, ROPE), shift1, wkv1, conv1, ssm1)):
            lst.append(v)
    y = final_norm(g, x, norm_final).reshape(B, S, D_MODEL)
    return y, [jnp.stack(v) for v in outs]


def kernel(x_prompt, x_sample, cache_ckv, cache_kpe, state_rwkv_shift, state_rwkv_wkv, state_ssm_conv, state_ssm, page_table, c_prompt, c_sample, w_ada, b_ada, norm_attn, norm_ffn, norm_final, w_in, mla_q_norm, mla_w_uq, mla_kv_norm, mla_w_uk, mla_w_uv, rwkv_mu, rwkv_w0, rwkv_w2, rwkv_a0, rwkv_a2, rwkv_g2, rwkv_k_k, rwkv_k_a, rwkv_r_k, rwkv_ln_w, rwkv_ln_b, ssm_conv_w, ssm_conv_b, ssm_dt_bias, ssm_a_log, ssm_d, ssm_norm, w_branch, w_out, ffn_w1, ffn_w3, ffn_w2, moe_router, moe_router_b, moe_w1, moe_w3, moe_w2):
    W = dict(w_ada=w_ada, b_ada=b_ada, norm_attn=norm_attn, norm_ffn=norm_ffn, w_in=w_in, mla_q_norm=mla_q_norm,
             mla_w_uq=mla_w_uq, mla_kv_norm=mla_kv_norm, mla_w_uk=mla_w_uk, mla_w_uv=mla_w_uv, rwkv_mu=rwkv_mu,
             rwkv_w0=rwkv_w0, rwkv_w2=rwkv_w2, rwkv_a0=rwkv_a0, rwkv_a2=rwkv_a2, rwkv_g2=rwkv_g2, rwkv_k_k=rwkv_k_k,
             rwkv_k_a=rwkv_k_a, rwkv_r_k=rwkv_r_k, rwkv_ln_w=rwkv_ln_w, rwkv_ln_b=rwkv_ln_b, ssm_conv_w=ssm_conv_w,
             ssm_conv_b=ssm_conv_b, ssm_dt_bias=ssm_dt_bias, ssm_a_log=ssm_a_log, ssm_d=ssm_d, ssm_norm=ssm_norm,
             w_branch=w_branch, w_out=w_out, ffn_w1=ffn_w1, ffn_w3=ffn_w3, ffn_w2=ffn_w2, moe_router=moe_router,
             moe_router_b=moe_router_b, moe_w1=moe_w1, moe_w3=moe_w3, moe_w2=moe_w2)
    depth = w_in.shape[0]
    layers = [_pack_layer(l, W) for l in range(depth)]
    bp, sp, _ = x_prompt.shape
    bs, ss, _ = x_sample.shape
    dt = x_prompt.dtype
    y_prompt, (p_ckv, p_kpe, p_shift, p_wkv, p_conv, p_ssm) = _trunk(
        x_prompt, c_prompt, jnp.arange(sp, dtype=jnp.int32), None,
        jnp.zeros((depth, bp, RWKV_IN), dt), jnp.zeros((depth, bp, 8, RHD, RHD), dt),
        jnp.zeros((depth, bp, SCONV - 1, SCD), dt), jnp.zeros((depth, bp, SHEADS, SP, SN), dt), layers, norm_final)
    p_ckv = p_ckv.reshape(depth, bp * sp // PAGE, PAGE, KVL)
    p_kpe = p_kpe.reshape(depth, bp * sp // PAGE, PAGE, ROPE)
    past_len = page_table.shape[1] * PAGE
    pos_s = past_len + jnp.arange(ss, dtype=jnp.int32)
    y_sample, (s_ckv, s_kpe, s_shift, s_wkv, s_conv, s_ssm) = _trunk(
        x_sample, c_sample, pos_s, (cache_ckv, jnp.swapaxes(cache_kpe, 2, 3), page_table),
        state_rwkv_shift, state_rwkv_wkv, state_ssm_conv, state_ssm, layers, norm_final)
    return (y_prompt, y_sample, p_ckv, p_kpe, p_shift, p_wkv, p_conv, p_ssm,
            s_ckv, s_kpe, s_shift, s_wkv, s_conv, s_ssm)
```

```python
import functools

import jax
import jax.numpy as jnp
from jax import lax
from jax.experimental import pallas as pl
from jax.experimental.pallas import tpu as pltpu

F32 = jnp.float32
BF16 = jnp.bfloat16

D_MODEL = 1024
DEPTH = 2
PAGE = 128
H = 8
NOPE = 64
ROPE = 32
VD = 64
QL = 256
KVL = 256
ROPE_THETA = 10000.0
MLA_SCALE = (NOPE + ROPE) ** -0.5
LOG2E = 1.4426950408889634
QK = 384
RDIM = 512
RHD = 64
DECAY_LORA = 64
AAA_LORA = 64
GATE_LORA = 128
RWKV_IN = 3 * RDIM + DECAY_LORA + AAA_LORA + GATE_LORA
RWKV_LN_EPS = 64e-5
SHEADS = 8
SP = 64
SDIM = 512
SGROUPS = 2
SN = 128
SCONV = 4
SCHUNK = 128
SCD = SDIM + 2 * SGROUPS * SN
NBRANCH = 3
NE = 8
TOPK = 2
EPS = 1e-6
SEG_A = 768
DT_TILE = 4
DT_LANE = 32
SEG_WIDTHS = (SEG_A, RWKV_IN, SDIM, SCD, NBRANCH * D_MODEL)
W_IN_COLS = sum(SEG_WIDTHS)
VMEM_LIMIT = 56 * 1024 * 1024


def _cparams(*sem):
    return pltpu.CompilerParams(dimension_semantics=sem, vmem_limit_bytes=VMEM_LIMIT)


def _dot(a, b):
    return jnp.dot(a, b, preferred_element_type=F32)


def _dot_nt(a, b):
    return lax.dot_general(a, b, (((1,), (1,)), ((), ())), preferred_element_type=F32)


def _hi_lo(x):
    hi = x.astype(BF16)
    return hi, (x.astype(F32) - hi.astype(F32)).astype(BF16)


def _dot3(a, w, dot=_dot):
    a_hi, a_lo = _hi_lo(a)
    w_hi, w_lo = _hi_lo(w)
    return dot(a_hi, w_hi) + dot(a_lo, w_hi) + dot(a_hi, w_lo)


def _mm(a, w, precise):
    return _dot3(a, w) if precise else _dot(a.astype(BF16), w)


def _split_dot(x, w01):
    hi = x.astype(BF16)
    lo = (x - hi.astype(F32)).astype(BF16)
    return _dot(hi, w01) + _dot(lo, w01)


def _split3_dot_left(w01, x):
    hi = x.astype(BF16)
    r1 = x - hi.astype(F32)
    mid = r1.astype(BF16)
    lo = (r1 - mid.astype(F32)).astype(BF16)
    return _dot(w01, hi) + _dot(w01, mid) + _dot(w01, lo)


def _sigmoid(x):
    return 1.0 / (1.0 + jnp.exp(-x))


def _silu(x):
    return x * _sigmoid(x)


def _softplus(x):
    return jnp.maximum(x, 0.0) + jnp.log(1.0 + jnp.exp(-jnp.abs(x)))


def _rms(x):
    return x * lax.rsqrt(jnp.mean(x * x, axis=-1, keepdims=True) + EPS)


class _Group:
    def __init__(self, B, S, tm):
        self.B, self.S, self.M = B, S, B * S
        self.precise = S == 1
        self.act_dtype = F32 if self.precise else BF16
        if S == 1:
            self.tm = min(tm, self.M)
            self.grid = (1, self.M // self.tm)
        else:
            self.tm = min(tm, S)
            self.grid = (B, S // self.tm)
        self.ns = self.grid[1]

    def rows(self, width, colblock=0):
        ns = self.ns
        return pl.BlockSpec((self.tm, width), lambda b, s: (b * ns + s, colblock))

    def full(self, shape):
        nd = len(shape)
        return pl.BlockSpec(shape, lambda b, s: (0,) * nd)

    def mod_array(self, m):
        return m.reshape(1, self.M, -1) if self.S == 1 else m.reshape(self.B, 1, -1)

    def mod_spec(self, width):
        if self.S == 1:
            return pl.BlockSpec((1, self.tm, width), lambda b, s: (0, s, 0))
        return pl.BlockSpec((1, 1, width), lambda b, s: (b, 0, 0))

    def pos_spec(self, width):
        if self.S == 1:
            return self.rows(width)
        return pl.BlockSpec((self.tm, width), lambda b, s: (s, 0))

    def tmaj_shape(self, width):
        return (self.M, width) if self.S == 1 else (self.S, self.B * width)

    def tmaj_spec(self, width):
        if self.S == 1:
            return self.rows(width)
        return pl.BlockSpec((self.tm, width), lambda b, s: (s, b))


def _ada_kernel(c_ref, w_ref, b_ref, o_ref):
    c = c_ref[...]
    o_ref[...] = _dot3(_silu(c), w_ref[...]) + b_ref[...]


def ada_matmul(c, w, b):
    m, k = c.shape
    n = w.shape[1]
    tn = 1024
    return pl.pallas_call(
        _ada_kernel,
        grid=(n // tn,),
        in_specs=[pl.BlockSpec((m, k), lambda j: (0, 0)), pl.BlockSpec((k, tn), lambda j: (0, j)),
                  pl.BlockSpec((1, tn), lambda j: (0, j))],
        out_specs=pl.BlockSpec((m, tn), lambda j: (0, j)),
        out_shape=jax.ShapeDtypeStruct((m, n), F32),
        compiler_params=_cparams("arbitrary"),
        name="ada_matmul",
    )(c, w, b.reshape(1, n))


def _win_kernel(x_ref, nw_ref, sc_ref, sh_ref, w_ref, *o_refs):
    h = (_rms(x_ref[...]) * nw_ref[...] * (1.0 + sc_ref[0]) + sh_ref[0]).astype(BF16)
    off = 0
    for o in o_refs:
        n = o.shape[-1]
        for c in range(0, n, 256):
            o[:, c:c + 256] = _dot(h, w_ref[:, off + c:off + c + 256])
        off += n


def _win_cols_kernel(x_ref, nw_ref, sc_ref, sh_ref, w_ref, o_ref):
    h = _rms(x_ref[...]) * nw_ref[...] * (1.0 + sc_ref[0]) + sh_ref[0]
    o_ref[...] = _dot3(h, w_ref[...])


def win_project_precise(g, x, nw, sc, sh, w_packed):
    tn = 512
    out = pl.pallas_call(
        _win_cols_kernel,
        grid=(W_IN_COLS // tn,),
        in_specs=[pl.BlockSpec((g.M, D_MODEL), lambda j: (0, 0)), pl.BlockSpec((1, D_MODEL), lambda j: (0, 0)),
                  pl.BlockSpec((1, g.M, D_MODEL), lambda j: (0, 0, 0)),
                  pl.BlockSpec((1, g.M, D_MODEL), lambda j: (0, 0, 0)),
                  pl.BlockSpec((D_MODEL, tn), lambda j: (0, j))],
        out_specs=pl.BlockSpec((g.M, tn), lambda j: (0, j)),
        out_shape=jax.ShapeDtypeStruct((g.M, W_IN_COLS), F32),
        compiler_params=_cparams("parallel"),
        name="win_project_precise",
    )(x, nw.reshape(1, -1), g.mod_array(sc), g.mod_array(sh), w_packed)
    offs = [0]
    for w in SEG_WIDTHS:
        offs.append(offs[-1] + w)
    return [out[:, offs[i]:offs[i + 1]] for i in range(len(SEG_WIDTHS))]


def win_project(g, x, nw, sc, sh, w_packed):
    if g.precise:
        return win_project_precise(g, x, nw, sc, sh, w_packed)
    return pl.pallas_call(
        _win_kernel,
        grid=g.grid,
        in_specs=[g.rows(D_MODEL), g.full((1, D_MODEL)), g.mod_spec(D_MODEL), g.mod_spec(D_MODEL),
                  g.full((D_MODEL, W_IN_COLS))],
        out_specs=[g.rows(w) for w in SEG_WIDTHS],
        out_shape=[jax.ShapeDtypeStruct((g.M, w), F32) for w in SEG_WIDTHS],
        compiler_params=_cparams("parallel", "parallel"),
        name="win_project",
    )(x, nw.reshape(1, -1), g.mod_array(sc), g.mod_array(sh), w_packed)


def _mla_prep_kernel(a_ref, cos_ref, sin_ref, cos8_ref, sin8_ref, qn_ref, kvn_ref, wuq_ref, wuk_ref,
                     qc_ref, kc_ref, ckv_ref, kpe_ref, *, precise):
    a = a_ref[...]
    tm = a.shape[0]
    odt = qc_ref.dtype
    qn = _rms(a[:, 0:QL]) * qn_ref[...]
    qa = _mm(qn, wuq_ref[...], precise)
    q_rope = qa[:, 512:768] * cos8_ref[...] + qa[:, 768:1024] * sin8_ref[...]
    zpad = jnp.zeros((tm, QK - KVL - ROPE), odt)
    qscale = MLA_SCALE if precise else MLA_SCALE * LOG2E
    for h in range(H):
        q_abs = _mm(qa[:, h * NOPE:(h + 1) * NOPE], wuk_ref[h], precise) * qscale
        qc_ref[h, :, 0:KVL] = q_abs.astype(odt)
        qc_ref[h, :, KVL:KVL + ROPE] = (q_rope[:, h * ROPE:(h + 1) * ROPE] * qscale).astype(odt)
        qc_ref[h, :, KVL + ROPE:QK] = zpad
    ckv = _rms(a[:, QL:QL + KVL]) * kvn_ref[...]
    kpe = a[:, 512:544] * cos_ref[...] + a[:, 552:584] * sin_ref[...]
    ckv_ref[...] = ckv
    kpe_ref[...] = kpe
    kc_ref[:, 0:KVL] = ckv.astype(odt)
    kc_ref[:, KVL:KVL + ROPE] = kpe.astype(odt)
    kc_ref[:, KVL + ROPE:QK] = zpad


def mla_prep(g, seg_a, tabs, qn, kvn, wuq, wuk):
    cos, sin, cos8, sin8 = tabs
    ns = g.ns
    return pl.pallas_call(
        functools.partial(_mla_prep_kernel, precise=g.precise),
        grid=g.grid,
        in_specs=[g.rows(SEG_A), g.pos_spec(ROPE), g.pos_spec(ROPE), g.pos_spec(H * ROPE), g.pos_spec(H * ROPE),
                  g.full((1, QL)), g.full((1, KVL)), g.full((QL, 1024)), g.full((H, NOPE, KVL))],
        out_specs=[pl.BlockSpec((H, g.tm, QK), lambda b, s: (0, b * ns + s, 0)), g.rows(QK), g.rows(KVL),
                   g.rows(ROPE)],
        out_shape=[jax.ShapeDtypeStruct((H, g.M, QK), g.act_dtype), jax.ShapeDtypeStruct((g.M, QK), g.act_dtype),
                   jax.ShapeDtypeStruct((g.M, KVL), F32), jax.ShapeDtypeStruct((g.M, ROPE), F32)],
        compiler_params=_cparams("parallel", "parallel"),
        name="mla_prep",
    )(seg_a, cos, sin, cos8, sin8, qn.reshape(1, -1), kvn.reshape(1, -1), wuq, wuk)


NEG = -1e30


def _attn_kernel(q_ref, k_ref, wuv_ref, o_ref, m_scr, l_scr, a_scr, acc_scr, s_scr, p_scr, *, tq, tk):
    qi = pl.program_id(1)
    q = q_ref[...].reshape(H * tq, QK)
    m_scr[...] = jnp.full(m_scr.shape, NEG, F32)
    l_scr[...] = jnp.zeros(l_scr.shape, F32)
    acc_scr[...] = jnp.zeros(acc_scr.shape, F32)
    reps = tk // 128
    wide = lambda a: jnp.concatenate([a] * reps, axis=-1)

    def keys(j):
        return k_ref[pl.ds(pl.multiple_of(j * tk, tk), tk), :]

    def scores(j, slot):
        s_scr[slot] = _dot_nt(q, keys(j))

    def softmax_pv(j, slot, masked):
        if masked:
            visible = (j * tk + lax.broadcasted_iota(jnp.int32, (tq, tk), 1)
                       <= qi * tq + lax.broadcasted_iota(jnp.int32, (tq, tk), 0))
        for h in range(H):
            rs = pl.ds(h * tq, tq)
            s = s_scr[slot, rs, :]
            if masked:
                s = jnp.where(visible, s, NEG)
            m_prev = m_scr[rs, :]
            m_new = jnp.maximum(m_prev, jnp.max(s, axis=-1, keepdims=True))
            alpha = jnp.exp2(m_prev - m_new)
            p = jnp.exp2(s - wide(m_new))
            l_scr[rs, :] = alpha * l_scr[rs, :] + jnp.sum(p, axis=-1, keepdims=True)
            m_scr[rs, :] = m_new
            a_scr[rs, :] = alpha
            p_scr[slot, rs, :] = p.astype(BF16)
        alpha = a_scr[...]
        acc_scr[...] = (acc_scr[...] * jnp.concatenate([alpha] * (KVL // 128), axis=-1)
                        + _dot(p_scr[slot], keys(j)[:, 0:KVL]))

    n_full = (qi * tq) // tk

    scores(0, 0)

    def body(i, carry):
        j = 2 * i
        scores(j + 1, 1)
        softmax_pv(j, 0, False)
        scores(j + 2, 0)
        softmax_pv(j + 1, 1, False)
        return carry

    lax.fori_loop(0, n_full // 2, body, 0)

    @pl.when(n_full % 2 == 0)
    def _():
        softmax_pv(n_full, 0, True)

    @pl.when(n_full % 2 == 1)
    def _():
        scores(n_full, 1)
        softmax_pv(n_full - 1, 0, False)
        softmax_pv(n_full, 1, True)
    inv_l = 1.0 / l_scr[...]
    o = acc_scr[...] * jnp.concatenate([inv_l] * (KVL // 128), axis=-1)
    for h in range(H):
        o_ref[:, h * VD:(h + 1) * VD] = _dot(o[h * tq:(h + 1) * tq].astype(BF16), wuv_ref[h]).astype(o_ref.dtype)


def mla_attention(B, S, qc, kc, wuv):
    tq = min(128, S)
    tk = min(256, S)
    nq = S // tq
    rows = H * tq
    return pl.pallas_call(
        functools.partial(_attn_kernel, tq=tq, tk=tk),
        grid=(B, nq),
        in_specs=[pl.BlockSpec((H, tq, QK), lambda b, i: (0, b * nq + i, 0)),
                  pl.BlockSpec((S, QK), lambda b, i: (b, 0)),
                  pl.BlockSpec((H, KVL, VD), lambda b, i: (0, 0, 0))],
        out_specs=pl.BlockSpec((tq, H * VD), lambda b, i: (b * nq + i, 0)),
        out_shape=jax.ShapeDtypeStruct((B * S, H * VD), BF16),
        scratch_shapes=[pltpu.VMEM((rows, 128), F32), pltpu.VMEM((rows, 128), F32), pltpu.VMEM((rows, 128), F32),
                        pltpu.VMEM((rows, KVL), F32), pltpu.VMEM((2, rows, tk), F32),
                        pltpu.VMEM((2, rows, tk), BF16)],
        compiler_params=_cparams("parallel", "parallel"),
        name="mla_attention",
    )(qc, kc, wuv)


PAGES_PER_STEP = 16


def _paged_kernel(pt_ref, q_ref, knew_ref, wuv_ref, *rest):
    pp = PAGES_PER_STEP
    ckv_refs, kpe_refs = rest[:pp], rest[pp:2 * pp]
    o_ref, m_scr, l_scr, acc_scr = rest[2 * pp:]
    j = pl.program_id(1)
    q = q_ref[...]

    @pl.when(j == 0)
    def _():
        m_scr[...] = jnp.full(m_scr.shape, NEG, F32)
        l_scr[...] = jnp.zeros(l_scr.shape, F32)
        acc_scr[...] = jnp.zeros(acc_scr.shape, F32)

    def stack_hi_lo(x):
        hi = x.astype(BF16).astype(F32)
        return jnp.concatenate([hi, x - hi], axis=0).astype(BF16)

    qa2 = stack_hi_lo(q[:, 0:KVL])
    qp2 = stack_hi_lo(q[:, KVL:KVL + ROPE])

    def chain(refs_c, refs_p):
        ckv_hi, ckv_lo = _hi_lo(jnp.concatenate([r[...] for r in refs_c], axis=0))
        kpe_hi, kpe_lo = _hi_lo(jnp.concatenate([r[...] for r in refs_p], axis=1))
        s2 = _dot_nt(qa2, ckv_hi) + _dot(qp2, kpe_hi)
        s = s2[0:H] + s2[H:2 * H] + _dot_nt(qa2[0:H], ckv_lo) + _dot(qp2[0:H], kpe_lo)
        m = jnp.max(s, axis=-1, keepdims=True)
        p = jnp.exp(s - m)
        p2 = stack_hi_lo(p)
        pv2 = _dot(p2, jnp.concatenate([ckv_hi, ckv_lo], axis=1))
        pv = pv2[0:H, 0:KVL] + pv2[H:2 * H, 0:KVL] + pv2[0:H, KVL:2 * KVL]
        return m, jnp.sum(p, axis=-1, keepdims=True), pv

    half = pp // 2
    parts = [chain(ckv_refs[i * half:(i + 1) * half], kpe_refs[i * half:(i + 1) * half]) for i in range(2)]
    m_prev = m_scr[...]
    m_new = jnp.maximum(m_prev, jnp.maximum(parts[0][0], parts[1][0]))
    alpha = jnp.exp(m_prev - m_new)
    l = alpha * l_scr[...]
    acc = alpha * acc_scr[...]
    for m_i, l_i, pv_i in parts:
        w_i = jnp.exp(m_i - m_new)
        l = l + w_i * l_i
        acc = acc + w_i * pv_i
    l_scr[...] = l
    acc_scr[...] = acc
    m_scr[...] = m_new

    @pl.when(j == pl.num_programs(1) - 1)
    def _():
        kn = knew_ref[0]
        s_new = jnp.sum(q * kn, axis=-1, keepdims=True)
        m_prev = m_scr[...]
        m_new = jnp.maximum(m_prev, s_new)
        alpha = jnp.exp(m_prev - m_new)
        p_new = jnp.exp(s_new - m_new)
        l = alpha * l_scr[...] + p_new
        acc = alpha * acc_scr[...] + p_new * kn[:, 0:KVL]
        o = acc / l
        for h in range(H):
            o_ref[0, :, h * VD:(h + 1) * VD] = _dot3(o[h:h + 1], wuv_ref[h])


def paged_attention(layer, page_table, qc, kc_new, cache_ckv, cache_kpe_t, wuv):
    B, n_pages = page_table.shape
    pp = PAGES_PER_STEP
    nsteps = n_pages // pp

    def page_spec(i, shape):
        return pl.BlockSpec((None, None) + shape, lambda b, j, pt: (layer, pt[b, j * pp + i], 0, 0))

    grid_spec = pltpu.PrefetchScalarGridSpec(
        num_scalar_prefetch=1,
        grid=(B, nsteps),
        in_specs=[pl.BlockSpec((None, H, QK), lambda b, j, pt: (b, 0, 0)),
                  pl.BlockSpec((1, 1, QK), lambda b, j, pt: (b, 0, 0)),
                  pl.BlockSpec((H, KVL, VD), lambda b, j, pt: (0, 0, 0))]
        + [page_spec(i, (PAGE, KVL)) for i in range(pp)] + [page_spec(i, (ROPE, PAGE)) for i in range(pp)],
        out_specs=pl.BlockSpec((1, 1, H * VD), lambda b, j, pt: (b, 0, 0)),
        scratch_shapes=[pltpu.VMEM((H, 1), F32), pltpu.VMEM((H, 1), F32), pltpu.VMEM((H, KVL), F32)],
    )
    out = pl.pallas_call(
        _paged_kernel,
        grid_spec=grid_spec,
        out_shape=jax.ShapeDtypeStruct((B, 1, H * VD), F32),
        compiler_params=_cparams("parallel", "arbitrary"),
        name="paged_attention",
    )(page_table, jnp.transpose(qc, (1, 0, 2)), kc_new.reshape(B, 1, QK), wuv, *([cache_ckv] * pp),
      *([cache_kpe_t] * pp))
    return out.reshape(B, H * VD)


def _rwkv_prep_kernel(rw_ref, prev_ref, mu_ref, w0_ref, a0_ref, kk_ref_, ka_ref, w2_ref, a2_ref, g2_ref, ones_ref,
                      r_o, w_o, k_o, v_o, kk_o, b_o, g_o, *rest, precise, seq):
    rw = rw_ref[...]
    if seq:
        vt_o, carry = rest
        tm = rw.shape[0]

        @pl.when(pl.program_id(1) == 0)
        def _():
            carry[...] = prev_ref[0]

        row8 = lax.broadcasted_iota(jnp.int32, (8, RWKV_IN), 0)
        shifted = pltpu.roll(rw, 1, 0)
        top = jnp.where(row8 == 0, pltpu.roll(carry[...], 1, 0), shifted[0:8])
        prev = jnp.concatenate([top, shifted[8:]], axis=0)
        carry[...] = rw[tm - 8:tm]
    else:
        prev = prev_ref[...]
    xs = rw + (prev - rw) * mu_ref[...]
    r = xs[:, 0:RDIM]
    k = xs[:, RDIM:2 * RDIM]
    v = xs[:, 2 * RDIM:3 * RDIM]
    o = 3 * RDIM
    wl = xs[:, o:o + DECAY_LORA]
    al = xs[:, o + DECAY_LORA:o + DECAY_LORA + AAA_LORA]
    gl = xs[:, o + DECAY_LORA + AAA_LORA:RWKV_IN]
    w = -_softplus(-(w0_ref[...] + _mm(jnp.tanh(wl), w2_ref[...], precise))) - 0.5
    a = _sigmoid(a0_ref[...] + _mm(al, a2_ref[...], precise))
    kk = k * kk_ref_[...]
    kk = kk * lax.rsqrt(jnp.maximum(_split_dot(kk * kk, ones_ref[...]), 1e-24))
    r_o[...] = r
    w_o[...] = jnp.exp(-jnp.exp(w))
    k_o[...] = k * (1.0 + (a - 1.0) * ka_ref[...])
    v_o[...] = v
    kk_o[...] = kk
    b_o[...] = kk * a
    g_o[...] = _mm(_sigmoid(gl), g2_ref[...], precise)
    if seq:
        ri = lax.broadcasted_iota(jnp.int32, (SCAN_W, SCAN_W), 0)
        ci = lax.broadcasted_iota(jnp.int32, (SCAN_W, SCAN_W), 1)
        same_head = (ri // RHD) == (ci // RHD)
        eye_rep = (lax.broadcasted_iota(jnp.int32, (RHD, SCAN_W), 0)
                   == lax.broadcasted_iota(jnp.int32, (RHD, SCAN_W), 1) % RHD).astype(BF16)
        for c in range(tm // RHD):
            for t in range(SCAN_TPB):
                v_c = v[c * RHD:(c + 1) * RHD, t * SCAN_W:(t + 1) * SCAN_W].astype(BF16)
                blockdiag = jnp.where(same_head, jnp.concatenate([v_c] * SCAN_HPT, axis=0), jnp.zeros((), BF16))
                vt_o[c, t] = _dot_nt(eye_rep, blockdiag).astype(vt_o.dtype)


def rwkv_prep(g, rw, prev, p, mw, ones_bd):
    vec = lambda a: a.reshape(1, -1)
    tshape = jax.ShapeDtypeStruct(g.tmaj_shape(RDIM), F32)
    seq = g.S > 1
    out_specs = [g.tmaj_spec(RDIM)] * 6 + [g.rows(RDIM)]
    out_shape = [tshape] * 6 + [jax.ShapeDtypeStruct((g.M, RDIM), F32)]
    scratch = []
    if seq:
        assert g.tm % RHD == 0
        cpt = g.tm // RHD
        prev = jnp.pad(prev[:, None, :], ((0, 0), (7, 0), (0, 0)))
        prev_spec = pl.BlockSpec((1, 8, RWKV_IN), lambda b, s: (b, 0, 0))
        out_specs.append(pl.BlockSpec((cpt, SCAN_TPB, RHD, SCAN_W), lambda b, s: (s, b, 0, 0)))
        out_shape.append(jax.ShapeDtypeStruct((g.S // RHD, g.B * SCAN_TPB, RHD, SCAN_W), BF16))
        scratch = [pltpu.VMEM((8, RWKV_IN), F32)]
    else:
        prev_spec = g.rows(RWKV_IN)
    return pl.pallas_call(
        functools.partial(_rwkv_prep_kernel, precise=g.precise, seq=seq),
        grid=g.grid,
        in_specs=[g.rows(RWKV_IN), prev_spec, g.full((1, RWKV_IN)), g.full((1, RDIM)), g.full((1, RDIM)),
                  g.full((1, RDIM)), g.full((1, RDIM)), g.full((DECAY_LORA, RDIM)), g.full((AAA_LORA, RDIM)),
                  g.full((GATE_LORA, RDIM)), g.full((RDIM, RDIM))],
        out_specs=out_specs,
        out_shape=out_shape,
        scratch_shapes=scratch,
        compiler_params=_cparams("parallel", "arbitrary"),
        name="rwkv_prep",
    )(rw, prev, vec(p["mu"]), vec(p["w0"]), vec(p["a0"]), vec(p["k_k"]), vec(p["k_a"]), mw["w2"], mw["a2"], mw["g2"],
      ones_bd)


SCAN_NB = 16
SCAN_HPT = 4
SCAN_W = SCAN_HPT * RHD
SCAN_TPB = RDIM // SCAN_W
SCAN_GB = 8
SCAN_YSUB = 128 // SCAN_HPT


def _scan_kernel(r_ref, w_ref, k_ref, kk_ref, b_ref, vt_ref, s0_ref, yt_ref, sout_ref, s_scr, *, tc_len, ng, precise):
    tc = pl.program_id(1)

    def pick(x, w01):
        return _split_dot(x, w01) if precise else _dot(x.astype(BF16), w01)

    @pl.when(tc == 0)
    def _():
        s_scr[...] = s0_ref[...]

    ri = lax.broadcasted_iota(jnp.int32, (SCAN_W, SCAN_W), 0)
    ci = lax.broadcasted_iota(jnp.int32, (SCAN_W, SCAN_W), 1)
    same_head = (ri // RHD) == (ci // RHD)
    ones_bd = same_head.astype(BF16)
    yri = lax.broadcasted_iota(jnp.int32, (SCAN_W, 128), 0)
    yci = lax.broadcasted_iota(jnp.int32, (SCAN_W, 128), 1)
    yt_ref[...] = jnp.zeros(yt_ref.shape, F32)
    gb = min(SCAN_GB, ng)
    sub = min(SCAN_YSUB, tc_len)

    def step(tt, carry, part):
        tg = part * sub + tt
        e_t = (same_head & ((ri % RHD) == tg)).astype(BF16)
        y_t = ((yri // RHD) * SCAN_YSUB + tt == yci).astype(BF16)
        ylanes = pl.ds(part * 128, 128)

        def rows(ref, g0):
            return jnp.stack([jnp.broadcast_to(ref[pl.ds(tg, 1), pl.ds((g0 + i) * SCAN_W, SCAN_W)], (RHD, SCAN_W))
                              for i in range(gb)])

        def issue(g0):
            sa = pick((s_scr[g0:g0 + gb] * rows(kk_ref, g0)).reshape(gb * RHD, SCAN_W), ones_bd)
            vcol = pick(vt_ref[0, g0:g0 + gb].reshape(gb * RHD, SCAN_W), e_t)
            return sa.reshape(gb, RHD, SCAN_W), vcol.reshape(gb, RHD, SCAN_W)

        pend = issue(0)
        y_pend = None
        for g0 in range(0, ng, gb):
            nxt = issue(g0 + gb) if g0 + gb < ng else None
            sa, vcol = pend
            s = s_scr[g0:g0 + gb] * rows(w_ref, g0) - sa * rows(b_ref, g0) + vcol * rows(k_ref, g0)
            s_scr[g0:g0 + gb] = s
            y = pick((s * rows(r_ref, g0)).reshape(gb * RHD, SCAN_W), y_t).reshape(gb, RHD, 128)
            if y_pend is not None:
                gp, yp = y_pend
                yt_ref[0, gp:gp + gb, :, ylanes] += yp
            y_pend = (g0, y)
            pend = nxt
        gp, yp = y_pend
        yt_ref[0, gp:gp + gb, :, ylanes] += yp
        return carry

    unroll = 4 if sub % 4 == 0 else 1
    for part in range(tc_len // sub):
        lax.fori_loop(0, sub, functools.partial(step, part=part), 0, unroll=unroll)

    @pl.when(tc == pl.num_programs(1) - 1)
    def _():
        sout_ref[...] = s_scr[...]


def rwkv_scan(B, S, r, w, k, kk, b, vt, s0, precise):
    tc_len = min(RHD, S)
    nb = min(SCAN_NB, B)
    ng = nb * SCAN_TPB
    nchunks = S // tc_len
    row_spec = pl.BlockSpec((tc_len, nb * RDIM), lambda bg, c: (c, bg))
    st_spec = pl.BlockSpec((ng, RHD, SCAN_W), lambda bg, c: (bg, 0, 0))
    ch_spec = pl.BlockSpec((1, ng, RHD, SCAN_W), lambda bg, c: (c, bg, 0, 0))
    return pl.pallas_call(
        functools.partial(_scan_kernel, tc_len=tc_len, ng=ng, precise=precise),
        grid=(B // nb, nchunks),
        in_specs=[row_spec] * 5 + [ch_spec, st_spec],
        out_specs=[ch_spec, st_spec],
        out_shape=[jax.ShapeDtypeStruct((nchunks, B * SCAN_TPB, RHD, SCAN_W), F32),
                   jax.ShapeDtypeStruct((B * SCAN_TPB, RHD, SCAN_W), F32)],
        scratch_shapes=[pltpu.VMEM((ng, RHD, SCAN_W), F32)],
        compiler_params=_cparams("parallel", "arbitrary"),
        name="rwkv_scan",
    )(r, w, k, kk, b, vt, s0)


def _rwkv_post_kernel(y_ref, r_ref, k_ref, v_ref, g_ref, lnw_ref, lnb_ref, rk_ref, ones_ref, o_ref):
    ones = ones_ref[...]
    y = y_ref[...]
    mu = _split_dot(y, ones) * (1.0 / RHD)
    yc = y - mu
    var = _split_dot(yc * yc, ones) * (1.0 / RHD)
    yn = yc * lax.rsqrt(var + RWKV_LN_EPS) * lnw_ref[...] + lnb_ref[...]
    v = v_ref[...]
    bonus = _split_dot(r_ref[...] * k_ref[...] * rk_ref[...], ones)
    o_ref[...] = ((yn + bonus * v) * g_ref[...]).astype(o_ref.dtype)


def rwkv_post(g, y, r, k, v, gate, p, ones_bd):
    vec = lambda a: a.reshape(1, -1)
    return pl.pallas_call(
        _rwkv_post_kernel,
        grid=g.grid,
        in_specs=[g.rows(RDIM), g.tmaj_spec(RDIM), g.tmaj_spec(RDIM), g.tmaj_spec(RDIM), g.rows(RDIM),
                  g.full((1, RDIM)), g.full((1, RDIM)), g.full((1, RDIM)), g.full((RDIM, RDIM))],
        out_specs=g.rows(RDIM),
        out_shape=jax.ShapeDtypeStruct((g.M, RDIM), g.act_dtype),
        compiler_params=_cparams("parallel", "parallel"),
        name="rwkv_post",
    )(y, r, k, v, gate, vec(p["ln_w"]), vec(p["ln_b"]), vec(p["r_k"]), ones_bd)


def _ssm_gate_norm(y, z, nw):
    y = y * _silu(z)
    gw = SDIM // SGROUPS
    parts = [_rms(y[:, i * gw:(i + 1) * gw]) for i in range(SGROUPS)]
    return jnp.concatenate(parts, axis=-1) * nw


def _ssd_kernel(xbc_ref, z_ref, dt_ref, tail0_ref, cw_ref, cb_ref, dtb_ref, a_ref, dsk_ref, nw_ref,
                o_ref, hout_ref, tail_scr, h_scr, y_scr):
    c = pl.program_id(1)
    L = SCHUNK

    @pl.when(c == 0)
    def _():
        tail_scr[...] = tail0_ref[0]
        h_scr[...] = jnp.zeros(h_scr.shape, F32)

    xbc = xbc_ref[...]
    tail = tail_scr[...]
    row8 = lax.broadcasted_iota(jnp.int32, (8, SCD), 0)
    conv = cb_ref[...] + xbc * cw_ref[SCONV - 1:SCONV, :]
    for sft in range(1, SCONV):
        sh = pltpu.roll(xbc, sft, 0)
        top = jnp.where(row8 < sft, pltpu.roll(tail, sft, 0), sh[0:8])
        sh = jnp.concatenate([top, sh[8:]], axis=0)
        conv = conv + sh * cw_ref[SCONV - 1 - sft:SCONV - sft, :]
    tail_scr[...] = xbc[L - 8:L]
    act = _silu(conv)
    xa = act[:, 0:SDIM]
    lane = lax.broadcasted_iota(jnp.int32, (L, 128), 1)
    dt_valid = (lane >= DT_LANE) & (lane < DT_LANE + SHEADS)
    dtt = jnp.where(dt_valid, _softplus(dt_ref[...] + dtb_ref[...]), 0.0)
    adt = dtt * a_ref[...]
    ri = lax.broadcasted_iota(jnp.int32, (L, L), 0)
    ci = lax.broadcasted_iota(jnp.int32, (L, L), 1)
    causal = ri >= ci
    cs = _split3_dot_left(causal.astype(BF16), adt)
    cs_t = cs.T
    total = cs[L - 1:L, :]
    for grp in range(SGROUPS):
        bm = act[:, SDIM + grp * SN:SDIM + (grp + 1) * SN]
        cm = act[:, SDIM + SGROUPS * SN + grp * SN:SDIM + SGROUPS * SN + (grp + 1) * SN]
        cb = _dot_nt(cm.astype(BF16), bm.astype(BF16))
        for hh in range(SHEADS // SGROUPS):
            h = grp * (SHEADS // SGROUPS) + hh
            ln = DT_LANE + h
            col = cs[:, ln:ln + 1]
            row = cs_t[ln:ln + 1, :]
            tot = total[:, ln:ln + 1]
            lmat = jnp.where(causal, jnp.exp(jnp.where(causal, col - row, 0.0)), 0.0)
            xh = xa[:, h * SP:(h + 1) * SP]
            xdt = (xh * dtt[:, ln:ln + 1]).astype(BF16)
            hprev = h_scr[h]
            y = _dot((cb * lmat).astype(BF16), xdt)
            y = y + _dot((cm * jnp.exp(col)).astype(BF16), hprev.astype(BF16))
            y_scr[:, h * SP:(h + 1) * SP] = y + dsk_ref[:, h * SP:(h + 1) * SP] * xh
            bdec = (bm * jnp.exp(tot - col)).T.astype(BF16)
            h_scr[h] = hprev * jnp.exp(tot) + _dot(bdec, xdt)
    o_ref[...] = _ssm_gate_norm(y_scr[...], z_ref[...], nw_ref[...]).astype(o_ref.dtype)

    @pl.when(c == pl.num_programs(1) - 1)
    def _():
        hout_ref[0] = h_scr[...]


def ssd_prompt(B, S, xbc, z, seg_a, tail0, p):
    nc = S // SCHUNK
    L = SCHUNK
    full = lambda shape: pl.BlockSpec(shape, lambda b, c: (0,) * len(shape))
    return pl.pallas_call(
        _ssd_kernel,
        grid=(B, nc),
        in_specs=[pl.BlockSpec((L, SCD), lambda b, c: (b * nc + c, 0)),
                  pl.BlockSpec((L, SDIM), lambda b, c: (b * nc + c, 0)),
                  pl.BlockSpec((L, 128), lambda b, c: (b * nc + c, DT_TILE)),
                  pl.BlockSpec((1, 8, SCD), lambda b, c: (b, 0, 0)),
                  full((SCONV, SCD)), full((1, SCD)), full((1, 128)), full((1, 128)), full((1, SDIM)),
                  full((1, SDIM))],
        out_specs=[pl.BlockSpec((L, SDIM), lambda b, c: (b * nc + c, 0)),
                   pl.BlockSpec((1, SHEADS, SN, SP), lambda b, c: (b, 0, 0, 0))],
        out_shape=[jax.ShapeDtypeStruct((B * S, SDIM), BF16), jax.ShapeDtypeStruct((B, SHEADS, SN, SP), F32)],
        scratch_shapes=[pltpu.VMEM((8, SCD), F32), pltpu.VMEM((SHEADS, SN, SP), F32), pltpu.VMEM((L, SDIM), F32)],
        compiler_params=_cparams("parallel", "arbitrary"),
        name="ssd_prompt",
    )(xbc, z, seg_a, tail0, p["conv_w"], p["conv_b"].reshape(1, -1), p["dtb128"], p["a128"], p["dskip512"],
      p["ssm_norm"].reshape(1, -1))


def _ssm_step_pre_kernel(xbc_ref, c0_ref, c1_ref, c2_ref, dt_ref, cw_ref, cb_ref, dtb_ref, a_ref,
                         act_ref, xdt_ref, dec_ref):
    conv = (cb_ref[...] + c0_ref[...] * cw_ref[0:1, :] + c1_ref[...] * cw_ref[1:2, :] + c2_ref[...] * cw_ref[2:3, :]
            + xbc_ref[...] * cw_ref[3:4, :])
    act = _silu(conv)
    act_ref[...] = act
    dtt = _softplus(dt_ref[...] + dtb_ref[...])
    dec_ref[...] = jnp.exp(dtt * a_ref[...])
    for h in range(SHEADS):
        xdt_ref[:, h * SP:(h + 1) * SP] = act[:, h * SP:(h + 1) * SP] * dtt[:, DT_LANE + h:DT_LANE + h + 1]


def ssm_step_pre(g, xbc, c0, c1, c2, seg_a, p):
    return pl.pallas_call(
        _ssm_step_pre_kernel,
        grid=g.grid,
        in_specs=[g.rows(SCD)] * 4 + [g.rows(128, DT_TILE), g.full((SCONV, SCD)), g.full((1, SCD)),
                                      g.full((1, 128)), g.full((1, 128))],
        out_specs=[g.rows(SCD), g.rows(SDIM), g.rows(128)],
        out_shape=[jax.ShapeDtypeStruct((g.M, SCD), F32), jax.ShapeDtypeStruct((g.M, SDIM), F32),
                   jax.ShapeDtypeStruct((g.M, 128), F32)],
        compiler_params=_cparams("parallel", "parallel"),
        name="ssm_step_pre",
    )(xbc, c0, c1, c2, seg_a, p["conv_w"], p["conv_b"].reshape(1, -1), p["dtb128"], p["a128"])


SSM_STEP_BT = 8


def _ssm_step_kernel(h0_ref, xb_ref, dec_ref, bm_ref, cm_ref, h1_ref, y_ref):
    bt = SSM_STEP_BT
    rp = lax.broadcasted_iota(jnp.int32, (SP, 128), 0)
    lp = lax.broadcasted_iota(jnp.int32, (SP, 128), 1)
    pick = [lp == rp, lp == rp + SP]
    hpg = SHEADS // SGROUPS
    for i in range(bt):
        for hp in range(SHEADS // 2):
            yrow = jnp.zeros((1, 128), F32)
            for e in range(2):
                h = hp * 2 + e
                grp = h // hpg
                bm = bm_ref[i:i + 1, grp * SN:(grp + 1) * SN]
                cm = cm_ref[i:i + 1, grp * SN:(grp + 1) * SN]
                h1 = h0_ref[i, h] * dec_ref[i, h] + xb_ref[i, h] * bm
                h1_ref[i, h] = h1
                ycol = jnp.sum(h1 * cm, axis=-1, keepdims=True)
                yrow = yrow + jnp.sum(jnp.where(pick[e], ycol, 0.0), axis=0, keepdims=True)
            y_ref[i:i + 1, hp * 128:(hp + 1) * 128] = yrow


def ssm_step(B, layer, h0, xb, dec, bm, cm):
    bt = SSM_STEP_BT
    st = pl.BlockSpec((bt, SHEADS, SP, SN), lambda i: (i, 0, 0, 0))
    return pl.pallas_call(
        _ssm_step_kernel,
        grid=(B // bt,),
        in_specs=[pl.BlockSpec((None, bt, SHEADS, SP, SN), lambda i: (layer, i, 0, 0, 0)), st,
                  pl.BlockSpec((bt, SHEADS, 1, SN), lambda i: (i, 0, 0, 0)),
                  pl.BlockSpec((bt, SGROUPS * SN), lambda i: (i, 0)), pl.BlockSpec((bt, SGROUPS * SN), lambda i: (i, 0))],
        out_specs=[st, pl.BlockSpec((bt, SDIM), lambda i: (i, 0))],
        out_shape=[jax.ShapeDtypeStruct((B, SHEADS, SP, SN), F32), jax.ShapeDtypeStruct((B, SDIM), F32)],
        compiler_params=_cparams("parallel"),
        name="ssm_step",
    )(h0, xb, dec, bm, cm)


def _ssm_step_post_kernel(y_ref, x_ref, z_ref, dsk_ref, nw_ref, o_ref):
    y = y_ref[...] + dsk_ref[...] * x_ref[...]
    o_ref[...] = _ssm_gate_norm(y, z_ref[...], nw_ref[...]).astype(o_ref.dtype)


def ssm_step_post(g, y, act, z, p):
    return pl.pallas_call(
        _ssm_step_post_kernel,
        grid=g.grid,
        in_specs=[g.rows(SDIM), g.rows(SDIM), g.rows(SDIM), g.full((1, SDIM)), g.full((1, SDIM))],
        out_specs=g.rows(SDIM),
        out_shape=jax.ShapeDtypeStruct((g.M, SDIM), g.act_dtype),
        compiler_params=_cparams("parallel", "parallel"),
        name="ssm_step_post",
    )(y, act, z, p["dskip512"], p["ssm_norm"].reshape(1, -1))


def _merge_kernel(a_ref, r_ref, s_ref, gate_ref, x_ref, ga_ref, wb_ref, wo_ref, o_ref, *, precise):
    acc = None
    for i, br in enumerate((a_ref, r_ref, s_ref)):
        t = _sigmoid(gate_ref[:, i * D_MODEL:(i + 1) * D_MODEL]) * _mm(br[...], wb_ref[i], precise)
        acc = t if acc is None else acc + t
    mix = _mm(acc, wo_ref[...], precise)
    o_ref[...] = x_ref[...] + ga_ref[0] * mix


def merge_branches(g, a_out, r_out, s_out, gate, x, ga, wb, wo):
    return pl.pallas_call(
        functools.partial(_merge_kernel, precise=g.precise),
        grid=g.grid,
        in_specs=[g.rows(512), g.rows(512), g.rows(512), g.rows(NBRANCH * D_MODEL), g.rows(D_MODEL),
                  g.mod_spec(D_MODEL), g.full((NBRANCH, 512, D_MODEL)), g.full((D_MODEL, D_MODEL))],
        out_specs=g.rows(D_MODEL),
        out_shape=jax.ShapeDtypeStruct((g.M, D_MODEL), F32),
        compiler_params=_cparams("parallel", "parallel"),
        name="merge_branches",
    )(a_out, r_out, s_out, gate, x, g.mod_array(ga), wb, wo)


def _ffn_kernel(x_ref, nw_ref, sc_ref, sh_ref, gf_ref, w1_ref, w3_ref, w2_ref, o_ref, h_scr, acc_scr, *, precise):
    j = pl.program_id(2)

    @pl.when(j == 0)
    def _():
        h_scr[...] = (_rms(x_ref[...]) * nw_ref[...] * (1.0 + sc_ref[0]) + sh_ref[0]).astype(h_scr.dtype)
        acc_scr[...] = jnp.zeros(acc_scr.shape, F32)

    h = h_scr[...]
    u = _silu(_mm(h, w1_ref[...], precise)) * _mm(h, w3_ref[...], precise)
    acc_scr[...] += _mm(u, w2_ref[...], precise)

    @pl.when(j == pl.num_programs(2) - 1)
    def _():
        o_ref[...] = x_ref[...] + gf_ref[0] * acc_scr[...]


def dense_ffn(g, x, nw, sc, sh, gf, w1, w3, w2, tf):
    dff = w1.shape[1]
    lift = lambda spec: pl.BlockSpec(spec.block_shape, lambda b, s, j, f=spec.index_map: f(b, s))
    return pl.pallas_call(
        functools.partial(_ffn_kernel, precise=g.precise),
        grid=g.grid + (dff // tf,),
        in_specs=[lift(g.rows(D_MODEL)), lift(g.full((1, D_MODEL))), lift(g.mod_spec(D_MODEL)),
                  lift(g.mod_spec(D_MODEL)), lift(g.mod_spec(D_MODEL)),
                  pl.BlockSpec((D_MODEL, tf), lambda b, s, j: (0, j)), pl.BlockSpec((D_MODEL, tf), lambda b, s, j: (0, j)),
                  pl.BlockSpec((tf, D_MODEL), lambda b, s, j: (j, 0))],
        out_specs=lift(g.rows(D_MODEL)),
        out_shape=jax.ShapeDtypeStruct((g.M, D_MODEL), F32),
        scratch_shapes=[pltpu.VMEM((g.tm, D_MODEL), g.act_dtype), pltpu.VMEM((g.tm, D_MODEL), F32)],
        compiler_params=_cparams("parallel", "parallel", "arbitrary"),
        name="dense_ffn",
    )(x, nw.reshape(1, -1), g.mod_array(sc), g.mod_array(sh), g.mod_array(gf), w1, w3, w2)


def _router_kernel(x_ref, nw_ref, sc_ref, sh_ref, wr_ref, rb_ref, h_ref, logit_ref):
    h = _rms(x_ref[...]) * nw_ref[...] * (1.0 + sc_ref[0]) + sh_ref[0]
    h_ref[...] = h.astype(BF16)
    logit_ref[...] = _dot3(h, wr_ref[...]) + rb_ref[...]


def moe_router(g, x, nw, sc, sh, wr, rb):
    return pl.pallas_call(
        _router_kernel,
        grid=g.grid,
        in_specs=[g.rows(D_MODEL), g.full((1, D_MODEL)), g.mod_spec(D_MODEL), g.mod_spec(D_MODEL),
                  g.full((D_MODEL, 128)), g.full((1, 128))],
        out_specs=[g.rows(D_MODEL), g.rows(128)],
        out_shape=[jax.ShapeDtypeStruct((g.M, D_MODEL), BF16), jax.ShapeDtypeStruct((g.M, 128), F32)],
        compiler_params=_cparams("parallel", "parallel"),
        name="moe_router",
    )(x, nw.reshape(1, -1), g.mod_array(sc), g.mod_array(sh), wr, rb)


def _expert_kernel(te_ref, tv_ref, h_ref, w1_ref, w3_ref, w2_ref, o_ref, acc_scr):
    i = pl.program_id(0)
    j = pl.program_id(1)

    @pl.when(j == 0)
    def _():
        acc_scr[...] = jnp.zeros(acc_scr.shape, F32)

    @pl.when(tv_ref[i] > 0)
    def _():
        h = h_ref[...]
        u = (_silu(_dot(h, w1_ref[...])) * _dot(h, w3_ref[...])).astype(BF16)
        acc_scr[...] += _dot(u, w2_ref[...])

    @pl.when(j == pl.num_programs(1) - 1)
    def _():
        o_ref[...] = acc_scr[...]


def expert_ffn(tile_expert, tile_valid, h_sorted, w1, w3, w2, tm, tf):
    rows = h_sorted.shape[0]
    dffe = w1.shape[2]
    grid_spec = pltpu.PrefetchScalarGridSpec(
        num_scalar_prefetch=2,
        grid=(rows // tm, dffe // tf),
        in_specs=[pl.BlockSpec((tm, D_MODEL), lambda i, j, te, tv: (i, 0)),
                  pl.BlockSpec((None, D_MODEL, tf), lambda i, j, te, tv: (te[i], 0, j)),
                  pl.BlockSpec((None, D_MODEL, tf), lambda i, j, te, tv: (te[i], 0, j)),
                  pl.BlockSpec((None, tf, D_MODEL), lambda i, j, te, tv: (te[i], j, 0))],
        out_specs=pl.BlockSpec((tm, D_MODEL), lambda i, j, te, tv: (i, 0)),
        scratch_shapes=[pltpu.VMEM((tm, D_MODEL), F32)],
    )
    return pl.pallas_call(
        _expert_kernel,
        grid_spec=grid_spec,
        out_shape=jax.ShapeDtypeStruct((rows, D_MODEL), F32),
        compiler_params=_cparams("parallel", "arbitrary"),
        name="expert_ffn",
    )(tile_expert, tile_valid, h_sorted, w1, w3, w2)


def _combine_kernel(x_ref, gf_ref, y0_ref, y1_ref, wt_ref, *rest):
    o_ref = rest[-1]
    wt = wt_ref[...]
    f = wt[:, 0:1] * y0_ref[...] + wt[:, 1:2] * y1_ref[...]
    x = x_ref[...] + gf_ref[0] * f
    if len(rest) == 2:
        x = _rms(x) * rest[0][...]
    o_ref[...] = x


def moe_combine(g, x, gf, y0, y1, wt, out_norm_w=None):
    extra_specs, extra = [], []
    if out_norm_w is not None:
        extra_specs, extra = [g.full((1, D_MODEL))], [out_norm_w.reshape(1, -1)]
    return pl.pallas_call(
        _combine_kernel,
        grid=g.grid,
        in_specs=[g.rows(D_MODEL), g.mod_spec(D_MODEL), g.rows(D_MODEL), g.rows(D_MODEL), g.rows(128)] + extra_specs,
        out_specs=g.rows(D_MODEL),
        out_shape=jax.ShapeDtypeStruct((g.M, D_MODEL), F32),
        compiler_params=_cparams("parallel", "parallel"),
        name="moe_combine",
    )(x, g.mod_array(gf), y0, y1, wt, *extra)


def moe_ffn(g, x, nw, sc, sh, gf, wr, rb, w1, w3, w2, out_norm_w=None):
    M = g.M
    h, logits = moe_router(g, x, nw, sc, sh, wr, rb)
    top_v, top_i = lax.top_k(logits[:, :NE], TOPK)
    top_w = jax.nn.softmax(top_v, axis=-1)
    tm = min(512, max(128, M // 4))
    tf = 1792
    flat_e = top_i.reshape(-1)
    onehot = (flat_e[:, None] == jnp.arange(NE)[None, :]).astype(jnp.int32)
    rank = jnp.take_along_axis(jnp.cumsum(onehot, axis=0) - onehot, flat_e[:, None], axis=1)[:, 0]
    counts = jnp.sum(onehot, axis=0)
    padded = ((counts + tm - 1) // tm) * tm
    starts = jnp.cumsum(padded) - padded
    pos = starts[flat_e] + rank
    n_rows = M * TOPK + NE * tm
    row_token = jnp.zeros((n_rows,), jnp.int32).at[pos].set(jnp.arange(M * TOPK, dtype=jnp.int32) // TOPK)
    tile_start = jnp.arange(n_rows // tm, dtype=jnp.int32) * tm
    ends = starts + padded
    tile_expert = jnp.minimum(jnp.sum((tile_start[:, None] >= ends[None, :]).astype(jnp.int32), axis=1), NE - 1)
    tile_valid = (tile_start < ends[NE - 1]).astype(jnp.int32)
    h_sorted = h.at[row_token].get(mode="promise_in_bounds")
    y_sorted = expert_ffn(tile_expert.astype(jnp.int32), tile_valid, h_sorted, w1, w3, w2, tm, tf)
    pos2 = pos.reshape(M, TOPK)
    y0 = y_sorted.at[pos2[:, 0]].get(mode="promise_in_bounds")
    y1 = y_sorted.at[pos2[:, 1]].get(mode="promise_in_bounds")
    wt = jnp.pad(top_w, ((0, 0), (0, 128 - TOPK)))
    return moe_combine(g, x, gf, y0, y1, wt, out_norm_w)


def _final_norm_kernel(x_ref, w_ref, o_ref):
    o_ref[...] = _rms(x_ref[...]) * w_ref[...]


def final_norm(g, x, w):
    return pl.pallas_call(
        _final_norm_kernel,
        grid=g.grid,
        in_specs=[g.rows(D_MODEL), g.full((1, D_MODEL))],
        out_specs=g.rows(D_MODEL),
        out_shape=jax.ShapeDtypeStruct((g.M, D_MODEL), F32),
        compiler_params=_cparams("parallel", "parallel"),
        name="final_norm",
    )(x, w.reshape(1, -1))


def _rot_half_cols(w):
    half = ROPE // 2
    return jnp.concatenate([-w[..., half:], w[..., :half]], axis=-1)


def _pack_layer(l, W):
    p = {}
    w_in = W["w_in"][l]
    o = 0
    q_c, kv_c, kr = w_in[:, 0:QL], w_in[:, QL:QL + KVL], w_in[:, QL + KVL:QL + KVL + ROPE]
    o = QL + KVL + ROPE
    rw = w_in[:, o:o + RWKV_IN]
    o += RWKV_IN
    z = w_in[:, o:o + SDIM]
    o += SDIM
    xbc = w_in[:, o:o + SCD]
    o += SCD
    dt = w_in[:, o:o + SHEADS]
    o += SHEADS
    gate = w_in[:, o:]
    seg_a = jnp.concatenate([q_c, kv_c, kr, dt, _rot_half_cols(kr),
                             jnp.zeros((D_MODEL, SEG_A - (QL + KVL + 2 * ROPE + SHEADS)), F32)], axis=1)
    mw = {}
    mw["w_in"] = jnp.concatenate([seg_a, rw, z, xbc, gate], axis=1)
    wq = W["mla_w_uq"][l].reshape(QL, H, NOPE + ROPE)
    pe = wq[:, :, NOPE:]
    mw["wuq"] = jnp.concatenate([wq[:, :, :NOPE].reshape(QL, H * NOPE), pe.reshape(QL, H * ROPE),
                                 _rot_half_cols(pe).reshape(QL, H * ROPE)], axis=1)
    mw["wuk"] = jnp.transpose(W["mla_w_uk"][l], (1, 2, 0))
    mw["wuv"] = jnp.transpose(W["mla_w_uv"][l], (1, 0, 2))
    mw["w2"], mw["a2"], mw["g2"] = W["rwkv_w2"][l], W["rwkv_a2"][l], W["rwkv_g2"][l]
    mw["w_branch"], mw["w_out"] = W["w_branch"][l], W["w_out"][l]
    if l % 2 == 0:
        mw["ffn"] = tuple(W[n][l // 2] for n in ("ffn_w1", "ffn_w3", "ffn_w2"))
    p["mw"] = mw
    p["mw16"] = jax.tree_util.tree_map(lambda a: a.astype(BF16), mw)
    p["q_norm"], p["kv_norm"] = W["mla_q_norm"][l], W["mla_kv_norm"][l]
    p["rwkv"] = dict(mu=W["rwkv_mu"][l], w0=W["rwkv_w0"][l], a0=W["rwkv_a0"][l], k_k=W["rwkv_k_k"][l],
                     k_a=W["rwkv_k_a"][l], ln_w=W["rwkv_ln_w"][l], ln_b=W["rwkv_ln_b"][l],
                     r_k=W["rwkv_r_k"][l].reshape(-1))
    lanes = jnp.arange(128)
    head_lane = (lanes >= DT_LANE) & (lanes < DT_LANE + SHEADS)
    idx = jnp.clip(lanes - DT_LANE, 0, SHEADS - 1)
    p["ssm"] = dict(conv_w=W["ssm_conv_w"][l], conv_b=W["ssm_conv_b"][l],
                    dtb128=jnp.where(head_lane, W["ssm_dt_bias"][l][idx], 0.0).reshape(1, 128),
                    a128=jnp.where(head_lane, -jnp.exp(W["ssm_a_log"][l][idx]), 0.0).reshape(1, 128),
                    dskip512=jnp.repeat(W["ssm_d"][l], SP).reshape(1, SDIM), ssm_norm=W["ssm_norm"][l])
    p["norm_attn"], p["norm_ffn"] = W["norm_attn"][l], W["norm_ffn"][l]
    p["w_ada"], p["b_ada"] = W["w_ada"][l], W["b_ada"][l]
    if l % 2 == 1:
        wr = jnp.pad(W["moe_router"][l // 2], ((0, 0), (0, 128 - NE)))
        p["router"] = (wr, jnp.pad(W["moe_router_b"][l // 2], (0, 128 - NE)).reshape(1, 128))
        p["moe"] = tuple(W[n][l // 2].astype(BF16) for n in ("moe_w1", "moe_w3", "moe_w2"))
    return p


def _rope_tables(pos):
    half = ROPE // 2
    freq = ROPE_THETA ** (-jnp.arange(half, dtype=F32) / half)
    ang = pos.astype(F32)[:, None] * freq[None, :]
    cos = jnp.concatenate([jnp.cos(ang)] * 2, axis=-1)
    sin = jnp.concatenate([jnp.sin(ang)] * 2, axis=-1)
    return cos, sin, jnp.tile(cos, (1, H)), jnp.tile(sin, (1, H))


def _trunk(x3, c, pos, paged, shift0, wkv0, conv0, ssm0, layers, norm_final):
    B, S, _ = x3.shape
    M = B * S
    g = _Group(B, S, 512)
    g_win = _Group(B, S, 256)
    x = x3.reshape(M, D_MODEL)
    tabs = _rope_tables(pos if S > 1 else jnp.broadcast_to(pos, (M,)))
    ri = jnp.arange(RDIM)
    ones_bd = ((ri[:, None] // RHD) == (ri[None, :] // RHD)).astype(BF16)
    outs = [[] for _ in range(6)]
    for l, p in enumerate(layers):
        mw = p["mw"] if g.precise else p["mw16"]
        ada = ada_matmul(c, p["w_ada"], p["b_ada"])
        sh_a, sc_a, g_a, sh_f, sc_f, g_f = jnp.split(ada, 6, axis=-1)
        seg_a, rw, z, xbc, gate = win_project(g_win, x, p["norm_attn"], sc_a, sh_a, mw["w_in"])

        qc, kc, ckv, kpe = mla_prep(g, seg_a, tabs, p["q_norm"], p["kv_norm"], mw["wuq"], mw["wuk"])
        if paged is None:
            a_out = mla_attention(B, S, qc, kc, mw["wuv"])
        else:
            cache_ckv, cache_kpe_t, page_table = paged
            a_out = paged_attention(l, page_table, qc, kc, cache_ckv, cache_kpe_t, mw["wuv"])

        rw3 = rw.reshape(B, S, RWKV_IN)
        tc_len = min(RHD, S)
        nch = S // tc_len
        tpb, hpt = SCAN_TPB, SCAN_HPT
        as_rows = lambda a: a.reshape(S, B * RDIM)
        if S > 1:
            r_t, w_t, k_t, v_t, kk_t, b_t, gate_r, vt = rwkv_prep(g, rw, shift0[l], p["rwkv"], mw, ones_bd)
        else:
            r_t, w_t, k_t, v_t, kk_t, b_t, gate_r = rwkv_prep(g, rw, shift0[l], p["rwkv"], mw, ones_bd)
            vt = v_t.reshape(nch, tc_len, B, tpb, hpt, RHD).transpose(0, 2, 3, 5, 4, 1)
            vt = jnp.pad(vt, ((0, 0),) * 5 + ((0, RHD - tc_len),)).reshape(nch, B * tpb, RHD, SCAN_W)
        s0 = wkv0[l].reshape(B, tpb, hpt, RHD, RHD).transpose(0, 1, 3, 2, 4).reshape(B * tpb, RHD, SCAN_W)
        yt, s1 = rwkv_scan(B, S, as_rows(r_t), as_rows(w_t), as_rows(k_t), as_rows(kk_t), as_rows(b_t), vt, s0,
                           g.precise)
        ysub = min(SCAN_YSUB, tc_len)
        y = yt.reshape(nch, B, tpb, RHD, SCAN_W // 128, hpt, SCAN_YSUB)[:, :, :, :, :tc_len // ysub, :, :ysub]
        y = y.transpose(1, 0, 4, 6, 2, 5, 3).reshape(M, RDIM)
        wkv1 = s1.reshape(B, tpb, RHD, hpt, RHD).transpose(0, 1, 3, 2, 4).reshape(B, tpb * hpt, RHD, RHD)
        r_out = rwkv_post(g, y, r_t, k_t, v_t, gate_r, p["rwkv"], ones_bd)
        shift1 = rw3[:, -1]

        ps = p["ssm"]
        if S > 1:
            tail0 = jnp.pad(conv0[l], ((0, 0), (8 - (SCONV - 1), 0), (0, 0)))
            s_out, h_t = ssd_prompt(B, S, xbc, z, seg_a, tail0, ps)
            ssm1 = jnp.swapaxes(h_t, 2, 3)
            conv1 = xbc.reshape(B, S, SCD)[:, S - (SCONV - 1):]
        else:
            c0, c1, c2 = conv0[l][:, 0], conv0[l][:, 1], conv0[l][:, 2]
            act, xdt, dec128 = ssm_step_pre(g, xbc, c0, c1, c2, seg_a, ps)
            xb = jnp.broadcast_to(xdt.reshape(B, SHEADS, SP, 1), (B, SHEADS, SP, SN))
            dec = jnp.broadcast_to(dec128[:, DT_LANE:DT_LANE + SHEADS].reshape(B, SHEADS, 1, 1), (B, SHEADS, 1, SN))
            ssm1, y_s = ssm_step(B, l, ssm0, xb, dec, act[:, SDIM:SDIM + SGROUPS * SN], act[:, SDIM + SGROUPS * SN:])
            s_out = ssm_step_post(g, y_s, act[:, :SDIM], z, ps)
            conv1 = jnp.concatenate([conv0[l][:, 1:], xbc[:, None, :]], axis=1)

        x = merge_branches(g, a_out, r_out, s_out, gate, x, g_a, mw["w_branch"], mw["w_out"])

        if l % 2 == 0:
            w1, w3, w2 = mw["ffn"]
            x = dense_ffn(g, x, p["norm_ffn"], sc_f, sh_f, g_f, w1, w3, w2, w1.shape[1] // 2)
        else:
            w1, w3, w2 = p["moe"]
            last = l == len(layers) - 1
            x = moe_ffn(g, x, p["norm_ffn"], sc_f, sh_f, g_f, *p["router"], w1, w3, w2,
                        out_norm_w=norm_final if last else None)
        for lst, v in zip(outs, (ckv.reshape(B, S, KVL), kpe.reshape(B, S, ROPE), shift1, wkv1, conv1, ssm1)):
            lst.append(v)
    if (len(layers) - 1) % 2 == 0:
        x = final_norm(g, x, norm_final)
    y = x.reshape(B, S, D_MODEL)
    return y, [jnp.stack(v) for v in outs]


def kernel(x_prompt, x_sample, cache_ckv, cache_kpe, state_rwkv_shift, state_rwkv_wkv, state_ssm_conv, state_ssm, page_table, c_prompt, c_sample, w_ada, b_ada, norm_attn, norm_ffn, norm_final, w_in, mla_q_norm, mla_w_uq, mla_kv_norm, mla_w_uk, mla_w_uv, rwkv_mu, rwkv_w0, rwkv_w2, rwkv_a0, rwkv_a2, rwkv_g2, rwkv_k_k, rwkv_k_a, rwkv_r_k, rwkv_ln_w, rwkv_ln_b, ssm_conv_w, ssm_conv_b, ssm_dt_bias, ssm_a_log, ssm_d, ssm_norm, w_branch, w_out, ffn_w1, ffn_w3, ffn_w2, moe_router, moe_router_b, moe_w1, moe_w3, moe_w2):
    W = dict(w_ada=w_ada, b_ada=b_ada, norm_attn=norm_attn, norm_ffn=norm_ffn, w_in=w_in, mla_q_norm=mla_q_norm,
             mla_w_uq=mla_w_uq, mla_kv_norm=mla_kv_norm, mla_w_uk=mla_w_uk, mla_w_uv=mla_w_uv, rwkv_mu=rwkv_mu,
             rwkv_w0=rwkv_w0, rwkv_w2=rwkv_w2, rwkv_a0=rwkv_a0, rwkv_a2=rwkv_a2, rwkv_g2=rwkv_g2, rwkv_k_k=rwkv_k_k,
             rwkv_k_a=rwkv_k_a, rwkv_r_k=rwkv_r_k, rwkv_ln_w=rwkv_ln_w, rwkv_ln_b=rwkv_ln_b, ssm_conv_w=ssm_conv_w,
             ssm_conv_b=ssm_conv_b, ssm_dt_bias=ssm_dt_bias, ssm_a_log=ssm_a_log, ssm_d=ssm_d, ssm_norm=ssm_norm,
             w_branch=w_branch, w_out=w_out, ffn_w1=ffn_w1, ffn_w3=ffn_w3, ffn_w2=ffn_w2, moe_router=moe_router,
             moe_router_b=moe_router_b, moe_w1=moe_w1, moe_w3=moe_w3, moe_w2=moe_w2)
    depth = w_in.shape[0]
    layers = [_pack_layer(l, W) for l in range(depth)]
    bp, sp, _ = x_prompt.shape
    bs, ss, _ = x_sample.shape
    dt = x_prompt.dtype
    y_prompt, (p_ckv, p_kpe, p_shift, p_wkv, p_conv, p_ssm) = _trunk(
        x_prompt, c_prompt, jnp.arange(sp, dtype=jnp.int32), None,
        jnp.zeros((depth, bp, RWKV_IN), dt), jnp.zeros((depth, bp, 8, RHD, RHD), dt),
        jnp.zeros((depth, bp, SCONV - 1, SCD), dt), jnp.zeros((depth, bp, SHEADS, SP, SN), dt), layers, norm_final)
    p_ckv = p_ckv.reshape(depth, bp * sp // PAGE, PAGE, KVL)
    p_kpe = p_kpe.reshape(depth, bp * sp // PAGE, PAGE, ROPE)
    past_len = page_table.shape[1] * PAGE
    pos_s = past_len + jnp.arange(ss, dtype=jnp.int32)
    y_sample, (s_ckv, s_kpe, s_shift, s_wkv, s_conv, s_ssm) = _trunk(
        x_sample, c_sample, pos_s, (cache_ckv, jnp.swapaxes(cache_kpe, 2, 3), page_table),
        state_rwkv_shift, state_rwkv_wkv, state_ssm_conv, state_ssm, layers, norm_final)
    return (y_prompt, y_sample, p_ckv, p_kpe, p_shift, p_wkv, p_conv, p_ssm,
            s_ckv, s_kpe, s_shift, s_wkv, s_conv, s_ssm)
```

```python
import functools

import jax
import jax.numpy as jnp
from jax import lax
from jax.experimental import pallas as pl
from jax.experimental.pallas import tpu as pltpu

F32 = jnp.float32
BF16 = jnp.bfloat16

D_MODEL = 1024
DEPTH = 2
PAGE = 128
H = 8
NOPE = 64
ROPE = 32
VD = 64
QL = 256
KVL = 256
ROPE_THETA = 10000.0
MLA_SCALE = (NOPE + ROPE) ** -0.5
LOG2E = 1.4426950408889634
QK = 384
RDIM = 512
RHD = 64
DECAY_LORA = 64
AAA_LORA = 64
GATE_LORA = 128
RWKV_IN = 3 * RDIM + DECAY_LORA + AAA_LORA + GATE_LORA
RWKV_LN_EPS = 64e-5
SHEADS = 8
SP = 64
SDIM = 512
SGROUPS = 2
SN = 128
SCONV = 4
SCHUNK = 128
SCD = SDIM + 2 * SGROUPS * SN
NBRANCH = 3
NE = 8
TOPK = 2
EPS = 1e-6
SEG_A = 768
DT_TILE = 4
DT_LANE = 32
SEG_WIDTHS = (SEG_A, RWKV_IN, SDIM, SCD, NBRANCH * D_MODEL)
W_IN_COLS = sum(SEG_WIDTHS)
VMEM_LIMIT = 56 * 1024 * 1024


def _cparams(*sem):
    return pltpu.CompilerParams(dimension_semantics=sem, vmem_limit_bytes=VMEM_LIMIT)


def _dot(a, b):
    return jnp.dot(a, b, preferred_element_type=F32)


def _dot_nt(a, b):
    return lax.dot_general(a, b, (((1,), (1,)), ((), ())), preferred_element_type=F32)


def _hi_lo(x):
    hi = x.astype(BF16)
    return hi, (x.astype(F32) - hi.astype(F32)).astype(BF16)


def _dot3(a, w, dot=_dot):
    a_hi, a_lo = _hi_lo(a)
    w_hi, w_lo = _hi_lo(w)
    return dot(a_hi, w_hi) + dot(a_lo, w_hi) + dot(a_hi, w_lo)


def _mm(a, w, precise):
    return _dot3(a, w) if precise else _dot(a.astype(BF16), w)


def _split_dot(x, w01):
    hi = x.astype(BF16)
    lo = (x - hi.astype(F32)).astype(BF16)
    return _dot(hi, w01) + _dot(lo, w01)


def _split3_dot(x, w01):
    hi = x.astype(BF16)
    r1 = x - hi.astype(F32)
    mid = r1.astype(BF16)
    lo = (r1 - mid.astype(F32)).astype(BF16)
    return _dot(hi, w01) + _dot(mid, w01) + _dot(lo, w01)


def _split3_dot_left(w01, x):
    hi = x.astype(BF16)
    r1 = x - hi.astype(F32)
    mid = r1.astype(BF16)
    lo = (r1 - mid.astype(F32)).astype(BF16)
    return _dot(w01, hi) + _dot(w01, mid) + _dot(w01, lo)


def _sigmoid(x):
    return 1.0 / (1.0 + jnp.exp(-x))


def _silu(x):
    return x * _sigmoid(x)


def _softplus(x):
    return jnp.maximum(x, 0.0) + jnp.log(1.0 + jnp.exp(-jnp.abs(x)))


def _rms(x):
    return x * lax.rsqrt(jnp.mean(x * x, axis=-1, keepdims=True) + EPS)


class _Group:
    def __init__(self, B, S, tm):
        self.B, self.S, self.M = B, S, B * S
        self.precise = S == 1
        self.act_dtype = F32 if self.precise else BF16
        if S == 1:
            self.tm = min(tm, self.M)
            self.grid = (1, self.M // self.tm)
        else:
            self.tm = min(tm, S)
            self.grid = (B, S // self.tm)
        self.ns = self.grid[1]

    def rows(self, width, colblock=0):
        ns = self.ns
        return pl.BlockSpec((self.tm, width), lambda b, s: (b * ns + s, colblock))

    def full(self, shape):
        nd = len(shape)
        return pl.BlockSpec(shape, lambda b, s: (0,) * nd)

    def mod_array(self, m):
        return m.reshape(1, self.M, -1) if self.S == 1 else m.reshape(self.B, 1, -1)

    def mod_spec(self, width):
        if self.S == 1:
            return pl.BlockSpec((1, self.tm, width), lambda b, s: (0, s, 0))
        return pl.BlockSpec((1, 1, width), lambda b, s: (b, 0, 0))

    def pos_spec(self, width):
        if self.S == 1:
            return self.rows(width)
        return pl.BlockSpec((self.tm, width), lambda b, s: (s, 0))

    def tmaj_shape(self, width):
        return (self.M, width) if self.S == 1 else (self.S, self.B * width)

    def tmaj_spec(self, width):
        if self.S == 1:
            return self.rows(width)
        return pl.BlockSpec((self.tm, width), lambda b, s: (s, b))


def _ada_kernel(c_ref, w_ref, b_ref, o_ref):
    c = c_ref[...]
    o_ref[...] = _dot3(_silu(c), w_ref[...]) + b_ref[...]


def ada_matmul(c, w, b):
    m, k = c.shape
    n = w.shape[1]
    tn = 1024
    return pl.pallas_call(
        _ada_kernel,
        grid=(n // tn,),
        in_specs=[pl.BlockSpec((m, k), lambda j: (0, 0)), pl.BlockSpec((k, tn), lambda j: (0, j)),
                  pl.BlockSpec((1, tn), lambda j: (0, j))],
        out_specs=pl.BlockSpec((m, tn), lambda j: (0, j)),
        out_shape=jax.ShapeDtypeStruct((m, n), F32),
        compiler_params=_cparams("arbitrary"),
        name="ada_matmul",
    )(c, w, b.reshape(1, n))


def _win_kernel(x_ref, nw_ref, sc_ref, sh_ref, w_ref, *o_refs):
    h = (_rms(x_ref[...]) * nw_ref[...] * (1.0 + sc_ref[0]) + sh_ref[0]).astype(BF16)
    off = 0
    for o in o_refs:
        n = o.shape[-1]
        for c in range(0, n, 256):
            o[:, c:c + 256] = _dot(h, w_ref[:, off + c:off + c + 256])
        off += n


def _win_cols_kernel(x_ref, nw_ref, sc_ref, sh_ref, w_ref, o_ref):
    h = _rms(x_ref[...]) * nw_ref[...] * (1.0 + sc_ref[0]) + sh_ref[0]
    o_ref[...] = _dot3(h, w_ref[...])


def win_project_precise(g, x, nw, sc, sh, w_packed):
    tn = 512
    out = pl.pallas_call(
        _win_cols_kernel,
        grid=(W_IN_COLS // tn,),
        in_specs=[pl.BlockSpec((g.M, D_MODEL), lambda j: (0, 0)), pl.BlockSpec((1, D_MODEL), lambda j: (0, 0)),
                  pl.BlockSpec((1, g.M, D_MODEL), lambda j: (0, 0, 0)),
                  pl.BlockSpec((1, g.M, D_MODEL), lambda j: (0, 0, 0)),
                  pl.BlockSpec((D_MODEL, tn), lambda j: (0, j))],
        out_specs=pl.BlockSpec((g.M, tn), lambda j: (0, j)),
        out_shape=jax.ShapeDtypeStruct((g.M, W_IN_COLS), F32),
        compiler_params=_cparams("parallel"),
        name="win_project_precise",
    )(x, nw.reshape(1, -1), g.mod_array(sc), g.mod_array(sh), w_packed)
    offs = [0]
    for w in SEG_WIDTHS:
        offs.append(offs[-1] + w)
    return [out[:, offs[i]:offs[i + 1]] for i in range(len(SEG_WIDTHS))]


def win_project(g, x, nw, sc, sh, w_packed):
    if g.precise:
        return win_project_precise(g, x, nw, sc, sh, w_packed)
    return pl.pallas_call(
        _win_kernel,
        grid=g.grid,
        in_specs=[g.rows(D_MODEL), g.full((1, D_MODEL)), g.mod_spec(D_MODEL), g.mod_spec(D_MODEL),
                  g.full((D_MODEL, W_IN_COLS))],
        out_specs=[g.rows(w) for w in SEG_WIDTHS],
        out_shape=[jax.ShapeDtypeStruct((g.M, w), F32) for w in SEG_WIDTHS],
        compiler_params=_cparams("parallel", "parallel"),
        name="win_project",
    )(x, nw.reshape(1, -1), g.mod_array(sc), g.mod_array(sh), w_packed)


def _mla_prep_kernel(a_ref, cos_ref, sin_ref, cos8_ref, sin8_ref, qn_ref, kvn_ref, wuq_ref, wuk_ref,
                     qc_ref, kc_ref, ckv_ref, kpe_ref, *, precise):
    a = a_ref[...]
    tm = a.shape[0]
    odt = qc_ref.dtype
    qn = _rms(a[:, 0:QL]) * qn_ref[...]
    qa = _mm(qn, wuq_ref[...], precise)
    q_rope = qa[:, 512:768] * cos8_ref[...] + qa[:, 768:1024] * sin8_ref[...]
    zpad = jnp.zeros((tm, QK - KVL - ROPE), odt)
    qscale = MLA_SCALE if precise else MLA_SCALE * LOG2E
    for h in range(H):
        q_abs = _mm(qa[:, h * NOPE:(h + 1) * NOPE], wuk_ref[h], precise) * qscale
        qc_ref[h, :, 0:KVL] = q_abs.astype(odt)
        qc_ref[h, :, KVL:KVL + ROPE] = (q_rope[:, h * ROPE:(h + 1) * ROPE] * qscale).astype(odt)
        qc_ref[h, :, KVL + ROPE:QK] = zpad
    ckv = _rms(a[:, QL:QL + KVL]) * kvn_ref[...]
    kpe = a[:, 512:544] * cos_ref[...] + a[:, 552:584] * sin_ref[...]
    ckv_ref[...] = ckv
    kpe_ref[...] = kpe
    kc_ref[:, 0:KVL] = ckv.astype(odt)
    kc_ref[:, KVL:KVL + ROPE] = kpe.astype(odt)
    kc_ref[:, KVL + ROPE:QK] = zpad


def mla_prep(g, seg_a, tabs, qn, kvn, wuq, wuk):
    cos, sin, cos8, sin8 = tabs
    ns = g.ns
    return pl.pallas_call(
        functools.partial(_mla_prep_kernel, precise=g.precise),
        grid=g.grid,
        in_specs=[g.rows(SEG_A), g.pos_spec(ROPE), g.pos_spec(ROPE), g.pos_spec(H * ROPE), g.pos_spec(H * ROPE),
                  g.full((1, QL)), g.full((1, KVL)), g.full((QL, 1024)), g.full((H, NOPE, KVL))],
        out_specs=[pl.BlockSpec((H, g.tm, QK), lambda b, s: (0, b * ns + s, 0)), g.rows(QK), g.rows(KVL),
                   g.rows(ROPE)],
        out_shape=[jax.ShapeDtypeStruct((H, g.M, QK), g.act_dtype), jax.ShapeDtypeStruct((g.M, QK), g.act_dtype),
                   jax.ShapeDtypeStruct((g.M, KVL), F32), jax.ShapeDtypeStruct((g.M, ROPE), F32)],
        compiler_params=_cparams("parallel", "parallel"),
        name="mla_prep",
    )(seg_a, cos, sin, cos8, sin8, qn.reshape(1, -1), kvn.reshape(1, -1), wuq, wuk)


NEG = -1e30


def _attn_kernel(q_ref, k_ref, wuv_ref, o_ref, m_scr, l_scr, a_scr, acc_scr, s_scr, p_scr, *, tq, tk):
    qi = pl.program_id(1)
    q = q_ref[...].reshape(H * tq, QK)
    m_scr[...] = jnp.full(m_scr.shape, NEG, F32)
    l_scr[...] = jnp.zeros(l_scr.shape, F32)
    acc_scr[...] = jnp.zeros(acc_scr.shape, F32)
    reps = tk // 128
    wide = lambda a: jnp.concatenate([a] * reps, axis=-1)

    def keys(j):
        return k_ref[pl.ds(pl.multiple_of(j * tk, tk), tk), :]

    def scores(j, slot):
        s_scr[slot] = _dot_nt(q, keys(j))

    def softmax_pv(j, slot, masked):
        if masked:
            visible = (j * tk + lax.broadcasted_iota(jnp.int32, (tq, tk), 1)
                       <= qi * tq + lax.broadcasted_iota(jnp.int32, (tq, tk), 0))
        for h in range(H):
            rs = pl.ds(h * tq, tq)
            s = s_scr[slot, rs, :]
            if masked:
                s = jnp.where(visible, s, NEG)
            m_prev = m_scr[rs, :]
            m_new = jnp.maximum(m_prev, jnp.max(s, axis=-1, keepdims=True))
            alpha = jnp.exp2(m_prev - m_new)
            p = jnp.exp2(s - wide(m_new))
            l_scr[rs, :] = alpha * l_scr[rs, :] + jnp.sum(p, axis=-1, keepdims=True)
            m_scr[rs, :] = m_new
            a_scr[rs, :] = alpha
            p_scr[slot, rs, :] = p.astype(BF16)
        alpha = a_scr[...]
        acc_scr[...] = (acc_scr[...] * jnp.concatenate([alpha] * (KVL // 128), axis=-1)
                        + _dot(p_scr[slot], keys(j)[:, 0:KVL]))

    n_full = (qi * tq) // tk

    scores(0, 0)

    def body(i, carry):
        j = 2 * i
        scores(j + 1, 1)
        softmax_pv(j, 0, False)
        scores(j + 2, 0)
        softmax_pv(j + 1, 1, False)
        return carry

    lax.fori_loop(0, n_full // 2, body, 0)

    @pl.when(n_full % 2 == 0)
    def _():
        softmax_pv(n_full, 0, True)

    @pl.when(n_full % 2 == 1)
    def _():
        scores(n_full, 1)
        softmax_pv(n_full - 1, 0, False)
        softmax_pv(n_full, 1, True)
    inv_l = 1.0 / l_scr[...]
    o = acc_scr[...] * jnp.concatenate([inv_l] * (KVL // 128), axis=-1)
    for h in range(H):
        o_ref[:, h * VD:(h + 1) * VD] = _dot(o[h * tq:(h + 1) * tq].astype(BF16), wuv_ref[h]).astype(o_ref.dtype)


def mla_attention(B, S, qc, kc, wuv):
    tq = min(128, S)
    tk = min(256, S)
    nq = S // tq
    rows = H * tq
    return pl.pallas_call(
        functools.partial(_attn_kernel, tq=tq, tk=tk),
        grid=(B, nq),
        in_specs=[pl.BlockSpec((H, tq, QK), lambda b, i: (0, b * nq + i, 0)),
                  pl.BlockSpec((S, QK), lambda b, i: (b, 0)),
                  pl.BlockSpec((H, KVL, VD), lambda b, i: (0, 0, 0))],
        out_specs=pl.BlockSpec((tq, H * VD), lambda b, i: (b * nq + i, 0)),
        out_shape=jax.ShapeDtypeStruct((B * S, H * VD), BF16),
        scratch_shapes=[pltpu.VMEM((rows, 128), F32), pltpu.VMEM((rows, 128), F32), pltpu.VMEM((rows, 128), F32),
                        pltpu.VMEM((rows, KVL), F32), pltpu.VMEM((2, rows, tk), F32),
                        pltpu.VMEM((2, rows, tk), BF16)],
        compiler_params=_cparams("parallel", "parallel"),
        name="mla_attention",
    )(qc, kc, wuv)


PAGES_PER_STEP = 16


def _paged_kernel(pt_ref, q_ref, knew_ref, wuv_ref, *rest):
    pp = PAGES_PER_STEP
    ckv_refs, kpe_refs = rest[:pp], rest[pp:2 * pp]
    o_ref, m_scr, l_scr, acc_scr = rest[2 * pp:]
    j = pl.program_id(1)
    q = q_ref[...]

    @pl.when(j == 0)
    def _():
        m_scr[...] = jnp.full(m_scr.shape, NEG, F32)
        l_scr[...] = jnp.zeros(l_scr.shape, F32)
        acc_scr[...] = jnp.zeros(acc_scr.shape, F32)

    def stack_hi_lo(x):
        hi = x.astype(BF16).astype(F32)
        return jnp.concatenate([hi, x - hi], axis=0).astype(BF16)

    qa2 = stack_hi_lo(q[:, 0:KVL])
    qp2 = stack_hi_lo(q[:, KVL:KVL + ROPE])

    def chain(refs_c, refs_p):
        ckv_hi, ckv_lo = _hi_lo(jnp.concatenate([r[...] for r in refs_c], axis=0))
        kpe_hi, kpe_lo = _hi_lo(jnp.concatenate([r[...] for r in refs_p], axis=1))
        s2 = _dot_nt(qa2, ckv_hi) + _dot(qp2, kpe_hi)
        s = s2[0:H] + s2[H:2 * H] + _dot_nt(qa2[0:H], ckv_lo) + _dot(qp2[0:H], kpe_lo)
        m = jnp.max(s, axis=-1, keepdims=True)
        p = jnp.exp(s - m)
        p2 = stack_hi_lo(p)
        pv2 = _dot(p2, jnp.concatenate([ckv_hi, ckv_lo], axis=1))
        pv = pv2[0:H, 0:KVL] + pv2[H:2 * H, 0:KVL] + pv2[0:H, KVL:2 * KVL]
        return m, jnp.sum(p, axis=-1, keepdims=True), pv

    half = pp // 2
    parts = [chain(ckv_refs[i * half:(i + 1) * half], kpe_refs[i * half:(i + 1) * half]) for i in range(2)]
    m_prev = m_scr[...]
    m_new = jnp.maximum(m_prev, jnp.maximum(parts[0][0], parts[1][0]))
    alpha = jnp.exp(m_prev - m_new)
    l = alpha * l_scr[...]
    acc = alpha * acc_scr[...]
    for m_i, l_i, pv_i in parts:
        w_i = jnp.exp(m_i - m_new)
        l = l + w_i * l_i
        acc = acc + w_i * pv_i
    l_scr[...] = l
    acc_scr[...] = acc
    m_scr[...] = m_new

    @pl.when(j == pl.num_programs(1) - 1)
    def _():
        kn = knew_ref[0]
        s_new = jnp.sum(q * kn, axis=-1, keepdims=True)
        m_prev = m_scr[...]
        m_new = jnp.maximum(m_prev, s_new)
        alpha = jnp.exp(m_prev - m_new)
        p_new = jnp.exp(s_new - m_new)
        l = alpha * l_scr[...] + p_new
        acc = alpha * acc_scr[...] + p_new * kn[:, 0:KVL]
        o = acc / l
        for h in range(H):
            o_ref[0, :, h * VD:(h + 1) * VD] = _dot3(o[h:h + 1], wuv_ref[h])


def paged_attention(layer, page_table, qc, kc_new, cache_ckv, cache_kpe_t, wuv):
    B, n_pages = page_table.shape
    pp = PAGES_PER_STEP
    nsteps = n_pages // pp

    def page_spec(i, shape):
        return pl.BlockSpec((None, None) + shape, lambda b, j, pt: (layer, pt[b, j * pp + i], 0, 0))

    grid_spec = pltpu.PrefetchScalarGridSpec(
        num_scalar_prefetch=1,
        grid=(B, nsteps),
        in_specs=[pl.BlockSpec((None, H, QK), lambda b, j, pt: (b, 0, 0)),
                  pl.BlockSpec((1, 1, QK), lambda b, j, pt: (b, 0, 0)),
                  pl.BlockSpec((H, KVL, VD), lambda b, j, pt: (0, 0, 0))]
        + [page_spec(i, (PAGE, KVL)) for i in range(pp)] + [page_spec(i, (ROPE, PAGE)) for i in range(pp)],
        out_specs=pl.BlockSpec((1, 1, H * VD), lambda b, j, pt: (b, 0, 0)),
        scratch_shapes=[pltpu.VMEM((H, 1), F32), pltpu.VMEM((H, 1), F32), pltpu.VMEM((H, KVL), F32)],
    )
    out = pl.pallas_call(
        _paged_kernel,
        grid_spec=grid_spec,
        out_shape=jax.ShapeDtypeStruct((B, 1, H * VD), F32),
        compiler_params=_cparams("parallel", "arbitrary"),
        name="paged_attention",
    )(page_table, jnp.transpose(qc, (1, 0, 2)), kc_new.reshape(B, 1, QK), wuv, *([cache_ckv] * pp),
      *([cache_kpe_t] * pp))
    return out.reshape(B, H * VD)


def _rwkv_prep_kernel(rw_ref, prev_ref, mu_ref, w0_ref, a0_ref, kk_ref_, ka_ref, w2_ref, a2_ref, g2_ref, ones_ref,
                      r_o, w_o, k_o, v_o, kk_o, b_o, g_o, *rest, precise, seq):
    rw = rw_ref[...]
    if seq:
        vt_o, carry = rest
        tm = rw.shape[0]

        @pl.when(pl.program_id(1) == 0)
        def _():
            carry[...] = prev_ref[0]

        row8 = lax.broadcasted_iota(jnp.int32, (8, RWKV_IN), 0)
        shifted = pltpu.roll(rw, 1, 0)
        top = jnp.where(row8 == 0, pltpu.roll(carry[...], 1, 0), shifted[0:8])
        prev = jnp.concatenate([top, shifted[8:]], axis=0)
        carry[...] = rw[tm - 8:tm]
    else:
        prev = prev_ref[...]
    xs = rw + (prev - rw) * mu_ref[...]
    r = xs[:, 0:RDIM]
    k = xs[:, RDIM:2 * RDIM]
    v = xs[:, 2 * RDIM:3 * RDIM]
    o = 3 * RDIM
    wl = xs[:, o:o + DECAY_LORA]
    al = xs[:, o + DECAY_LORA:o + DECAY_LORA + AAA_LORA]
    gl = xs[:, o + DECAY_LORA + AAA_LORA:RWKV_IN]
    w = -_softplus(-(w0_ref[...] + _mm(jnp.tanh(wl), w2_ref[...], precise))) - 0.5
    a = _sigmoid(a0_ref[...] + _mm(al, a2_ref[...], precise))
    kk = k * kk_ref_[...]
    kk = kk * lax.rsqrt(jnp.maximum(_split_dot(kk * kk, ones_ref[...]), 1e-24))
    r_o[...] = r
    w_o[...] = jnp.exp(-jnp.exp(w))
    k_o[...] = k * (1.0 + (a - 1.0) * ka_ref[...])
    v_o[...] = v
    kk_o[...] = kk
    b_o[...] = kk * a
    g_o[...] = _mm(_sigmoid(gl), g2_ref[...], precise)
    if seq:
        ri = lax.broadcasted_iota(jnp.int32, (SCAN_W, SCAN_W), 0)
        ci = lax.broadcasted_iota(jnp.int32, (SCAN_W, SCAN_W), 1)
        same_head = (ri // RHD) == (ci // RHD)
        eye_rep = (lax.broadcasted_iota(jnp.int32, (RHD, SCAN_W), 0)
                   == lax.broadcasted_iota(jnp.int32, (RHD, SCAN_W), 1) % RHD).astype(BF16)
        for c in range(tm // RHD):
            for t in range(SCAN_TPB):
                v_c = v[c * RHD:(c + 1) * RHD, t * SCAN_W:(t + 1) * SCAN_W].astype(BF16)
                blockdiag = jnp.where(same_head, jnp.concatenate([v_c] * SCAN_HPT, axis=0), jnp.zeros((), BF16))
                vt_o[c, t] = _dot_nt(eye_rep, blockdiag).astype(vt_o.dtype)


def rwkv_prep(g, rw, prev, p, mw, ones_bd):
    vec = lambda a: a.reshape(1, -1)
    tshape = jax.ShapeDtypeStruct(g.tmaj_shape(RDIM), F32)
    seq = g.S > 1
    out_specs = [g.tmaj_spec(RDIM)] * 6 + [g.rows(RDIM)]
    out_shape = [tshape] * 6 + [jax.ShapeDtypeStruct((g.M, RDIM), F32)]
    scratch = []
    if seq:
        assert g.tm % RHD == 0
        cpt = g.tm // RHD
        prev = jnp.pad(prev[:, None, :], ((0, 0), (7, 0), (0, 0)))
        prev_spec = pl.BlockSpec((1, 8, RWKV_IN), lambda b, s: (b, 0, 0))
        out_specs.append(pl.BlockSpec((cpt, SCAN_TPB, RHD, SCAN_W), lambda b, s: (s, b, 0, 0)))
        out_shape.append(jax.ShapeDtypeStruct((g.S // RHD, g.B * SCAN_TPB, RHD, SCAN_W), BF16))
        scratch = [pltpu.VMEM((8, RWKV_IN), F32)]
    else:
        prev_spec = g.rows(RWKV_IN)
    return pl.pallas_call(
        functools.partial(_rwkv_prep_kernel, precise=g.precise, seq=seq),
        grid=g.grid,
        in_specs=[g.rows(RWKV_IN), prev_spec, g.full((1, RWKV_IN)), g.full((1, RDIM)), g.full((1, RDIM)),
                  g.full((1, RDIM)), g.full((1, RDIM)), g.full((DECAY_LORA, RDIM)), g.full((AAA_LORA, RDIM)),
                  g.full((GATE_LORA, RDIM)), g.full((RDIM, RDIM))],
        out_specs=out_specs,
        out_shape=out_shape,
        scratch_shapes=scratch,
        compiler_params=_cparams("parallel", "arbitrary"),
        name="rwkv_prep",
    )(rw, prev, vec(p["mu"]), vec(p["w0"]), vec(p["a0"]), vec(p["k_k"]), vec(p["k_a"]), mw["w2"], mw["a2"], mw["g2"],
      ones_bd)


SCAN_NB = 16
SCAN_HPT = 4
SCAN_W = SCAN_HPT * RHD
SCAN_TPB = RDIM // SCAN_W
SCAN_GB = 8
SCAN_YSUB = 128 // SCAN_HPT


def _scan_kernel(r_ref, w_ref, k_ref, kk_ref, b_ref, vt_ref, s0_ref, yt_ref, sout_ref, s_scr, *, tc_len, ng, precise):
    tc = pl.program_id(1)

    def pick(x, w01):
        return _split_dot(x, w01) if precise else _dot(x.astype(BF16), w01)

    @pl.when(tc == 0)
    def _():
        s_scr[...] = s0_ref[...]

    ri = lax.broadcasted_iota(jnp.int32, (SCAN_W, SCAN_W), 0)
    ci = lax.broadcasted_iota(jnp.int32, (SCAN_W, SCAN_W), 1)
    same_head = (ri // RHD) == (ci // RHD)
    ones_bd = same_head.astype(BF16)
    yri = lax.broadcasted_iota(jnp.int32, (SCAN_W, 128), 0)
    yci = lax.broadcasted_iota(jnp.int32, (SCAN_W, 128), 1)
    yt_ref[...] = jnp.zeros(yt_ref.shape, F32)
    gb = min(SCAN_GB, ng)
    sub = min(SCAN_YSUB, tc_len)

    def step(tt, carry, part):
        tg = part * sub + tt
        e_t = (same_head & ((ri % RHD) == tg)).astype(BF16)
        y_t = ((yri // RHD) * SCAN_YSUB + tt == yci).astype(BF16)
        ylanes = pl.ds(part * 128, 128)

        def rows(ref, g0):
            return jnp.stack([jnp.broadcast_to(ref[pl.ds(tg, 1), pl.ds((g0 + i) * SCAN_W, SCAN_W)], (RHD, SCAN_W))
                              for i in range(gb)])

        def issue(g0):
            sa = pick((s_scr[g0:g0 + gb] * rows(kk_ref, g0)).reshape(gb * RHD, SCAN_W), ones_bd)
            vcol = pick(vt_ref[0, g0:g0 + gb].reshape(gb * RHD, SCAN_W), e_t)
            return sa.reshape(gb, RHD, SCAN_W), vcol.reshape(gb, RHD, SCAN_W)

        pend = issue(0)
        y_pend = None
        for g0 in range(0, ng, gb):
            nxt = issue(g0 + gb) if g0 + gb < ng else None
            sa, vcol = pend
            s = s_scr[g0:g0 + gb] * rows(w_ref, g0) - sa * rows(b_ref, g0) + vcol * rows(k_ref, g0)
            s_scr[g0:g0 + gb] = s
            y = pick((s * rows(r_ref, g0)).reshape(gb * RHD, SCAN_W), y_t).reshape(gb, RHD, 128)
            if y_pend is not None:
                gp, yp = y_pend
                yt_ref[0, gp:gp + gb, :, ylanes] += yp
            y_pend = (g0, y)
            pend = nxt
        gp, yp = y_pend
        yt_ref[0, gp:gp + gb, :, ylanes] += yp
        return carry

    unroll = 4 if sub % 4 == 0 else 1
    for part in range(tc_len // sub):
        lax.fori_loop(0, sub, functools.partial(step, part=part), 0, unroll=unroll)

    @pl.when(tc == pl.num_programs(1) - 1)
    def _():
        sout_ref[...] = s_scr[...]


def rwkv_scan(B, S, r, w, k, kk, b, vt, s0, precise):
    tc_len = min(RHD, S)
    nb = min(SCAN_NB, B)
    ng = nb * SCAN_TPB
    nchunks = S // tc_len
    row_spec = pl.BlockSpec((tc_len, nb * RDIM), lambda bg, c: (c, bg))
    st_spec = pl.BlockSpec((ng, RHD, SCAN_W), lambda bg, c: (bg, 0, 0))
    ch_spec = pl.BlockSpec((1, ng, RHD, SCAN_W), lambda bg, c: (c, bg, 0, 0))
    return pl.pallas_call(
        functools.partial(_scan_kernel, tc_len=tc_len, ng=ng, precise=precise),
        grid=(B // nb, nchunks),
        in_specs=[row_spec] * 5 + [ch_spec, st_spec],
        out_specs=[ch_spec, st_spec],
        out_shape=[jax.ShapeDtypeStruct((nchunks, B * SCAN_TPB, RHD, SCAN_W), F32),
                   jax.ShapeDtypeStruct((B * SCAN_TPB, RHD, SCAN_W), F32)],
        scratch_shapes=[pltpu.VMEM((ng, RHD, SCAN_W), F32)],
        compiler_params=_cparams("parallel", "arbitrary"),
        name="rwkv_scan",
    )(r, w, k, kk, b, vt, s0)


def _rwkv_post_kernel(y_ref, r_ref, k_ref, v_ref, g_ref, lnw_ref, lnb_ref, rk_ref, ones_ref, o_ref):
    ones = ones_ref[...]
    y = y_ref[...]
    mu = _split_dot(y, ones) * (1.0 / RHD)
    yc = y - mu
    var = _split_dot(yc * yc, ones) * (1.0 / RHD)
    yn = yc * lax.rsqrt(var + RWKV_LN_EPS) * lnw_ref[...] + lnb_ref[...]
    v = v_ref[...]
    bonus = _split_dot(r_ref[...] * k_ref[...] * rk_ref[...], ones)
    o_ref[...] = ((yn + bonus * v) * g_ref[...]).astype(o_ref.dtype)


def rwkv_post(g, y, r, k, v, gate, p, ones_bd):
    vec = lambda a: a.reshape(1, -1)
    return pl.pallas_call(
        _rwkv_post_kernel,
        grid=g.grid,
        in_specs=[g.rows(RDIM), g.tmaj_spec(RDIM), g.tmaj_spec(RDIM), g.tmaj_spec(RDIM), g.rows(RDIM),
                  g.full((1, RDIM)), g.full((1, RDIM)), g.full((1, RDIM)), g.full((RDIM, RDIM))],
        out_specs=g.rows(RDIM),
        out_shape=jax.ShapeDtypeStruct((g.M, RDIM), g.act_dtype),
        compiler_params=_cparams("parallel", "parallel"),
        name="rwkv_post",
    )(y, r, k, v, gate, vec(p["ln_w"]), vec(p["ln_b"]), vec(p["r_k"]), ones_bd)


def _ssm_gate_norm(y, z, nw):
    y = y * _silu(z)
    gw = SDIM // SGROUPS
    parts = [_rms(y[:, i * gw:(i + 1) * gw]) for i in range(SGROUPS)]
    return jnp.concatenate(parts, axis=-1) * nw


def _ssd_kernel(xbc_ref, z_ref, dt_ref, tail0_ref, cw_ref, cb_ref, dtb_ref, a_ref, dsk_ref, nw_ref,
                o_ref, hout_ref, tail_scr, h_scr, y_scr):
    c = pl.program_id(1)
    L = SCHUNK

    @pl.when(c == 0)
    def _():
        tail_scr[...] = tail0_ref[0]
        h_scr[...] = jnp.zeros(h_scr.shape, F32)

    xbc = xbc_ref[...]
    tail = tail_scr[...]
    row8 = lax.broadcasted_iota(jnp.int32, (8, SCD), 0)
    conv = cb_ref[...] + xbc * cw_ref[SCONV - 1:SCONV, :]
    for sft in range(1, SCONV):
        sh = pltpu.roll(xbc, sft, 0)
        top = jnp.where(row8 < sft, pltpu.roll(tail, sft, 0), sh[0:8])
        sh = jnp.concatenate([top, sh[8:]], axis=0)
        conv = conv + sh * cw_ref[SCONV - 1 - sft:SCONV - sft, :]
    tail_scr[...] = xbc[L - 8:L]
    act = _silu(conv)
    xa = act[:, 0:SDIM]
    lane = lax.broadcasted_iota(jnp.int32, (L, 128), 1)
    dt_valid = (lane >= DT_LANE) & (lane < DT_LANE + SHEADS)
    dtt = jnp.where(dt_valid, _softplus(dt_ref[...] + dtb_ref[...]), 0.0)
    adt = dtt * a_ref[...]
    ri = lax.broadcasted_iota(jnp.int32, (L, L), 0)
    ci = lax.broadcasted_iota(jnp.int32, (L, L), 1)
    causal = ri >= ci
    cs = _split3_dot_left(causal.astype(BF16), adt)
    cs_t = cs.T
    sel_r = lax.broadcasted_iota(jnp.int32, (128, SHEADS * 128), 0)
    sel_c = lax.broadcasted_iota(jnp.int32, (128, SHEADS * 128), 1)
    col_all = _split3_dot(cs, (sel_r == DT_LANE + sel_c // 128).astype(BF16))
    sel_r = lax.broadcasted_iota(jnp.int32, (128, SDIM), 0)
    sel_c = lax.broadcasted_iota(jnp.int32, (128, SDIM), 1)
    xdt_all = xa * _split3_dot(dtt, (sel_r == DT_LANE + sel_c // SP).astype(BF16))
    for grp in range(SGROUPS):
        bm = act[:, SDIM + grp * SN:SDIM + (grp + 1) * SN]
        cm = act[:, SDIM + SGROUPS * SN + grp * SN:SDIM + SGROUPS * SN + (grp + 1) * SN]
        cb = _dot_nt(cm.astype(BF16), bm.astype(BF16))
        for hh in range(SHEADS // SGROUPS):
            h = grp * (SHEADS // SGROUPS) + hh
            ln = DT_LANE + h
            col = col_all[:, h * 128:(h + 1) * 128]
            row = cs_t[ln:ln + 1, :]
            tot = col[L - 1:L, :]
            lmat = jnp.exp(jnp.where(causal, col - row, NEG))
            xh = xa[:, h * SP:(h + 1) * SP]
            xdt = xdt_all[:, h * SP:(h + 1) * SP].astype(BF16)
            hprev = h_scr[h]
            y = _dot((cb * lmat).astype(BF16), xdt)
            y = y + _dot((cm * jnp.exp(col)).astype(BF16), hprev.astype(BF16))
            y_scr[:, h * SP:(h + 1) * SP] = y + dsk_ref[:, h * SP:(h + 1) * SP] * xh
            bdec = (bm * jnp.exp(tot - col)).T.astype(BF16)
            h_scr[h] = hprev * jnp.exp(tot[:, 0:SP]) + _dot(bdec, xdt)
    o_ref[...] = _ssm_gate_norm(y_scr[...], z_ref[...], nw_ref[...]).astype(o_ref.dtype)

    @pl.when(c == pl.num_programs(1) - 1)
    def _():
        hout_ref[0] = h_scr[...]


def ssd_prompt(B, S, xbc, z, seg_a, tail0, p):
    nc = S // SCHUNK
    L = SCHUNK
    full = lambda shape: pl.BlockSpec(shape, lambda b, c: (0,) * len(shape))
    return pl.pallas_call(
        _ssd_kernel,
        grid=(B, nc),
        in_specs=[pl.BlockSpec((L, SCD), lambda b, c: (b * nc + c, 0)),
                  pl.BlockSpec((L, SDIM), lambda b, c: (b * nc + c, 0)),
                  pl.BlockSpec((L, 128), lambda b, c: (b * nc + c, DT_TILE)),
                  pl.BlockSpec((1, 8, SCD), lambda b, c: (b, 0, 0)),
                  full((SCONV, SCD)), full((1, SCD)), full((1, 128)), full((1, 128)), full((1, SDIM)),
                  full((1, SDIM))],
        out_specs=[pl.BlockSpec((L, SDIM), lambda b, c: (b * nc + c, 0)),
                   pl.BlockSpec((1, SHEADS, SN, SP), lambda b, c: (b, 0, 0, 0))],
        out_shape=[jax.ShapeDtypeStruct((B * S, SDIM), BF16), jax.ShapeDtypeStruct((B, SHEADS, SN, SP), F32)],
        scratch_shapes=[pltpu.VMEM((8, SCD), F32), pltpu.VMEM((SHEADS, SN, SP), F32), pltpu.VMEM((L, SDIM), F32)],
        compiler_params=_cparams("parallel", "arbitrary"),
        name="ssd_prompt",
    )(xbc, z, seg_a, tail0, p["conv_w"], p["conv_b"].reshape(1, -1), p["dtb128"], p["a128"], p["dskip512"],
      p["ssm_norm"].reshape(1, -1))


def _ssm_step_pre_kernel(xbc_ref, c0_ref, c1_ref, c2_ref, dt_ref, cw_ref, cb_ref, dtb_ref, a_ref,
                         act_ref, xdt_ref, dec_ref):
    conv = (cb_ref[...] + c0_ref[...] * cw_ref[0:1, :] + c1_ref[...] * cw_ref[1:2, :] + c2_ref[...] * cw_ref[2:3, :]
            + xbc_ref[...] * cw_ref[3:4, :])
    act = _silu(conv)
    act_ref[...] = act
    dtt = _softplus(dt_ref[...] + dtb_ref[...])
    dec_ref[...] = jnp.exp(dtt * a_ref[...])
    for h in range(SHEADS):
        xdt_ref[:, h * SP:(h + 1) * SP] = act[:, h * SP:(h + 1) * SP] * dtt[:, DT_LANE + h:DT_LANE + h + 1]


def ssm_step_pre(g, xbc, c0, c1, c2, seg_a, p):
    return pl.pallas_call(
        _ssm_step_pre_kernel,
        grid=g.grid,
        in_specs=[g.rows(SCD)] * 4 + [g.rows(128, DT_TILE), g.full((SCONV, SCD)), g.full((1, SCD)),
                                      g.full((1, 128)), g.full((1, 128))],
        out_specs=[g.rows(SCD), g.rows(SDIM), g.rows(128)],
        out_shape=[jax.ShapeDtypeStruct((g.M, SCD), F32), jax.ShapeDtypeStruct((g.M, SDIM), F32),
                   jax.ShapeDtypeStruct((g.M, 128), F32)],
        compiler_params=_cparams("parallel", "parallel"),
        name="ssm_step_pre",
    )(xbc, c0, c1, c2, seg_a, p["conv_w"], p["conv_b"].reshape(1, -1), p["dtb128"], p["a128"])


SSM_STEP_BT = 8


def _ssm_step_kernel(h0_ref, xb_ref, dec_ref, bm_ref, cm_ref, h1_ref, y_ref):
    bt = SSM_STEP_BT
    rp = lax.broadcasted_iota(jnp.int32, (SP, 128), 0)
    lp = lax.broadcasted_iota(jnp.int32, (SP, 128), 1)
    pick = [lp == rp, lp == rp + SP]
    hpg = SHEADS // SGROUPS
    for i in range(bt):
        for hp in range(SHEADS // 2):
            yrow = jnp.zeros((1, 128), F32)
            for e in range(2):
                h = hp * 2 + e
                grp = h // hpg
                bm = bm_ref[i:i + 1, grp * SN:(grp + 1) * SN]
                cm = cm_ref[i:i + 1, grp * SN:(grp + 1) * SN]
                h1 = h0_ref[i, h] * dec_ref[i, h] + xb_ref[i, h] * bm
                h1_ref[i, h] = h1
                ycol = jnp.sum(h1 * cm, axis=-1, keepdims=True)
                yrow = yrow + jnp.sum(jnp.where(pick[e], ycol, 0.0), axis=0, keepdims=True)
            y_ref[i:i + 1, hp * 128:(hp + 1) * 128] = yrow


def ssm_step(B, layer, h0, xb, dec, bm, cm):
    bt = SSM_STEP_BT
    st = pl.BlockSpec((bt, SHEADS, SP, SN), lambda i: (i, 0, 0, 0))
    return pl.pallas_call(
        _ssm_step_kernel,
        grid=(B // bt,),
        in_specs=[pl.BlockSpec((None, bt, SHEADS, SP, SN), lambda i: (layer, i, 0, 0, 0)), st,
                  pl.BlockSpec((bt, SHEADS, 1, SN), lambda i: (i, 0, 0, 0)),
                  pl.BlockSpec((bt, SGROUPS * SN), lambda i: (i, 0)), pl.BlockSpec((bt, SGROUPS * SN), lambda i: (i, 0))],
        out_specs=[st, pl.BlockSpec((bt, SDIM), lambda i: (i, 0))],
        out_shape=[jax.ShapeDtypeStruct((B, SHEADS, SP, SN), F32), jax.ShapeDtypeStruct((B, SDIM), F32)],
        compiler_params=_cparams("parallel"),
        name="ssm_step",
    )(h0, xb, dec, bm, cm)


def _ssm_step_post_kernel(y_ref, x_ref, z_ref, dsk_ref, nw_ref, o_ref):
    y = y_ref[...] + dsk_ref[...] * x_ref[...]
    o_ref[...] = _ssm_gate_norm(y, z_ref[...], nw_ref[...]).astype(o_ref.dtype)


def ssm_step_post(g, y, act, z, p):
    return pl.pallas_call(
        _ssm_step_post_kernel,
        grid=g.grid,
        in_specs=[g.rows(SDIM), g.rows(SDIM), g.rows(SDIM), g.full((1, SDIM)), g.full((1, SDIM))],
        out_specs=g.rows(SDIM),
        out_shape=jax.ShapeDtypeStruct((g.M, SDIM), g.act_dtype),
        compiler_params=_cparams("parallel", "parallel"),
        name="ssm_step_post",
    )(y, act, z, p["dskip512"], p["ssm_norm"].reshape(1, -1))


def _merge_kernel(a_ref, r_ref, s_ref, gate_ref, x_ref, ga_ref, wb_ref, wo_ref, o_ref, *, precise):
    acc = None
    for i, br in enumerate((a_ref, r_ref, s_ref)):
        t = _sigmoid(gate_ref[:, i * D_MODEL:(i + 1) * D_MODEL]) * _mm(br[...], wb_ref[i], precise)
        acc = t if acc is None else acc + t
    mix = _mm(acc, wo_ref[...], precise)
    o_ref[...] = x_ref[...] + ga_ref[0] * mix


def merge_branches(g, a_out, r_out, s_out, gate, x, ga, wb, wo):
    return pl.pallas_call(
        functools.partial(_merge_kernel, precise=g.precise),
        grid=g.grid,
        in_specs=[g.rows(512), g.rows(512), g.rows(512), g.rows(NBRANCH * D_MODEL), g.rows(D_MODEL),
                  g.mod_spec(D_MODEL), g.full((NBRANCH, 512, D_MODEL)), g.full((D_MODEL, D_MODEL))],
        out_specs=g.rows(D_MODEL),
        out_shape=jax.ShapeDtypeStruct((g.M, D_MODEL), F32),
        compiler_params=_cparams("parallel", "parallel"),
        name="merge_branches",
    )(a_out, r_out, s_out, gate, x, g.mod_array(ga), wb, wo)


def _ffn_kernel(x_ref, nw_ref, sc_ref, sh_ref, gf_ref, w1_ref, w3_ref, w2_ref, o_ref, h_scr, acc_scr, *, precise):
    j = pl.program_id(2)

    @pl.when(j == 0)
    def _():
        h_scr[...] = (_rms(x_ref[...]) * nw_ref[...] * (1.0 + sc_ref[0]) + sh_ref[0]).astype(h_scr.dtype)
        acc_scr[...] = jnp.zeros(acc_scr.shape, F32)

    h = h_scr[...]
    u = _silu(_mm(h, w1_ref[...], precise)) * _mm(h, w3_ref[...], precise)
    acc_scr[...] += _mm(u, w2_ref[...], precise)

    @pl.when(j == pl.num_programs(2) - 1)
    def _():
        o_ref[...] = x_ref[...] + gf_ref[0] * acc_scr[...]


def dense_ffn(g, x, nw, sc, sh, gf, w1, w3, w2, tf):
    dff = w1.shape[1]
    lift = lambda spec: pl.BlockSpec(spec.block_shape, lambda b, s, j, f=spec.index_map: f(b, s))
    return pl.pallas_call(
        functools.partial(_ffn_kernel, precise=g.precise),
        grid=g.grid + (dff // tf,),
        in_specs=[lift(g.rows(D_MODEL)), lift(g.full((1, D_MODEL))), lift(g.mod_spec(D_MODEL)),
                  lift(g.mod_spec(D_MODEL)), lift(g.mod_spec(D_MODEL)),
                  pl.BlockSpec((D_MODEL, tf), lambda b, s, j: (0, j)), pl.BlockSpec((D_MODEL, tf), lambda b, s, j: (0, j)),
                  pl.BlockSpec((tf, D_MODEL), lambda b, s, j: (j, 0))],
        out_specs=lift(g.rows(D_MODEL)),
        out_shape=jax.ShapeDtypeStruct((g.M, D_MODEL), F32),
        scratch_shapes=[pltpu.VMEM((g.tm, D_MODEL), g.act_dtype), pltpu.VMEM((g.tm, D_MODEL), F32)],
        compiler_params=_cparams("parallel", "parallel", "arbitrary"),
        name="dense_ffn",
    )(x, nw.reshape(1, -1), g.mod_array(sc), g.mod_array(sh), g.mod_array(gf), w1, w3, w2)


def _router_kernel(x_ref, nw_ref, sc_ref, sh_ref, wr_ref, rb_ref, h_ref, logit_ref):
    h = _rms(x_ref[...]) * nw_ref[...] * (1.0 + sc_ref[0]) + sh_ref[0]
    h_ref[...] = h.astype(BF16)
    logit_ref[...] = _dot3(h, wr_ref[...]) + rb_ref[...]


def moe_router(g, x, nw, sc, sh, wr, rb):
    return pl.pallas_call(
        _router_kernel,
        grid=g.grid,
        in_specs=[g.rows(D_MODEL), g.full((1, D_MODEL)), g.mod_spec(D_MODEL), g.mod_spec(D_MODEL),
                  g.full((D_MODEL, 128)), g.full((1, 128))],
        out_specs=[g.rows(D_MODEL), g.rows(128)],
        out_shape=[jax.ShapeDtypeStruct((g.M, D_MODEL), BF16), jax.ShapeDtypeStruct((g.M, 128), F32)],
        compiler_params=_cparams("parallel", "parallel"),
        name="moe_router",
    )(x, nw.reshape(1, -1), g.mod_array(sc), g.mod_array(sh), wr, rb)


def _expert_kernel(te_ref, tv_ref, h_ref, w1_ref, w3_ref, w2_ref, o_ref, acc_scr):
    i = pl.program_id(0)
    j = pl.program_id(1)

    @pl.when(j == 0)
    def _():
        acc_scr[...] = jnp.zeros(acc_scr.shape, F32)

    @pl.when(tv_ref[i] > 0)
    def _():
        h = h_ref[...]
        u = (_silu(_dot(h, w1_ref[...])) * _dot(h, w3_ref[...])).astype(BF16)
        acc_scr[...] += _dot(u, w2_ref[...])

    @pl.when(j == pl.num_programs(1) - 1)
    def _():
        o_ref[...] = acc_scr[...]


def expert_ffn(tile_expert, tile_valid, h_sorted, w1, w3, w2, tm, tf):
    rows = h_sorted.shape[0]
    dffe = w1.shape[2]
    grid_spec = pltpu.PrefetchScalarGridSpec(
        num_scalar_prefetch=2,
        grid=(rows // tm, dffe // tf),
        in_specs=[pl.BlockSpec((tm, D_MODEL), lambda i, j, te, tv: (i, 0)),
                  pl.BlockSpec((None, D_MODEL, tf), lambda i, j, te, tv: (te[i], 0, j)),
                  pl.BlockSpec((None, D_MODEL, tf), lambda i, j, te, tv: (te[i], 0, j)),
                  pl.BlockSpec((None, tf, D_MODEL), lambda i, j, te, tv: (te[i], j, 0))],
        out_specs=pl.BlockSpec((tm, D_MODEL), lambda i, j, te, tv: (i, 0)),
        scratch_shapes=[pltpu.VMEM((tm, D_MODEL), F32)],
    )
    return pl.pallas_call(
        _expert_kernel,
        grid_spec=grid_spec,
        out_shape=jax.ShapeDtypeStruct((rows, D_MODEL), F32),
        compiler_params=_cparams("parallel", "arbitrary"),
        name="expert_ffn",
    )(tile_expert, tile_valid, h_sorted, w1, w3, w2)


def _combine_kernel(x_ref, gf_ref, y0_ref, y1_ref, wt_ref, *rest):
    o_ref = rest[-1]
    wt = wt_ref[...]
    f = wt[:, 0:1] * y0_ref[...] + wt[:, 1:2] * y1_ref[...]
    x = x_ref[...] + gf_ref[0] * f
    if len(rest) == 2:
        x = _rms(x) * rest[0][...]
    o_ref[...] = x


def moe_combine(g, x, gf, y0, y1, wt, out_norm_w=None):
    extra_specs, extra = [], []
    if out_norm_w is not None:
        extra_specs, extra = [g.full((1, D_MODEL))], [out_norm_w.reshape(1, -1)]
    return pl.pallas_call(
        _combine_kernel,
        grid=g.grid,
        in_specs=[g.rows(D_MODEL), g.mod_spec(D_MODEL), g.rows(D_MODEL), g.rows(D_MODEL), g.rows(128)] + extra_specs,
        out_specs=g.rows(D_MODEL),
        out_shape=jax.ShapeDtypeStruct((g.M, D_MODEL), F32),
        compiler_params=_cparams("parallel", "parallel"),
        name="moe_combine",
    )(x, g.mod_array(gf), y0, y1, wt, *extra)


def moe_ffn(g, x, nw, sc, sh, gf, wr, rb, w1, w3, w2, out_norm_w=None):
    M = g.M
    h, logits = moe_router(g, x, nw, sc, sh, wr, rb)
    top_v, top_i = lax.top_k(logits[:, :NE], TOPK)
    top_w = jax.nn.softmax(top_v, axis=-1)
    tm = min(512, max(128, M // 4))
    tf = 1792
    flat_e = top_i.reshape(-1)
    onehot = (flat_e[:, None] == jnp.arange(NE)[None, :]).astype(jnp.int32)
    rank = jnp.take_along_axis(jnp.cumsum(onehot, axis=0) - onehot, flat_e[:, None], axis=1)[:, 0]
    counts = jnp.sum(onehot, axis=0)
    padded = ((counts + tm - 1) // tm) * tm
    starts = jnp.cumsum(padded) - padded
    pos = starts[flat_e] + rank
    n_rows = M * TOPK + NE * tm
    row_token = jnp.zeros((n_rows,), jnp.int32).at[pos].set(jnp.arange(M * TOPK, dtype=jnp.int32) // TOPK)
    tile_start = jnp.arange(n_rows // tm, dtype=jnp.int32) * tm
    ends = starts + padded
    tile_expert = jnp.minimum(jnp.sum((tile_start[:, None] >= ends[None, :]).astype(jnp.int32), axis=1), NE - 1)
    tile_valid = (tile_start < ends[NE - 1]).astype(jnp.int32)
    h_sorted = h.at[row_token].get(mode="promise_in_bounds")
    y_sorted = expert_ffn(tile_expert.astype(jnp.int32), tile_valid, h_sorted, w1, w3, w2, tm, tf)
    pos2 = pos.reshape(M, TOPK)
    y0 = y_sorted.at[pos2[:, 0]].get(mode="promise_in_bounds")
    y1 = y_sorted.at[pos2[:, 1]].get(mode="promise_in_bounds")
    wt = jnp.pad(top_w, ((0, 0), (0, 128 - TOPK)))
    return moe_combine(g, x, gf, y0, y1, wt, out_norm_w)


def _final_norm_kernel(x_ref, w_ref, o_ref):
    o_ref[...] = _rms(x_ref[...]) * w_ref[...]


def final_norm(g, x, w):
    return pl.pallas_call(
        _final_norm_kernel,
        grid=g.grid,
        in_specs=[g.rows(D_MODEL), g.full((1, D_MODEL))],
        out_specs=g.rows(D_MODEL),
        out_shape=jax.ShapeDtypeStruct((g.M, D_MODEL), F32),
        compiler_params=_cparams("parallel", "parallel"),
        name="final_norm",
    )(x, w.reshape(1, -1))


def _rot_half_cols(w):
    half = ROPE // 2
    return jnp.concatenate([-w[..., half:], w[..., :half]], axis=-1)


def _pack_layer(l, W):
    p = {}
    w_in = W["w_in"][l]
    o = 0
    q_c, kv_c, kr = w_in[:, 0:QL], w_in[:, QL:QL + KVL], w_in[:, QL + KVL:QL + KVL + ROPE]
    o = QL + KVL + ROPE
    rw = w_in[:, o:o + RWKV_IN]
    o += RWKV_IN
    z = w_in[:, o:o + SDIM]
    o += SDIM
    xbc = w_in[:, o:o + SCD]
    o += SCD
    dt = w_in[:, o:o + SHEADS]
    o += SHEADS
    gate = w_in[:, o:]
    seg_a = jnp.concatenate([q_c, kv_c, kr, dt, _rot_half_cols(kr),
                             jnp.zeros((D_MODEL, SEG_A - (QL + KVL + 2 * ROPE + SHEADS)), F32)], axis=1)
    mw = {}
    mw["w_in"] = jnp.concatenate([seg_a, rw, z, xbc, gate], axis=1)
    wq = W["mla_w_uq"][l].reshape(QL, H, NOPE + ROPE)
    pe = wq[:, :, NOPE:]
    mw["wuq"] = jnp.concatenate([wq[:, :, :NOPE].reshape(QL, H * NOPE), pe.reshape(QL, H * ROPE),
                                 _rot_half_cols(pe).reshape(QL, H * ROPE)], axis=1)
    mw["wuk"] = jnp.transpose(W["mla_w_uk"][l], (1, 2, 0))
    mw["wuv"] = jnp.transpose(W["mla_w_uv"][l], (1, 0, 2))
    mw["w2"], mw["a2"], mw["g2"] = W["rwkv_w2"][l], W["rwkv_a2"][l], W["rwkv_g2"][l]
    mw["w_branch"], mw["w_out"] = W["w_branch"][l], W["w_out"][l]
    if l % 2 == 0:
        mw["ffn"] = tuple(W[n][l // 2] for n in ("ffn_w1", "ffn_w3", "ffn_w2"))
    p["mw"] = mw
    p["mw16"] = jax.tree_util.tree_map(lambda a: a.astype(BF16), mw)
    p["q_norm"], p["kv_norm"] = W["mla_q_norm"][l], W["mla_kv_norm"][l]
    p["rwkv"] = dict(mu=W["rwkv_mu"][l], w0=W["rwkv_w0"][l], a0=W["rwkv_a0"][l], k_k=W["rwkv_k_k"][l],
                     k_a=W["rwkv_k_a"][l], ln_w=W["rwkv_ln_w"][l], ln_b=W["rwkv_ln_b"][l],
                     r_k=W["rwkv_r_k"][l].reshape(-1))
    lanes = jnp.arange(128)
    head_lane = (lanes >= DT_LANE) & (lanes < DT_LANE + SHEADS)
    idx = jnp.clip(lanes - DT_LANE, 0, SHEADS - 1)
    p["ssm"] = dict(conv_w=W["ssm_conv_w"][l], conv_b=W["ssm_conv_b"][l],
                    dtb128=jnp.where(head_lane, W["ssm_dt_bias"][l][idx], 0.0).reshape(1, 128),
                    a128=jnp.where(head_lane, -jnp.exp(W["ssm_a_log"][l][idx]), 0.0).reshape(1, 128),
                    dskip512=jnp.repeat(W["ssm_d"][l], SP).reshape(1, SDIM), ssm_norm=W["ssm_norm"][l])
    p["norm_attn"], p["norm_ffn"] = W["norm_attn"][l], W["norm_ffn"][l]
    p["w_ada"], p["b_ada"] = W["w_ada"][l], W["b_ada"][l]
    if l % 2 == 1:
        wr = jnp.pad(W["moe_router"][l // 2], ((0, 0), (0, 128 - NE)))
        p["router"] = (wr, jnp.pad(W["moe_router_b"][l // 2], (0, 128 - NE)).reshape(1, 128))
        p["moe"] = tuple(W[n][l // 2].astype(BF16) for n in ("moe_w1", "moe_w3", "moe_w2"))
    return p


def _rope_tables(pos):
    half = ROPE // 2
    freq = ROPE_THETA ** (-jnp.arange(half, dtype=F32) / half)
    ang = pos.astype(F32)[:, None] * freq[None, :]
    cos = jnp.concatenate([jnp.cos(ang)] * 2, axis=-1)
    sin = jnp.concatenate([jnp.sin(ang)] * 2, axis=-1)
    return cos, sin, jnp.tile(cos, (1, H)), jnp.tile(sin, (1, H))


def _trunk(x3, c, pos, paged, shift0, wkv0, conv0, ssm0, layers, norm_final):
    B, S, _ = x3.shape
    M = B * S
    g = _Group(B, S, 512)
    g_win = _Group(B, S, 256)
    x = x3.reshape(M, D_MODEL)
    tabs = _rope_tables(pos if S > 1 else jnp.broadcast_to(pos, (M,)))
    ri = jnp.arange(RDIM)
    ones_bd = ((ri[:, None] // RHD) == (ri[None, :] // RHD)).astype(BF16)
    outs = [[] for _ in range(6)]
    for l, p in enumerate(layers):
        mw = p["mw"] if g.precise else p["mw16"]
        ada = ada_matmul(c, p["w_ada"], p["b_ada"])
        sh_a, sc_a, g_a, sh_f, sc_f, g_f = jnp.split(ada, 6, axis=-1)
        seg_a, rw, z, xbc, gate = win_project(g_win, x, p["norm_attn"], sc_a, sh_a, mw["w_in"])

        qc, kc, ckv, kpe = mla_prep(g, seg_a, tabs, p["q_norm"], p["kv_norm"], mw["wuq"], mw["wuk"])
        if paged is None:
            a_out = mla_attention(B, S, qc, kc, mw["wuv"])
        else:
            cache_ckv, cache_kpe_t, page_table = paged
            a_out = paged_attention(l, page_table, qc, kc, cache_ckv, cache_kpe_t, mw["wuv"])

        rw3 = rw.reshape(B, S, RWKV_IN)
        tc_len = min(RHD, S)
        nch = S // tc_len
        tpb, hpt = SCAN_TPB, SCAN_HPT
        as_rows = lambda a: a.reshape(S, B * RDIM)
        if S > 1:
            r_t, w_t, k_t, v_t, kk_t, b_t, gate_r, vt = rwkv_prep(g, rw, shift0[l], p["rwkv"], mw, ones_bd)
        else:
            r_t, w_t, k_t, v_t, kk_t, b_t, gate_r = rwkv_prep(g, rw, shift0[l], p["rwkv"], mw, ones_bd)
            vt = v_t.reshape(nch, tc_len, B, tpb, hpt, RHD).transpose(0, 2, 3, 5, 4, 1)
            vt = jnp.pad(vt, ((0, 0),) * 5 + ((0, RHD - tc_len),)).reshape(nch, B * tpb, RHD, SCAN_W)
        s0 = wkv0[l].reshape(B, tpb, hpt, RHD, RHD).transpose(0, 1, 3, 2, 4).reshape(B * tpb, RHD, SCAN_W)
        yt, s1 = rwkv_scan(B, S, as_rows(r_t), as_rows(w_t), as_rows(k_t), as_rows(kk_t), as_rows(b_t), vt, s0,
                           g.precise)
        ysub = min(SCAN_YSUB, tc_len)
        y = yt.reshape(nch, B, tpb, RHD, SCAN_W // 128, hpt, SCAN_YSUB)[:, :, :, :, :tc_len // ysub, :, :ysub]
        y = y.transpose(1, 0, 4, 6, 2, 5, 3).reshape(M, RDIM)
        wkv1 = s1.reshape(B, tpb, RHD, hpt, RHD).transpose(0, 1, 3, 2, 4).reshape(B, tpb * hpt, RHD, RHD)
        r_out = rwkv_post(g, y, r_t, k_t, v_t, gate_r, p["rwkv"], ones_bd)
        shift1 = rw3[:, -1]

        ps = p["ssm"]
        if S > 1:
            tail0 = jnp.pad(conv0[l], ((0, 0), (8 - (SCONV - 1), 0), (0, 0)))
            s_out, h_t = ssd_prompt(B, S, xbc, z, seg_a, tail0, ps)
            ssm1 = jnp.swapaxes(h_t, 2, 3)
            conv1 = xbc.reshape(B, S, SCD)[:, S - (SCONV - 1):]
        else:
            c0, c1, c2 = conv0[l][:, 0], conv0[l][:, 1], conv0[l][:, 2]
            act, xdt, dec128 = ssm_step_pre(g, xbc, c0, c1, c2, seg_a, ps)
            xb = jnp.broadcast_to(xdt.reshape(B, SHEADS, SP, 1), (B, SHEADS, SP, SN))
            dec = jnp.broadcast_to(dec128[:, DT_LANE:DT_LANE + SHEADS].reshape(B, SHEADS, 1, 1), (B, SHEADS, 1, SN))
            ssm1, y_s = ssm_step(B, l, ssm0, xb, dec, act[:, SDIM:SDIM + SGROUPS * SN], act[:, SDIM + SGROUPS * SN:])
            s_out = ssm_step_post(g, y_s, act[:, :SDIM], z, ps)
            conv1 = jnp.concatenate([conv0[l][:, 1:], xbc[:, None, :]], axis=1)

        x = merge_branches(g, a_out, r_out, s_out, gate, x, g_a, mw["w_branch"], mw["w_out"])

        if l % 2 == 0:
            w1, w3, w2 = mw["ffn"]
            x = dense_ffn(g, x, p["norm_ffn"], sc_f, sh_f, g_f, w1, w3, w2, w1.shape[1] // 2)
        else:
            w1, w3, w2 = p["moe"]
            last = l == len(layers) - 1
            x = moe_ffn(g, x, p["norm_ffn"], sc_f, sh_f, g_f, *p["router"], w1, w3, w2,
                        out_norm_w=norm_final if last else None)
        for lst, v in zip(outs, (ckv.reshape(B, S, KVL), kpe.reshape(B, S, ROPE), shift1, wkv1, conv1, ssm1)):
            lst.append(v)
    if (len(layers) - 1) % 2 == 0:
        x = final_norm(g, x, norm_final)
    y = x.reshape(B, S, D_MODEL)
    return y, [jnp.stack(v) for v in outs]


def kernel(x_prompt, x_sample, cache_ckv, cache_kpe, state_rwkv_shift, state_rwkv_wkv, state_ssm_conv, state_ssm, page_table, c_prompt, c_sample, w_ada, b_ada, norm_attn, norm_ffn, norm_final, w_in, mla_q_norm, mla_w_uq, mla_kv_norm, mla_w_uk, mla_w_uv, rwkv_mu, rwkv_w0, rwkv_w2, rwkv_a0, rwkv_a2, rwkv_g2, rwkv_k_k, rwkv_k_a, rwkv_r_k, rwkv_ln_w, rwkv_ln_b, ssm_conv_w, ssm_conv_b, ssm_dt_bias, ssm_a_log, ssm_d, ssm_norm, w_branch, w_out, ffn_w1, ffn_w3, ffn_w2, moe_router, moe_router_b, moe_w1, moe_w3, moe_w2):
    W = dict(w_ada=w_ada, b_ada=b_ada, norm_attn=norm_attn, norm_ffn=norm_ffn, w_in=w_in, mla_q_norm=mla_q_norm,
             mla_w_uq=mla_w_uq, mla_kv_norm=mla_kv_norm, mla_w_uk=mla_w_uk, mla_w_uv=mla_w_uv, rwkv_mu=rwkv_mu,
             rwkv_w0=rwkv_w0, rwkv_w2=rwkv_w2, rwkv_a0=rwkv_a0, rwkv_a2=rwkv_a2, rwkv_g2=rwkv_g2, rwkv_k_k=rwkv_k_k,
             rwkv_k_a=rwkv_k_a, rwkv_r_k=rwkv_r_k, rwkv_ln_w=rwkv_ln_w, rwkv_ln_b=rwkv_ln_b, ssm_conv_w=ssm_conv_w,
             ssm_conv_b=ssm_conv_b, ssm_dt_bias=ssm_dt_bias, ssm_a_log=ssm_a_log, ssm_d=ssm_d, ssm_norm=ssm_norm,
             w_branch=w_branch, w_out=w_out, ffn_w1=ffn_w1, ffn_w3=ffn_w3, ffn_w2=ffn_w2, moe_router=moe_router,
             moe_router_b=moe_router_b, moe_w1=moe_w1, moe_w3=moe_w3, moe_w2=moe_w2)
    depth = w_in.shape[0]
    layers = [_pack_layer(l, W) for l in range(depth)]
    bp, sp, _ = x_prompt.shape
    bs, ss, _ = x_sample.shape
    dt = x_prompt.dtype
    y_prompt, (p_ckv, p_kpe, p_shift, p_wkv, p_conv, p_ssm) = _trunk(
        x_prompt, c_prompt, jnp.arange(sp, dtype=jnp.int32), None,
        jnp.zeros((depth, bp, RWKV_IN), dt), jnp.zeros((depth, bp, 8, RHD, RHD), dt),
        jnp.zeros((depth, bp, SCONV - 1, SCD), dt), jnp.zeros((depth, bp, SHEADS, SP, SN), dt), layers, norm_final)
    p_ckv = p_ckv.reshape(depth, bp * sp // PAGE, PAGE, KVL)
    p_kpe = p_kpe.reshape(depth, bp * sp // PAGE, PAGE, ROPE)
    past_len = page_table.shape[1] * PAGE
    pos_s = past_len + jnp.arange(ss, dtype=jnp.int32)
    y_sample, (s_ckv, s_kpe, s_shift, s_wkv, s_conv, s_ssm) = _trunk(
        x_sample, c_sample, pos_s, (cache_ckv, jnp.swapaxes(cache_kpe, 2, 3), page_table),
        state_rwkv_shift, state_rwkv_wkv, state_ssm_conv, state_ssm, layers, norm_final)
    return (y_prompt, y_sample, p_ckv, p_kpe, p_shift, p_wkv, p_conv, p_ssm,
            s_ckv, s_kpe, s_shift, s_wkv, s_conv, s_ssm)
```

```python
import functools

import jax
import jax.numpy as jnp
from jax import lax
from jax.experimental import pallas as pl
from jax.experimental.pallas import tpu as pltpu

F32 = jnp.float32
BF16 = jnp.bfloat16

D_MODEL = 1024
DEPTH = 2
PAGE = 128
H = 8
NOPE = 64
ROPE = 32
VD = 64
QL = 256
KVL = 256
ROPE_THETA = 10000.0
MLA_SCALE = (NOPE + ROPE) ** -0.5
LOG2E = 1.4426950408889634
QK = 384
RDIM = 512
RHD = 64
DECAY_LORA = 64
AAA_LORA = 64
GATE_LORA = 128
RWKV_IN = 3 * RDIM + DECAY_LORA + AAA_LORA + GATE_LORA
RWKV_LN_EPS = 64e-5
SHEADS = 8
SP = 64
SDIM = 512
SGROUPS = 2
SN = 128
SCONV = 4
SCHUNK = 128
SCD = SDIM + 2 * SGROUPS * SN
NBRANCH = 3
NE = 8
TOPK = 2
EPS = 1e-6
SEG_A = 768
DT_TILE = 4
DT_LANE = 32
SEG_WIDTHS = (SEG_A, RWKV_IN, SDIM, SCD, NBRANCH * D_MODEL)
W_IN_COLS = sum(SEG_WIDTHS)
LANES = 128
V7X_VMEM_BYTES = 64 * 1024 * 1024
VMEM_LIMIT = V7X_VMEM_BYTES - 8 * 1024 * 1024
ATTN_TQ = 128
ATTN_TK = 256
ROW_TILE = 512
WIN_ROW_TILE = 256
MOE_ROW_TILE = 512
MOE_FF_TILE = 1792


def _cparams(*sem):
    return pltpu.CompilerParams(dimension_semantics=sem, vmem_limit_bytes=VMEM_LIMIT)


def _dot(a, b):
    return jnp.dot(a, b, preferred_element_type=F32)


def _dot_nt(a, b):
    return lax.dot_general(a, b, (((1,), (1,)), ((), ())), preferred_element_type=F32)


def _hi_lo(x):
    hi = x.astype(BF16)
    return hi, (x.astype(F32) - hi.astype(F32)).astype(BF16)


def _dot3(a, w, dot=_dot):
    a_hi, a_lo = _hi_lo(a)
    w_hi, w_lo = _hi_lo(w)
    return dot(a_hi, w_hi) + dot(a_lo, w_hi) + dot(a_hi, w_lo)


def _mm(a, w, precise):
    return _dot3(a, w) if precise else _dot(a.astype(BF16), w)


def _split_dot(x, w01):
    hi = x.astype(BF16)
    lo = (x - hi.astype(F32)).astype(BF16)
    return _dot(hi, w01) + _dot(lo, w01)


def _split3_dot(x, w01):
    hi = x.astype(BF16)
    r1 = x - hi.astype(F32)
    mid = r1.astype(BF16)
    lo = (r1 - mid.astype(F32)).astype(BF16)
    return _dot(hi, w01) + _dot(mid, w01) + _dot(lo, w01)


def _split3_dot_left(w01, x):
    hi = x.astype(BF16)
    r1 = x - hi.astype(F32)
    mid = r1.astype(BF16)
    lo = (r1 - mid.astype(F32)).astype(BF16)
    return _dot(w01, hi) + _dot(w01, mid) + _dot(w01, lo)


def _sigmoid(x):
    return 1.0 / (1.0 + jnp.exp(-x))


def _silu(x):
    return x * _sigmoid(x)


def _softplus(x):
    return jnp.maximum(x, 0.0) + jnp.log(1.0 + jnp.exp(-jnp.abs(x)))


def _rms(x):
    return x * lax.rsqrt(jnp.mean(x * x, axis=-1, keepdims=True) + EPS)


class _Group:
    def __init__(self, B, S, tm):
        self.B, self.S, self.M = B, S, B * S
        self.precise = S == 1
        self.act_dtype = F32 if self.precise else BF16
        if S == 1:
            self.tm = min(tm, self.M)
            self.grid = (1, self.M // self.tm)
        else:
            self.tm = min(tm, S)
            self.grid = (B, S // self.tm)
        self.ns = self.grid[1]

    def rows(self, width, colblock=0):
        ns = self.ns
        return pl.BlockSpec((self.tm, width), lambda b, s: (b * ns + s, colblock))

    def full(self, shape):
        nd = len(shape)
        return pl.BlockSpec(shape, lambda b, s: (0,) * nd)

    def mod_array(self, m):
        return m.reshape(1, self.M, -1) if self.S == 1 else m.reshape(self.B, 1, -1)

    def mod_spec(self, width):
        if self.S == 1:
            return pl.BlockSpec((1, self.tm, width), lambda b, s: (0, s, 0))
        return pl.BlockSpec((1, 1, width), lambda b, s: (b, 0, 0))

    def pos_spec(self, width):
        if self.S == 1:
            return self.rows(width)
        return pl.BlockSpec((self.tm, width), lambda b, s: (s, 0))

    def tmaj_shape(self, width):
        return (self.M, width) if self.S == 1 else (self.S, self.B * width)

    def tmaj_spec(self, width):
        if self.S == 1:
            return self.rows(width)
        return pl.BlockSpec((self.tm, width), lambda b, s: (s, b))


def _ada_kernel(c_ref, w_ref, b_ref, o_ref):
    c = c_ref[...]
    o_ref[...] = _dot3(_silu(c), w_ref[...]) + b_ref[...]


def ada_matmul(c, w, b):
    m, k = c.shape
    n = w.shape[1]
    tn = 1024
    return pl.pallas_call(
        _ada_kernel,
        grid=(n // tn,),
        in_specs=[pl.BlockSpec((m, k), lambda j: (0, 0)), pl.BlockSpec((k, tn), lambda j: (0, j)),
                  pl.BlockSpec((1, tn), lambda j: (0, j))],
        out_specs=pl.BlockSpec((m, tn), lambda j: (0, j)),
        out_shape=jax.ShapeDtypeStruct((m, n), F32),
        compiler_params=_cparams("arbitrary"),
        name="ada_matmul",
    )(c, w, b.reshape(1, n))


def _win_kernel(x_ref, nw_ref, sc_ref, sh_ref, w_ref, *o_refs):
    h = (_rms(x_ref[...]) * nw_ref[...] * (1.0 + sc_ref[0]) + sh_ref[0]).astype(BF16)
    off = 0
    for o in o_refs:
        n = o.shape[-1]
        for c in range(0, n, 256):
            o[:, c:c + 256] = _dot(h, w_ref[:, off + c:off + c + 256])
        off += n


def _win_cols_kernel(x_ref, nw_ref, sc_ref, sh_ref, w_ref, o_ref):
    h = _rms(x_ref[...]) * nw_ref[...] * (1.0 + sc_ref[0]) + sh_ref[0]
    o_ref[...] = _dot3(h, w_ref[...])


def win_project_precise(g, x, nw, sc, sh, w_packed):
    tn = 512
    out = pl.pallas_call(
        _win_cols_kernel,
        grid=(W_IN_COLS // tn,),
        in_specs=[pl.BlockSpec((g.M, D_MODEL), lambda j: (0, 0)), pl.BlockSpec((1, D_MODEL), lambda j: (0, 0)),
                  pl.BlockSpec((1, g.M, D_MODEL), lambda j: (0, 0, 0)),
                  pl.BlockSpec((1, g.M, D_MODEL), lambda j: (0, 0, 0)),
                  pl.BlockSpec((D_MODEL, tn), lambda j: (0, j))],
        out_specs=pl.BlockSpec((g.M, tn), lambda j: (0, j)),
        out_shape=jax.ShapeDtypeStruct((g.M, W_IN_COLS), F32),
        compiler_params=_cparams("parallel"),
        name="win_project_precise",
    )(x, nw.reshape(1, -1), g.mod_array(sc), g.mod_array(sh), w_packed)
    offs = [0]
    for w in SEG_WIDTHS:
        offs.append(offs[-1] + w)
    return [out[:, offs[i]:offs[i + 1]] for i in range(len(SEG_WIDTHS))]


def win_project(g, x, nw, sc, sh, w_packed):
    if g.precise:
        return win_project_precise(g, x, nw, sc, sh, w_packed)
    return pl.pallas_call(
        _win_kernel,
        grid=g.grid,
        in_specs=[g.rows(D_MODEL), g.full((1, D_MODEL)), g.mod_spec(D_MODEL), g.mod_spec(D_MODEL),
                  g.full((D_MODEL, W_IN_COLS))],
        out_specs=[g.rows(w) for w in SEG_WIDTHS],
        out_shape=[jax.ShapeDtypeStruct((g.M, w), F32) for w in SEG_WIDTHS],
        compiler_params=_cparams("parallel", "parallel"),
        name="win_project",
    )(x, nw.reshape(1, -1), g.mod_array(sc), g.mod_array(sh), w_packed)


def _mla_prep_kernel(a_ref, cos_ref, sin_ref, cos8_ref, sin8_ref, qn_ref, kvn_ref, wuq_ref, wuk_ref,
                     qc_ref, kc_ref, ckv_ref, kpe_ref, *, precise):
    a = a_ref[...]
    tm = a.shape[0]
    odt = qc_ref.dtype
    qn = _rms(a[:, 0:QL]) * qn_ref[...]
    qa = _mm(qn, wuq_ref[...], precise)
    q_rope = qa[:, 512:768] * cos8_ref[...] + qa[:, 768:1024] * sin8_ref[...]
    zpad = jnp.zeros((tm, QK - KVL - ROPE), odt)
    qscale = MLA_SCALE if precise else MLA_SCALE * LOG2E
    for h in range(H):
        q_abs = _mm(qa[:, h * NOPE:(h + 1) * NOPE], wuk_ref[h], precise) * qscale
        qc_ref[h, :, 0:KVL] = q_abs.astype(odt)
        qc_ref[h, :, KVL:KVL + ROPE] = (q_rope[:, h * ROPE:(h + 1) * ROPE] * qscale).astype(odt)
        qc_ref[h, :, KVL + ROPE:QK] = zpad
    ckv = _rms(a[:, QL:QL + KVL]) * kvn_ref[...]
    kpe = a[:, 512:544] * cos_ref[...] + a[:, 552:584] * sin_ref[...]
    ckv_ref[...] = ckv
    kpe_ref[...] = kpe
    kc_ref[:, 0:KVL] = ckv.astype(odt)
    kc_ref[:, KVL:KVL + ROPE] = kpe.astype(odt)
    kc_ref[:, KVL + ROPE:QK] = zpad


def mla_prep(g, seg_a, tabs, qn, kvn, wuq, wuk):
    cos, sin, cos8, sin8 = tabs
    ns = g.ns
    return pl.pallas_call(
        functools.partial(_mla_prep_kernel, precise=g.precise),
        grid=g.grid,
        in_specs=[g.rows(SEG_A), g.pos_spec(ROPE), g.pos_spec(ROPE), g.pos_spec(H * ROPE), g.pos_spec(H * ROPE),
                  g.full((1, QL)), g.full((1, KVL)), g.full((QL, 1024)), g.full((H, NOPE, KVL))],
        out_specs=[pl.BlockSpec((H, g.tm, QK), lambda b, s: (0, b * ns + s, 0)), g.rows(QK), g.rows(KVL),
                   g.rows(ROPE)],
        out_shape=[jax.ShapeDtypeStruct((H, g.M, QK), g.act_dtype), jax.ShapeDtypeStruct((g.M, QK), g.act_dtype),
                   jax.ShapeDtypeStruct((g.M, KVL), F32), jax.ShapeDtypeStruct((g.M, ROPE), F32)],
        compiler_params=_cparams("parallel", "parallel"),
        name="mla_prep",
    )(seg_a, cos, sin, cos8, sin8, qn.reshape(1, -1), kvn.reshape(1, -1), wuq, wuk)


NEG = -1e30


def _attn_kernel(q_ref, k_ref, wuv_ref, o_ref, m_scr, l_scr, a_scr, acc_scr, s_scr, p_scr, *, tq, tk):
    qi = pl.program_id(1)
    q = q_ref[...].reshape(H * tq, QK)
    m_scr[...] = jnp.full(m_scr.shape, NEG, F32)
    l_scr[...] = jnp.zeros(l_scr.shape, F32)
    acc_scr[...] = jnp.zeros(acc_scr.shape, F32)
    reps = tk // LANES
    wide = lambda a: jnp.concatenate([a] * reps, axis=-1)

    def keys(j):
        return k_ref[pl.ds(pl.multiple_of(j * tk, tk), tk), :]

    def scores(j, slot):
        s_scr[slot] = _dot_nt(q, keys(j))

    def softmax_pv(j, slot, masked):
        if masked:
            visible = (j * tk + lax.broadcasted_iota(jnp.int32, (tq, tk), 1)
                       <= qi * tq + lax.broadcasted_iota(jnp.int32, (tq, tk), 0))
        for h in range(H):
            rs = pl.ds(h * tq, tq)
            s = s_scr[slot, rs, :]
            if masked:
                s = jnp.where(visible, s, NEG)
            m_prev = m_scr[rs, :]
            m_new = jnp.maximum(m_prev, jnp.max(s, axis=-1, keepdims=True))
            alpha = jnp.exp2(m_prev - m_new)
            p = jnp.exp2(s - wide(m_new))
            l_scr[rs, :] = alpha * l_scr[rs, :] + jnp.sum(p, axis=-1, keepdims=True)
            m_scr[rs, :] = m_new
            a_scr[rs, :] = alpha
            p_scr[slot, rs, :] = p.astype(BF16)
        alpha = a_scr[...]
        acc_scr[...] = (acc_scr[...] * jnp.concatenate([alpha] * (KVL // LANES), axis=-1)
                        + _dot(p_scr[slot], keys(j)[:, 0:KVL]))

    n_full = (qi * tq) // tk

    scores(0, 0)

    def body(i, carry):
        j = 2 * i
        scores(j + 1, 1)
        softmax_pv(j, 0, False)
        scores(j + 2, 0)
        softmax_pv(j + 1, 1, False)
        return carry

    lax.fori_loop(0, n_full // 2, body, 0)

    @pl.when(n_full % 2 == 0)
    def _():
        softmax_pv(n_full, 0, True)

    @pl.when(n_full % 2 == 1)
    def _():
        scores(n_full, 1)
        softmax_pv(n_full - 1, 0, False)
        softmax_pv(n_full, 1, True)
    inv_l = 1.0 / l_scr[...]
    o = acc_scr[...] * jnp.concatenate([inv_l] * (KVL // LANES), axis=-1)
    for h in range(H):
        o_ref[:, h * VD:(h + 1) * VD] = _dot(o[h * tq:(h + 1) * tq].astype(BF16), wuv_ref[h]).astype(o_ref.dtype)


def mla_attention(B, S, qc, kc, wuv):
    tq = min(ATTN_TQ, S)
    tk = min(ATTN_TK, S)
    nq = S // tq
    rows = H * tq
    return pl.pallas_call(
        functools.partial(_attn_kernel, tq=tq, tk=tk),
        grid=(B, nq),
        in_specs=[pl.BlockSpec((H, tq, QK), lambda b, i: (0, b * nq + i, 0)),
                  pl.BlockSpec((S, QK), lambda b, i: (b, 0)),
                  pl.BlockSpec((H, KVL, VD), lambda b, i: (0, 0, 0))],
        out_specs=pl.BlockSpec((tq, H * VD), lambda b, i: (b * nq + i, 0)),
        out_shape=jax.ShapeDtypeStruct((B * S, H * VD), BF16),
        scratch_shapes=[pltpu.VMEM((rows, LANES), F32), pltpu.VMEM((rows, LANES), F32), pltpu.VMEM((rows, LANES), F32),
                        pltpu.VMEM((rows, KVL), F32), pltpu.VMEM((2, rows, tk), F32),
                        pltpu.VMEM((2, rows, tk), BF16)],
        compiler_params=_cparams("parallel", "parallel"),
        name="mla_attention",
    )(qc, kc, wuv)


PAGES_PER_STEP = 16


def _paged_kernel(pt_ref, q_ref, knew_ref, wuv_ref, *rest):
    pp = PAGES_PER_STEP
    ckv_refs, kpe_refs = rest[:pp], rest[pp:2 * pp]
    o_ref, m_scr, l_scr, acc_scr = rest[2 * pp:]
    j = pl.program_id(1)
    q = q_ref[...]

    @pl.when(j == 0)
    def _():
        m_scr[...] = jnp.full(m_scr.shape, NEG, F32)
        l_scr[...] = jnp.zeros(l_scr.shape, F32)
        acc_scr[...] = jnp.zeros(acc_scr.shape, F32)

    def stack_hi_lo(x):
        hi = x.astype(BF16).astype(F32)
        return jnp.concatenate([hi, x - hi], axis=0).astype(BF16)

    qa2 = stack_hi_lo(q[:, 0:KVL])
    qp2 = stack_hi_lo(q[:, KVL:KVL + ROPE])

    def chain(refs_c, refs_p):
        ckv_hi, ckv_lo = _hi_lo(jnp.concatenate([r[...] for r in refs_c], axis=0))
        kpe_hi, kpe_lo = _hi_lo(jnp.concatenate([r[...] for r in refs_p], axis=1))
        s2 = _dot_nt(qa2, ckv_hi) + _dot(qp2, kpe_hi)
        s = s2[0:H] + s2[H:2 * H] + _dot_nt(qa2[0:H], ckv_lo) + _dot(qp2[0:H], kpe_lo)
        m = jnp.max(s, axis=-1, keepdims=True)
        p = jnp.exp(s - m)
        p2 = stack_hi_lo(p)
        pv2 = _dot(p2, jnp.concatenate([ckv_hi, ckv_lo], axis=1))
        pv = pv2[0:H, 0:KVL] + pv2[H:2 * H, 0:KVL] + pv2[0:H, KVL:2 * KVL]
        return m, jnp.sum(p, axis=-1, keepdims=True), pv

    half = pp // 2
    parts = [chain(ckv_refs[i * half:(i + 1) * half], kpe_refs[i * half:(i + 1) * half]) for i in range(2)]
    m_prev = m_scr[...]
    m_new = jnp.maximum(m_prev, jnp.maximum(parts[0][0], parts[1][0]))
    alpha = jnp.exp(m_prev - m_new)
    l = alpha * l_scr[...]
    acc = alpha * acc_scr[...]
    for m_i, l_i, pv_i in parts:
        w_i = jnp.exp(m_i - m_new)
        l = l + w_i * l_i
        acc = acc + w_i * pv_i
    l_scr[...] = l
    acc_scr[...] = acc
    m_scr[...] = m_new

    @pl.when(j == pl.num_programs(1) - 1)
    def _():
        kn = knew_ref[0]
        s_new = jnp.sum(q * kn, axis=-1, keepdims=True)
        m_prev = m_scr[...]
        m_new = jnp.maximum(m_prev, s_new)
        alpha = jnp.exp(m_prev - m_new)
        p_new = jnp.exp(s_new - m_new)
        l = alpha * l_scr[...] + p_new
        acc = alpha * acc_scr[...] + p_new * kn[:, 0:KVL]
        o = acc / l
        for h in range(H):
            o_ref[0, :, h * VD:(h + 1) * VD] = _dot3(o[h:h + 1], wuv_ref[h])


def paged_attention(layer, page_table, qc, kc_new, cache_ckv, cache_kpe_t, wuv):
    B, n_pages = page_table.shape
    pp = PAGES_PER_STEP
    nsteps = n_pages // pp

    def page_spec(i, shape):
        return pl.BlockSpec((None, None) + shape, lambda b, j, pt: (layer, pt[b, j * pp + i], 0, 0))

    grid_spec = pltpu.PrefetchScalarGridSpec(
        num_scalar_prefetch=1,
        grid=(B, nsteps),
        in_specs=[pl.BlockSpec((None, H, QK), lambda b, j, pt: (b, 0, 0)),
                  pl.BlockSpec((1, 1, QK), lambda b, j, pt: (b, 0, 0)),
                  pl.BlockSpec((H, KVL, VD), lambda b, j, pt: (0, 0, 0))]
        + [page_spec(i, (PAGE, KVL)) for i in range(pp)] + [page_spec(i, (ROPE, PAGE)) for i in range(pp)],
        out_specs=pl.BlockSpec((1, 1, H * VD), lambda b, j, pt: (b, 0, 0)),
        scratch_shapes=[pltpu.VMEM((H, 1), F32), pltpu.VMEM((H, 1), F32), pltpu.VMEM((H, KVL), F32)],
    )
    out = pl.pallas_call(
        _paged_kernel,
        grid_spec=grid_spec,
        out_shape=jax.ShapeDtypeStruct((B, 1, H * VD), F32),
        compiler_params=_cparams("parallel", "arbitrary"),
        name="paged_attention",
    )(page_table, jnp.transpose(qc, (1, 0, 2)), kc_new.reshape(B, 1, QK), wuv, *([cache_ckv] * pp),
      *([cache_kpe_t] * pp))
    return out.reshape(B, H * VD)


def _rwkv_prep_kernel(rw_ref, prev_ref, mu_ref, w0_ref, a0_ref, kk_ref_, ka_ref, w2_ref, a2_ref, g2_ref, ones_ref,
                      r_o, w_o, k_o, v_o, kk_o, b_o, g_o, *rest, precise, seq):
    rw = rw_ref[...]
    if seq:
        vt_o, carry = rest
        tm = rw.shape[0]

        @pl.when(pl.program_id(1) == 0)
        def _():
            carry[...] = prev_ref[0]

        row8 = lax.broadcasted_iota(jnp.int32, (8, RWKV_IN), 0)
        shifted = pltpu.roll(rw, 1, 0)
        top = jnp.where(row8 == 0, pltpu.roll(carry[...], 1, 0), shifted[0:8])
        prev = jnp.concatenate([top, shifted[8:]], axis=0)
        carry[...] = rw[tm - 8:tm]
    else:
        prev = prev_ref[...]
    xs = rw + (prev - rw) * mu_ref[...]
    r = xs[:, 0:RDIM]
    k = xs[:, RDIM:2 * RDIM]
    v = xs[:, 2 * RDIM:3 * RDIM]
    o = 3 * RDIM
    wl = xs[:, o:o + DECAY_LORA]
    al = xs[:, o + DECAY_LORA:o + DECAY_LORA + AAA_LORA]
    gl = xs[:, o + DECAY_LORA + AAA_LORA:RWKV_IN]
    w = -_softplus(-(w0_ref[...] + _mm(jnp.tanh(wl), w2_ref[...], precise))) - 0.5
    a = _sigmoid(a0_ref[...] + _mm(al, a2_ref[...], precise))
    kk = k * kk_ref_[...]
    kk = kk * lax.rsqrt(jnp.maximum(_split_dot(kk * kk, ones_ref[...]), 1e-24))
    r_o[...] = r
    w_o[...] = jnp.exp(-jnp.exp(w))
    k_o[...] = k * (1.0 + (a - 1.0) * ka_ref[...])
    v_o[...] = v
    kk_o[...] = kk
    b_o[...] = kk * a
    g_o[...] = _mm(_sigmoid(gl), g2_ref[...], precise)
    if seq:
        ri = lax.broadcasted_iota(jnp.int32, (SCAN_W, SCAN_W), 0)
        ci = lax.broadcasted_iota(jnp.int32, (SCAN_W, SCAN_W), 1)
        same_head = (ri // RHD) == (ci // RHD)
        eye_rep = (lax.broadcasted_iota(jnp.int32, (RHD, SCAN_W), 0)
                   == lax.broadcasted_iota(jnp.int32, (RHD, SCAN_W), 1) % RHD).astype(BF16)
        for c in range(tm // RHD):
            for t in range(SCAN_TPB):
                v_c = v[c * RHD:(c + 1) * RHD, t * SCAN_W:(t + 1) * SCAN_W].astype(BF16)
                blockdiag = jnp.where(same_head, jnp.concatenate([v_c] * SCAN_HPT, axis=0), jnp.zeros((), BF16))
                vt_o[c, t] = _dot_nt(eye_rep, blockdiag).astype(vt_o.dtype)


def rwkv_prep(g, rw, prev, p, mw, ones_bd):
    vec = lambda a: a.reshape(1, -1)
    tshape = jax.ShapeDtypeStruct(g.tmaj_shape(RDIM), F32)
    seq = g.S > 1
    out_specs = [g.tmaj_spec(RDIM)] * 6 + [g.rows(RDIM)]
    out_shape = [tshape] * 6 + [jax.ShapeDtypeStruct((g.M, RDIM), F32)]
    scratch = []
    if seq:
        assert g.tm % RHD == 0
        cpt = g.tm // RHD
        prev = jnp.pad(prev[:, None, :], ((0, 0), (7, 0), (0, 0)))
        prev_spec = pl.BlockSpec((1, 8, RWKV_IN), lambda b, s: (b, 0, 0))
        out_specs.append(pl.BlockSpec((cpt, SCAN_TPB, RHD, SCAN_W), lambda b, s: (s, b, 0, 0)))
        out_shape.append(jax.ShapeDtypeStruct((g.S // RHD, g.B * SCAN_TPB, RHD, SCAN_W), BF16))
        scratch = [pltpu.VMEM((8, RWKV_IN), F32)]
    else:
        prev_spec = g.rows(RWKV_IN)
    return pl.pallas_call(
        functools.partial(_rwkv_prep_kernel, precise=g.precise, seq=seq),
        grid=g.grid,
        in_specs=[g.rows(RWKV_IN), prev_spec, g.full((1, RWKV_IN)), g.full((1, RDIM)), g.full((1, RDIM)),
                  g.full((1, RDIM)), g.full((1, RDIM)), g.full((DECAY_LORA, RDIM)), g.full((AAA_LORA, RDIM)),
                  g.full((GATE_LORA, RDIM)), g.full((RDIM, RDIM))],
        out_specs=out_specs,
        out_shape=out_shape,
        scratch_shapes=scratch,
        compiler_params=_cparams("parallel", "arbitrary"),
        name="rwkv_prep",
    )(rw, prev, vec(p["mu"]), vec(p["w0"]), vec(p["a0"]), vec(p["k_k"]), vec(p["k_a"]), mw["w2"], mw["a2"], mw["g2"],
      ones_bd)


SCAN_NB = 16
SCAN_HPT = 4
SCAN_W = SCAN_HPT * RHD
SCAN_TPB = RDIM // SCAN_W
SCAN_GB = 8
SCAN_YSUB = LANES // SCAN_HPT


def _scan_kernel(r_ref, w_ref, k_ref, kk_ref, b_ref, vt_ref, s0_ref, yt_ref, sout_ref, s_scr, *, tc_len, ng, precise):
    tc = pl.program_id(1)

    def pick(x, w01):
        return _split_dot(x, w01) if precise else _dot(x.astype(BF16), w01)

    @pl.when(tc == 0)
    def _():
        s_scr[...] = s0_ref[...]

    ri = lax.broadcasted_iota(jnp.int32, (SCAN_W, SCAN_W), 0)
    ci = lax.broadcasted_iota(jnp.int32, (SCAN_W, SCAN_W), 1)
    same_head = (ri // RHD) == (ci // RHD)
    ones_bd = same_head.astype(BF16)
    yri = lax.broadcasted_iota(jnp.int32, (SCAN_W, LANES), 0)
    yci = lax.broadcasted_iota(jnp.int32, (SCAN_W, LANES), 1)
    yt_ref[...] = jnp.zeros(yt_ref.shape, F32)
    gb = min(SCAN_GB, ng)
    sub = min(SCAN_YSUB, tc_len)

    def step(tt, carry, part):
        tg = part * sub + tt
        e_t = (same_head & ((ri % RHD) == tg)).astype(BF16)
        y_t = ((yri // RHD) * SCAN_YSUB + tt == yci).astype(BF16)
        ylanes = pl.ds(part * LANES, LANES)

        def rows(ref, g0):
            return jnp.stack([jnp.broadcast_to(ref[pl.ds(tg, 1), pl.ds((g0 + i) * SCAN_W, SCAN_W)], (RHD, SCAN_W))
                              for i in range(gb)])

        def issue(g0):
            sa = pick((s_scr[g0:g0 + gb] * rows(kk_ref, g0)).reshape(gb * RHD, SCAN_W), ones_bd)
            vcol = pick(vt_ref[0, g0:g0 + gb].reshape(gb * RHD, SCAN_W), e_t)
            return sa.reshape(gb, RHD, SCAN_W), vcol.reshape(gb, RHD, SCAN_W)

        pend = issue(0)
        y_pend = None
        for g0 in range(0, ng, gb):
            nxt = issue(g0 + gb) if g0 + gb < ng else None
            sa, vcol = pend
            s = s_scr[g0:g0 + gb] * rows(w_ref, g0) - sa * rows(b_ref, g0) + vcol * rows(k_ref, g0)
            s_scr[g0:g0 + gb] = s
            y = pick((s * rows(r_ref, g0)).reshape(gb * RHD, SCAN_W), y_t).reshape(gb, RHD, LANES)
            if y_pend is not None:
                gp, yp = y_pend
                yt_ref[0, gp:gp + gb, :, ylanes] += yp
            y_pend = (g0, y)
            pend = nxt
        gp, yp = y_pend
        yt_ref[0, gp:gp + gb, :, ylanes] += yp
        return carry

    unroll = 4 if sub % 4 == 0 else 1
    for part in range(tc_len // sub):
        lax.fori_loop(0, sub, functools.partial(step, part=part), 0, unroll=unroll)

    @pl.when(tc == pl.num_programs(1) - 1)
    def _():
        sout_ref[...] = s_scr[...]


def rwkv_scan(B, S, r, w, k, kk, b, vt, s0, precise):
    tc_len = min(RHD, S)
    nb = min(SCAN_NB, B)
    ng = nb * SCAN_TPB
    nchunks = S // tc_len
    row_spec = pl.BlockSpec((tc_len, nb * RDIM), lambda bg, c: (c, bg))
    st_spec = pl.BlockSpec((ng, RHD, SCAN_W), lambda bg, c: (bg, 0, 0))
    ch_spec = pl.BlockSpec((1, ng, RHD, SCAN_W), lambda bg, c: (c, bg, 0, 0))
    return pl.pallas_call(
        functools.partial(_scan_kernel, tc_len=tc_len, ng=ng, precise=precise),
        grid=(B // nb, nchunks),
        in_specs=[row_spec] * 5 + [ch_spec, st_spec],
        out_specs=[ch_spec, st_spec],
        out_shape=[jax.ShapeDtypeStruct((nchunks, B * SCAN_TPB, RHD, SCAN_W), F32),
                   jax.ShapeDtypeStruct((B * SCAN_TPB, RHD, SCAN_W), F32)],
        scratch_shapes=[pltpu.VMEM((ng, RHD, SCAN_W), F32)],
        compiler_params=_cparams("parallel", "arbitrary"),
        name="rwkv_scan",
    )(r, w, k, kk, b, vt, s0)


def _rwkv_post_kernel(y_ref, r_ref, k_ref, v_ref, g_ref, lnw_ref, lnb_ref, rk_ref, ones_ref, o_ref):
    ones = ones_ref[...]
    y = y_ref[...]
    mu = _split_dot(y, ones) * (1.0 / RHD)
    yc = y - mu
    var = _split_dot(yc * yc, ones) * (1.0 / RHD)
    yn = yc * lax.rsqrt(var + RWKV_LN_EPS) * lnw_ref[...] + lnb_ref[...]
    v = v_ref[...]
    bonus = _split_dot(r_ref[...] * k_ref[...] * rk_ref[...], ones)
    o_ref[...] = ((yn + bonus * v) * g_ref[...]).astype(o_ref.dtype)


def rwkv_post(g, y, r, k, v, gate, p, ones_bd):
    vec = lambda a: a.reshape(1, -1)
    return pl.pallas_call(
        _rwkv_post_kernel,
        grid=g.grid,
        in_specs=[g.rows(RDIM), g.tmaj_spec(RDIM), g.tmaj_spec(RDIM), g.tmaj_spec(RDIM), g.rows(RDIM),
                  g.full((1, RDIM)), g.full((1, RDIM)), g.full((1, RDIM)), g.full((RDIM, RDIM))],
        out_specs=g.rows(RDIM),
        out_shape=jax.ShapeDtypeStruct((g.M, RDIM), g.act_dtype),
        compiler_params=_cparams("parallel", "parallel"),
        name="rwkv_post",
    )(y, r, k, v, gate, vec(p["ln_w"]), vec(p["ln_b"]), vec(p["r_k"]), ones_bd)


def _ssm_gate_norm(y, z, nw):
    y = y * _silu(z)
    gw = SDIM // SGROUPS
    parts = [_rms(y[:, i * gw:(i + 1) * gw]) for i in range(SGROUPS)]
    return jnp.concatenate(parts, axis=-1) * nw


def _ssd_kernel(xbc_ref, z_ref, dt_ref, tail0_ref, cw_ref, cb_ref, dtb_ref, a_ref, dsk_ref, nw_ref,
                o_ref, hout_ref, tail_scr, h_scr, y_scr):
    c = pl.program_id(1)
    L = SCHUNK

    @pl.when(c == 0)
    def _():
        tail_scr[...] = tail0_ref[0]
        h_scr[...] = jnp.zeros(h_scr.shape, F32)

    xbc = xbc_ref[...]
    tail = tail_scr[...]
    row8 = lax.broadcasted_iota(jnp.int32, (8, SCD), 0)
    conv = cb_ref[...] + xbc * cw_ref[SCONV - 1:SCONV, :]
    for sft in range(1, SCONV):
        sh = pltpu.roll(xbc, sft, 0)
        top = jnp.where(row8 < sft, pltpu.roll(tail, sft, 0), sh[0:8])
        sh = jnp.concatenate([top, sh[8:]], axis=0)
        conv = conv + sh * cw_ref[SCONV - 1 - sft:SCONV - sft, :]
    tail_scr[...] = xbc[L - 8:L]
    act = _silu(conv)
    xa = act[:, 0:SDIM]
    lane = lax.broadcasted_iota(jnp.int32, (L, LANES), 1)
    dt_valid = (lane >= DT_LANE) & (lane < DT_LANE + SHEADS)
    dtt = jnp.where(dt_valid, _softplus(dt_ref[...] + dtb_ref[...]), 0.0)
    adt = dtt * a_ref[...]
    ri = lax.broadcasted_iota(jnp.int32, (L, L), 0)
    ci = lax.broadcasted_iota(jnp.int32, (L, L), 1)
    causal = ri >= ci
    cs = _split3_dot_left(causal.astype(BF16), adt)
    cs_t = cs.T
    sel_r = lax.broadcasted_iota(jnp.int32, (LANES, SHEADS * LANES), 0)
    sel_c = lax.broadcasted_iota(jnp.int32, (LANES, SHEADS * LANES), 1)
    col_all = _split3_dot(cs, (sel_r == DT_LANE + sel_c // LANES).astype(BF16))
    sel_r = lax.broadcasted_iota(jnp.int32, (LANES, SDIM), 0)
    sel_c = lax.broadcasted_iota(jnp.int32, (LANES, SDIM), 1)
    xdt_all = xa * _split3_dot(dtt, (sel_r == DT_LANE + sel_c // SP).astype(BF16))
    for grp in range(SGROUPS):
        bm = act[:, SDIM + grp * SN:SDIM + (grp + 1) * SN]
        cm = act[:, SDIM + SGROUPS * SN + grp * SN:SDIM + SGROUPS * SN + (grp + 1) * SN]
        cb = _dot_nt(cm.astype(BF16), bm.astype(BF16))
        for hh in range(SHEADS // SGROUPS):
            h = grp * (SHEADS // SGROUPS) + hh
            ln = DT_LANE + h
            col = col_all[:, h * LANES:(h + 1) * LANES]
            row = cs_t[ln:ln + 1, :]
            tot = col[L - 1:L, :]
            lmat = jnp.exp(jnp.where(causal, col - row, NEG))
            xh = xa[:, h * SP:(h + 1) * SP]
            xdt = xdt_all[:, h * SP:(h + 1) * SP].astype(BF16)
            hprev = h_scr[h]
            y = _dot((cb * lmat).astype(BF16), xdt)
            y = y + _dot((cm * jnp.exp(col)).astype(BF16), hprev.astype(BF16))
            y_scr[:, h * SP:(h + 1) * SP] = y + dsk_ref[:, h * SP:(h + 1) * SP] * xh
            bdec = (bm * jnp.exp(tot - col)).T.astype(BF16)
            h_scr[h] = hprev * jnp.exp(tot[:, 0:SP]) + _dot(bdec, xdt)
    o_ref[...] = _ssm_gate_norm(y_scr[...], z_ref[...], nw_ref[...]).astype(o_ref.dtype)

    @pl.when(c == pl.num_programs(1) - 1)
    def _():
        hout_ref[0] = h_scr[...]


def ssd_prompt(B, S, xbc, z, seg_a, tail0, p):
    nc = S // SCHUNK
    L = SCHUNK
    full = lambda shape: pl.BlockSpec(shape, lambda b, c: (0,) * len(shape))
    return pl.pallas_call(
        _ssd_kernel,
        grid=(B, nc),
        in_specs=[pl.BlockSpec((L, SCD), lambda b, c: (b * nc + c, 0)),
                  pl.BlockSpec((L, SDIM), lambda b, c: (b * nc + c, 0)),
                  pl.BlockSpec((L, LANES), lambda b, c: (b * nc + c, DT_TILE)),
                  pl.BlockSpec((1, 8, SCD), lambda b, c: (b, 0, 0)),
                  full((SCONV, SCD)), full((1, SCD)), full((1, LANES)), full((1, LANES)), full((1, SDIM)),
                  full((1, SDIM))],
        out_specs=[pl.BlockSpec((L, SDIM), lambda b, c: (b * nc + c, 0)),
                   pl.BlockSpec((1, SHEADS, SN, SP), lambda b, c: (b, 0, 0, 0))],
        out_shape=[jax.ShapeDtypeStruct((B * S, SDIM), BF16), jax.ShapeDtypeStruct((B, SHEADS, SN, SP), F32)],
        scratch_shapes=[pltpu.VMEM((8, SCD), F32), pltpu.VMEM((SHEADS, SN, SP), F32), pltpu.VMEM((L, SDIM), F32)],
        compiler_params=_cparams("parallel", "arbitrary"),
        name="ssd_prompt",
    )(xbc, z, seg_a, tail0, p["conv_w"], p["conv_b"].reshape(1, -1), p["dtb128"], p["a128"], p["dskip512"],
      p["ssm_norm"].reshape(1, -1))


def _ssm_step_pre_kernel(xbc_ref, c0_ref, c1_ref, c2_ref, dt_ref, cw_ref, cb_ref, dtb_ref, a_ref,
                         act_ref, xdt_ref, dec_ref):
    conv = (cb_ref[...] + c0_ref[...] * cw_ref[0:1, :] + c1_ref[...] * cw_ref[1:2, :] + c2_ref[...] * cw_ref[2:3, :]
            + xbc_ref[...] * cw_ref[3:4, :])
    act = _silu(conv)
    act_ref[...] = act
    dtt = _softplus(dt_ref[...] + dtb_ref[...])
    dec_ref[...] = jnp.exp(dtt * a_ref[...])
    for h in range(SHEADS):
        xdt_ref[:, h * SP:(h + 1) * SP] = act[:, h * SP:(h + 1) * SP] * dtt[:, DT_LANE + h:DT_LANE + h + 1]


def ssm_step_pre(g, xbc, c0, c1, c2, seg_a, p):
    return pl.pallas_call(
        _ssm_step_pre_kernel,
        grid=g.grid,
        in_specs=[g.rows(SCD)] * 4 + [g.rows(LANES, DT_TILE), g.full((SCONV, SCD)), g.full((1, SCD)),
                                      g.full((1, LANES)), g.full((1, LANES))],
        out_specs=[g.rows(SCD), g.rows(SDIM), g.rows(LANES)],
        out_shape=[jax.ShapeDtypeStruct((g.M, SCD), F32), jax.ShapeDtypeStruct((g.M, SDIM), F32),
                   jax.ShapeDtypeStruct((g.M, LANES), F32)],
        compiler_params=_cparams("parallel", "parallel"),
        name="ssm_step_pre",
    )(xbc, c0, c1, c2, seg_a, p["conv_w"], p["conv_b"].reshape(1, -1), p["dtb128"], p["a128"])


SSM_STEP_BT = 8


def _ssm_step_kernel(h0_ref, xb_ref, dec_ref, bm_ref, cm_ref, h1_ref, y_ref):
    bt = SSM_STEP_BT
    rp = lax.broadcasted_iota(jnp.int32, (SP, LANES), 0)
    lp = lax.broadcasted_iota(jnp.int32, (SP, LANES), 1)
    pick = [lp == rp, lp == rp + SP]
    hpg = SHEADS // SGROUPS
    for i in range(bt):
        for hp in range(SHEADS // 2):
            yrow = jnp.zeros((1, LANES), F32)
            for e in range(2):
                h = hp * 2 + e
                grp = h // hpg
                bm = bm_ref[i:i + 1, grp * SN:(grp + 1) * SN]
                cm = cm_ref[i:i + 1, grp * SN:(grp + 1) * SN]
                h1 = h0_ref[i, h] * dec_ref[i, h] + xb_ref[i, h] * bm
                h1_ref[i, h] = h1
                ycol = jnp.sum(h1 * cm, axis=-1, keepdims=True)
                yrow = yrow + jnp.sum(jnp.where(pick[e], ycol, 0.0), axis=0, keepdims=True)
            y_ref[i:i + 1, hp * LANES:(hp + 1) * LANES] = yrow


def ssm_step(B, layer, h0, xb, dec, bm, cm):
    bt = SSM_STEP_BT
    st = pl.BlockSpec((bt, SHEADS, SP, SN), lambda i: (i, 0, 0, 0))
    return pl.pallas_call(
        _ssm_step_kernel,
        grid=(B // bt,),
        in_specs=[pl.BlockSpec((None, bt, SHEADS, SP, SN), lambda i: (layer, i, 0, 0, 0)), st,
                  pl.BlockSpec((bt, SHEADS, 1, SN), lambda i: (i, 0, 0, 0)),
                  pl.BlockSpec((bt, SGROUPS * SN), lambda i: (i, 0)), pl.BlockSpec((bt, SGROUPS * SN), lambda i: (i, 0))],
        out_specs=[st, pl.BlockSpec((bt, SDIM), lambda i: (i, 0))],
        out_shape=[jax.ShapeDtypeStruct((B, SHEADS, SP, SN), F32), jax.ShapeDtypeStruct((B, SDIM), F32)],
        compiler_params=_cparams("parallel"),
        name="ssm_step",
    )(h0, xb, dec, bm, cm)


def _ssm_step_post_kernel(y_ref, x_ref, z_ref, dsk_ref, nw_ref, o_ref):
    y = y_ref[...] + dsk_ref[...] * x_ref[...]
    o_ref[...] = _ssm_gate_norm(y, z_ref[...], nw_ref[...]).astype(o_ref.dtype)


def ssm_step_post(g, y, act, z, p):
    return pl.pallas_call(
        _ssm_step_post_kernel,
        grid=g.grid,
        in_specs=[g.rows(SDIM), g.rows(SDIM), g.rows(SDIM), g.full((1, SDIM)), g.full((1, SDIM))],
        out_specs=g.rows(SDIM),
        out_shape=jax.ShapeDtypeStruct((g.M, SDIM), g.act_dtype),
        compiler_params=_cparams("parallel", "parallel"),
        name="ssm_step_post",
    )(y, act, z, p["dskip512"], p["ssm_norm"].reshape(1, -1))


def _merge_kernel(a_ref, r_ref, s_ref, gate_ref, x_ref, ga_ref, wb_ref, wo_ref, o_ref, *, precise):
    acc = None
    for i, br in enumerate((a_ref, r_ref, s_ref)):
        t = _sigmoid(gate_ref[:, i * D_MODEL:(i + 1) * D_MODEL]) * _mm(br[...], wb_ref[i], precise)
        acc = t if acc is None else acc + t
    mix = _mm(acc, wo_ref[...], precise)
    o_ref[...] = x_ref[...] + ga_ref[0] * mix


def merge_branches(g, a_out, r_out, s_out, gate, x, ga, wb, wo):
    return pl.pallas_call(
        functools.partial(_merge_kernel, precise=g.precise),
        grid=g.grid,
        in_specs=[g.rows(512), g.rows(512), g.rows(512), g.rows(NBRANCH * D_MODEL), g.rows(D_MODEL),
                  g.mod_spec(D_MODEL), g.full((NBRANCH, 512, D_MODEL)), g.full((D_MODEL, D_MODEL))],
        out_specs=g.rows(D_MODEL),
        out_shape=jax.ShapeDtypeStruct((g.M, D_MODEL), F32),
        compiler_params=_cparams("parallel", "parallel"),
        name="merge_branches",
    )(a_out, r_out, s_out, gate, x, g.mod_array(ga), wb, wo)


def _ffn_kernel(x_ref, nw_ref, sc_ref, sh_ref, gf_ref, w1_ref, w3_ref, w2_ref, o_ref, h_scr, acc_scr, *, precise):
    j = pl.program_id(2)

    @pl.when(j == 0)
    def _():
        h_scr[...] = (_rms(x_ref[...]) * nw_ref[...] * (1.0 + sc_ref[0]) + sh_ref[0]).astype(h_scr.dtype)
        acc_scr[...] = jnp.zeros(acc_scr.shape, F32)

    h = h_scr[...]
    u = _silu(_mm(h, w1_ref[...], precise)) * _mm(h, w3_ref[...], precise)
    acc_scr[...] += _mm(u, w2_ref[...], precise)

    @pl.when(j == pl.num_programs(2) - 1)
    def _():
        o_ref[...] = x_ref[...] + gf_ref[0] * acc_scr[...]


def dense_ffn(g, x, nw, sc, sh, gf, w1, w3, w2, tf):
    dff = w1.shape[1]
    lift = lambda spec: pl.BlockSpec(spec.block_shape, lambda b, s, j, f=spec.index_map: f(b, s))
    return pl.pallas_call(
        functools.partial(_ffn_kernel, precise=g.precise),
        grid=g.grid + (dff // tf,),
        in_specs=[lift(g.rows(D_MODEL)), lift(g.full((1, D_MODEL))), lift(g.mod_spec(D_MODEL)),
                  lift(g.mod_spec(D_MODEL)), lift(g.mod_spec(D_MODEL)),
                  pl.BlockSpec((D_MODEL, tf), lambda b, s, j: (0, j)), pl.BlockSpec((D_MODEL, tf), lambda b, s, j: (0, j)),
                  pl.BlockSpec((tf, D_MODEL), lambda b, s, j: (j, 0))],
        out_specs=lift(g.rows(D_MODEL)),
        out_shape=jax.ShapeDtypeStruct((g.M, D_MODEL), F32),
        scratch_shapes=[pltpu.VMEM((g.tm, D_MODEL), g.act_dtype), pltpu.VMEM((g.tm, D_MODEL), F32)],
        compiler_params=_cparams("parallel", "parallel", "arbitrary"),
        name="dense_ffn",
    )(x, nw.reshape(1, -1), g.mod_array(sc), g.mod_array(sh), g.mod_array(gf), w1, w3, w2)


def _router_kernel(x_ref, nw_ref, sc_ref, sh_ref, wr_ref, rb_ref, h_ref, logit_ref):
    h = _rms(x_ref[...]) * nw_ref[...] * (1.0 + sc_ref[0]) + sh_ref[0]
    h_ref[...] = h.astype(BF16)
    logit_ref[...] = _dot3(h, wr_ref[...]) + rb_ref[...]


def moe_router(g, x, nw, sc, sh, wr, rb):
    return pl.pallas_call(
        _router_kernel,
        grid=g.grid,
        in_specs=[g.rows(D_MODEL), g.full((1, D_MODEL)), g.mod_spec(D_MODEL), g.mod_spec(D_MODEL),
                  g.full((D_MODEL, LANES)), g.full((1, LANES))],
        out_specs=[g.rows(D_MODEL), g.rows(LANES)],
        out_shape=[jax.ShapeDtypeStruct((g.M, D_MODEL), BF16), jax.ShapeDtypeStruct((g.M, LANES), F32)],
        compiler_params=_cparams("parallel", "parallel"),
        name="moe_router",
    )(x, nw.reshape(1, -1), g.mod_array(sc), g.mod_array(sh), wr, rb)


def _expert_kernel(te_ref, tv_ref, h_ref, w1_ref, w3_ref, w2_ref, o_ref, acc_scr):
    i = pl.program_id(0)
    j = pl.program_id(1)

    @pl.when(j == 0)
    def _():
        acc_scr[...] = jnp.zeros(acc_scr.shape, F32)

    @pl.when(tv_ref[i] > 0)
    def _():
        h = h_ref[...]
        u = (_silu(_dot(h, w1_ref[...])) * _dot(h, w3_ref[...])).astype(BF16)
        acc_scr[...] += _dot(u, w2_ref[...])

    @pl.when(j == pl.num_programs(1) - 1)
    def _():
        o_ref[...] = acc_scr[...]


def expert_ffn(tile_expert, tile_valid, h_sorted, w1, w3, w2, tm, tf):
    rows = h_sorted.shape[0]
    dffe = w1.shape[2]
    grid_spec = pltpu.PrefetchScalarGridSpec(
        num_scalar_prefetch=2,
        grid=(rows // tm, dffe // tf),
        in_specs=[pl.BlockSpec((tm, D_MODEL), lambda i, j, te, tv: (i, 0)),
                  pl.BlockSpec((None, D_MODEL, tf), lambda i, j, te, tv: (te[i], 0, j)),
                  pl.BlockSpec((None, D_MODEL, tf), lambda i, j, te, tv: (te[i], 0, j)),
                  pl.BlockSpec((None, tf, D_MODEL), lambda i, j, te, tv: (te[i], j, 0))],
        out_specs=pl.BlockSpec((tm, D_MODEL), lambda i, j, te, tv: (i, 0)),
        scratch_shapes=[pltpu.VMEM((tm, D_MODEL), F32)],
    )
    return pl.pallas_call(
        _expert_kernel,
        grid_spec=grid_spec,
        out_shape=jax.ShapeDtypeStruct((rows, D_MODEL), F32),
        compiler_params=_cparams("parallel", "arbitrary"),
        name="expert_ffn",
    )(tile_expert, tile_valid, h_sorted, w1, w3, w2)


def _combine_kernel(x_ref, gf_ref, y0_ref, y1_ref, wt_ref, *rest):
    o_ref = rest[-1]
    wt = wt_ref[...]
    f = wt[:, 0:1] * y0_ref[...] + wt[:, 1:2] * y1_ref[...]
    x = x_ref[...] + gf_ref[0] * f
    if len(rest) == 2:
        x = _rms(x) * rest[0][...]
    o_ref[...] = x


def moe_combine(g, x, gf, y0, y1, wt, out_norm_w=None):
    extra_specs, extra = [], []
    if out_norm_w is not None:
        extra_specs, extra = [g.full((1, D_MODEL))], [out_norm_w.reshape(1, -1)]
    return pl.pallas_call(
        _combine_kernel,
        grid=g.grid,
        in_specs=[g.rows(D_MODEL), g.mod_spec(D_MODEL), g.rows(D_MODEL), g.rows(D_MODEL), g.rows(LANES)] + extra_specs,
        out_specs=g.rows(D_MODEL),
        out_shape=jax.ShapeDtypeStruct((g.M, D_MODEL), F32),
        compiler_params=_cparams("parallel", "parallel"),
        name="moe_combine",
    )(x, g.mod_array(gf), y0, y1, wt, *extra)


def moe_ffn(g, x, nw, sc, sh, gf, wr, rb, w1, w3, w2, out_norm_w=None):
    M = g.M
    h, logits = moe_router(g, x, nw, sc, sh, wr, rb)
    top_v, top_i = lax.top_k(logits[:, :NE], TOPK)
    top_w = jax.nn.softmax(top_v, axis=-1)
    tm = min(MOE_ROW_TILE, max(LANES, M // 4))
    tf = MOE_FF_TILE
    flat_e = top_i.reshape(-1)
    onehot = (flat_e[:, None] == jnp.arange(NE)[None, :]).astype(jnp.int32)
    rank = jnp.take_along_axis(jnp.cumsum(onehot, axis=0) - onehot, flat_e[:, None], axis=1)[:, 0]
    counts = jnp.sum(onehot, axis=0)
    padded = ((counts + tm - 1) // tm) * tm
    starts = jnp.cumsum(padded) - padded
    pos = starts[flat_e] + rank
    n_rows = M * TOPK + NE * tm
    row_token = jnp.zeros((n_rows,), jnp.int32).at[pos].set(jnp.arange(M * TOPK, dtype=jnp.int32) // TOPK,
                                                             unique_indices=True)
    tile_start = jnp.arange(n_rows // tm, dtype=jnp.int32) * tm
    ends = starts + padded
    tile_expert = jnp.minimum(jnp.sum((tile_start[:, None] >= ends[None, :]).astype(jnp.int32), axis=1), NE - 1)
    tile_valid = (tile_start < ends[NE - 1]).astype(jnp.int32)
    h_sorted = h.at[row_token].get(mode="promise_in_bounds")
    y_sorted = expert_ffn(tile_expert.astype(jnp.int32), tile_valid, h_sorted, w1, w3, w2, tm, tf)
    pos2 = pos.reshape(M, TOPK)
    y0 = y_sorted.at[pos2[:, 0]].get(mode="promise_in_bounds")
    y1 = y_sorted.at[pos2[:, 1]].get(mode="promise_in_bounds")
    wt = jnp.pad(top_w, ((0, 0), (0, LANES - TOPK)))
    return moe_combine(g, x, gf, y0, y1, wt, out_norm_w)


def _final_norm_kernel(x_ref, w_ref, o_ref):
    o_ref[...] = _rms(x_ref[...]) * w_ref[...]


def final_norm(g, x, w):
    return pl.pallas_call(
        _final_norm_kernel,
        grid=g.grid,
        in_specs=[g.rows(D_MODEL), g.full((1, D_MODEL))],
        out_specs=g.rows(D_MODEL),
        out_shape=jax.ShapeDtypeStruct((g.M, D_MODEL), F32),
        compiler_params=_cparams("parallel", "parallel"),
        name="final_norm",
    )(x, w.reshape(1, -1))


def _rot_half_cols(w):
    half = ROPE // 2
    return jnp.concatenate([-w[..., half:], w[..., :half]], axis=-1)


def _pack_layer(l, W):
    p = {}
    w_in = W["w_in"][l]
    o = 0
    q_c, kv_c, kr = w_in[:, 0:QL], w_in[:, QL:QL + KVL], w_in[:, QL + KVL:QL + KVL + ROPE]
    o = QL + KVL + ROPE
    rw = w_in[:, o:o + RWKV_IN]
    o += RWKV_IN
    z = w_in[:, o:o + SDIM]
    o += SDIM
    xbc = w_in[:, o:o + SCD]
    o += SCD
    dt = w_in[:, o:o + SHEADS]
    o += SHEADS
    gate = w_in[:, o:]
    seg_a = jnp.concatenate([q_c, kv_c, kr, dt, _rot_half_cols(kr),
                             jnp.zeros((D_MODEL, SEG_A - (QL + KVL + 2 * ROPE + SHEADS)), F32)], axis=1)
    mw = {}
    mw["w_in"] = jnp.concatenate([seg_a, rw, z, xbc, gate], axis=1)
    wq = W["mla_w_uq"][l].reshape(QL, H, NOPE + ROPE)
    pe = wq[:, :, NOPE:]
    mw["wuq"] = jnp.concatenate([wq[:, :, :NOPE].reshape(QL, H * NOPE), pe.reshape(QL, H * ROPE),
                                 _rot_half_cols(pe).reshape(QL, H * ROPE)], axis=1)
    mw["wuk"] = jnp.transpose(W["mla_w_uk"][l], (1, 2, 0))
    mw["wuv"] = jnp.transpose(W["mla_w_uv"][l], (1, 0, 2))
    mw["w2"], mw["a2"], mw["g2"] = W["rwkv_w2"][l], W["rwkv_a2"][l], W["rwkv_g2"][l]
    mw["w_branch"], mw["w_out"] = W["w_branch"][l], W["w_out"][l]
    if l % 2 == 0:
        mw["ffn"] = tuple(W[n][l // 2] for n in ("ffn_w1", "ffn_w3", "ffn_w2"))
    p["mw"] = mw
    p["mw16"] = jax.tree_util.tree_map(lambda a: a.astype(BF16), mw)
    p["q_norm"], p["kv_norm"] = W["mla_q_norm"][l], W["mla_kv_norm"][l]
    p["rwkv"] = dict(mu=W["rwkv_mu"][l], w0=W["rwkv_w0"][l], a0=W["rwkv_a0"][l], k_k=W["rwkv_k_k"][l],
                     k_a=W["rwkv_k_a"][l], ln_w=W["rwkv_ln_w"][l], ln_b=W["rwkv_ln_b"][l],
                     r_k=W["rwkv_r_k"][l].reshape(-1))
    lanes = jnp.arange(LANES)
    head_lane = (lanes >= DT_LANE) & (lanes < DT_LANE + SHEADS)
    idx = jnp.clip(lanes - DT_LANE, 0, SHEADS - 1)
    p["ssm"] = dict(conv_w=W["ssm_conv_w"][l], conv_b=W["ssm_conv_b"][l],
                    dtb128=jnp.where(head_lane, W["ssm_dt_bias"][l][idx], 0.0).reshape(1, LANES),
                    a128=jnp.where(head_lane, -jnp.exp(W["ssm_a_log"][l][idx]), 0.0).reshape(1, LANES),
                    dskip512=jnp.repeat(W["ssm_d"][l], SP).reshape(1, SDIM), ssm_norm=W["ssm_norm"][l])
    p["norm_attn"], p["norm_ffn"] = W["norm_attn"][l], W["norm_ffn"][l]
    p["w_ada"], p["b_ada"] = W["w_ada"][l], W["b_ada"][l]
    if l % 2 == 1:
        wr = jnp.pad(W["moe_router"][l // 2], ((0, 0), (0, LANES - NE)))
        p["router"] = (wr, jnp.pad(W["moe_router_b"][l // 2], (0, LANES - NE)).reshape(1, LANES))
        p["moe"] = tuple(W[n][l // 2].astype(BF16) for n in ("moe_w1", "moe_w3", "moe_w2"))
    return p


def _rope_tables(pos):
    half = ROPE // 2
    freq = ROPE_THETA ** (-jnp.arange(half, dtype=F32) / half)
    ang = pos.astype(F32)[:, None] * freq[None, :]
    cos = jnp.concatenate([jnp.cos(ang)] * 2, axis=-1)
    sin = jnp.concatenate([jnp.sin(ang)] * 2, axis=-1)
    return cos, sin, jnp.tile(cos, (1, H)), jnp.tile(sin, (1, H))


def _trunk(x3, c, pos, paged, shift0, wkv0, conv0, ssm0, layers, norm_final):
    B, S, _ = x3.shape
    M = B * S
    g = _Group(B, S, ROW_TILE)
    g_win = _Group(B, S, WIN_ROW_TILE)
    x = x3.reshape(M, D_MODEL)
    tabs = _rope_tables(pos if S > 1 else jnp.broadcast_to(pos, (M,)))
    ri = jnp.arange(RDIM)
    ones_bd = ((ri[:, None] // RHD) == (ri[None, :] // RHD)).astype(BF16)
    outs = [[] for _ in range(6)]
    for l, p in enumerate(layers):
        mw = p["mw"] if g.precise else p["mw16"]
        ada = ada_matmul(c, p["w_ada"], p["b_ada"])
        sh_a, sc_a, g_a, sh_f, sc_f, g_f = jnp.split(ada, 6, axis=-1)
        seg_a, rw, z, xbc, gate = win_project(g_win, x, p["norm_attn"], sc_a, sh_a, mw["w_in"])

        qc, kc, ckv, kpe = mla_prep(g, seg_a, tabs, p["q_norm"], p["kv_norm"], mw["wuq"], mw["wuk"])
        if paged is None:
            a_out = mla_attention(B, S, qc, kc, mw["wuv"])
        else:
            cache_ckv, cache_kpe_t, page_table = paged
            a_out = paged_attention(l, page_table, qc, kc, cache_ckv, cache_kpe_t, mw["wuv"])

        rw3 = rw.reshape(B, S, RWKV_IN)
        tc_len = min(RHD, S)
        nch = S // tc_len
        tpb, hpt = SCAN_TPB, SCAN_HPT
        as_rows = lambda a: a.reshape(S, B * RDIM)
        if S > 1:
            r_t, w_t, k_t, v_t, kk_t, b_t, gate_r, vt = rwkv_prep(g, rw, shift0[l], p["rwkv"], mw, ones_bd)
        else:
            r_t, w_t, k_t, v_t, kk_t, b_t, gate_r = rwkv_prep(g, rw, shift0[l], p["rwkv"], mw, ones_bd)
            vt = v_t.reshape(nch, tc_len, B, tpb, hpt, RHD).transpose(0, 2, 3, 5, 4, 1)
            vt = jnp.pad(vt, ((0, 0),) * 5 + ((0, RHD - tc_len),)).reshape(nch, B * tpb, RHD, SCAN_W)
        s0 = wkv0[l].reshape(B, tpb, hpt, RHD, RHD).transpose(0, 1, 3, 2, 4).reshape(B * tpb, RHD, SCAN_W)
        yt, s1 = rwkv_scan(B, S, as_rows(r_t), as_rows(w_t), as_rows(k_t), as_rows(kk_t), as_rows(b_t), vt, s0,
                           g.precise)
        ysub = min(SCAN_YSUB, tc_len)
        y = yt.reshape(nch, B, tpb, RHD, SCAN_W // LANES, hpt, SCAN_YSUB)[:, :, :, :, :tc_len // ysub, :, :ysub]
        y = y.transpose(1, 0, 4, 6, 2, 5, 3).reshape(M, RDIM)
        wkv1 = s1.reshape(B, tpb, RHD, hpt, RHD).transpose(0, 1, 3, 2, 4).reshape(B, tpb * hpt, RHD, RHD)
        r_out = rwkv_post(g, y, r_t, k_t, v_t, gate_r, p["rwkv"], ones_bd)
        shift1 = rw3[:, -1]

        ps = p["ssm"]
        if S > 1:
            tail0 = jnp.pad(conv0[l], ((0, 0), (8 - (SCONV - 1), 0), (0, 0)))
            s_out, h_t = ssd_prompt(B, S, xbc, z, seg_a, tail0, ps)
            ssm1 = jnp.swapaxes(h_t, 2, 3)
            conv1 = xbc.reshape(B, S, SCD)[:, S - (SCONV - 1):]
        else:
            c0, c1, c2 = conv0[l][:, 0], conv0[l][:, 1], conv0[l][:, 2]
            act, xdt, dec128 = ssm_step_pre(g, xbc, c0, c1, c2, seg_a, ps)
            xb = jnp.broadcast_to(xdt.reshape(B, SHEADS, SP, 1), (B, SHEADS, SP, SN))
            dec = jnp.broadcast_to(dec128[:, DT_LANE:DT_LANE + SHEADS].reshape(B, SHEADS, 1, 1), (B, SHEADS, 1, SN))
            ssm1, y_s = ssm_step(B, l, ssm0, xb, dec, act[:, SDIM:SDIM + SGROUPS * SN], act[:, SDIM + SGROUPS * SN:])
            s_out = ssm_step_post(g, y_s, act[:, :SDIM], z, ps)
            conv1 = jnp.concatenate([conv0[l][:, 1:], xbc[:, None, :]], axis=1)

        x = merge_branches(g, a_out, r_out, s_out, gate, x, g_a, mw["w_branch"], mw["w_out"])

        if l % 2 == 0:
            w1, w3, w2 = mw["ffn"]
            x = dense_ffn(g, x, p["norm_ffn"], sc_f, sh_f, g_f, w1, w3, w2, w1.shape[1] // 2)
        else:
            w1, w3, w2 = p["moe"]
            last = l == len(layers) - 1
            x = moe_ffn(g, x, p["norm_ffn"], sc_f, sh_f, g_f, *p["router"], w1, w3, w2,
                        out_norm_w=norm_final if last else None)
        for lst, v in zip(outs, (ckv.reshape(B, S, KVL), kpe.reshape(B, S, ROPE), shift1, wkv1, conv1, ssm1)):
            lst.append(v)
    if (len(layers) - 1) % 2 == 0:
        x = final_norm(g, x, norm_final)
    y = x.reshape(B, S, D_MODEL)
    return y, [jnp.stack(v) for v in outs]


def kernel(x_prompt, x_sample, cache_ckv, cache_kpe, state_rwkv_shift, state_rwkv_wkv, state_ssm_conv, state_ssm, page_table, c_prompt, c_sample, w_ada, b_ada, norm_attn, norm_ffn, norm_final, w_in, mla_q_norm, mla_w_uq, mla_kv_norm, mla_w_uk, mla_w_uv, rwkv_mu, rwkv_w0, rwkv_w2, rwkv_a0, rwkv_a2, rwkv_g2, rwkv_k_k, rwkv_k_a, rwkv_r_k, rwkv_ln_w, rwkv_ln_b, ssm_conv_w, ssm_conv_b, ssm_dt_bias, ssm_a_log, ssm_d, ssm_norm, w_branch, w_out, ffn_w1, ffn_w3, ffn_w2, moe_router, moe_router_b, moe_w1, moe_w3, moe_w2):
    W = dict(w_ada=w_ada, b_ada=b_ada, norm_attn=norm_attn, norm_ffn=norm_ffn, w_in=w_in, mla_q_norm=mla_q_norm,
             mla_w_uq=mla_w_uq, mla_kv_norm=mla_kv_norm, mla_w_uk=mla_w_uk, mla_w_uv=mla_w_uv, rwkv_mu=rwkv_mu,
             rwkv_w0=rwkv_w0, rwkv_w2=rwkv_w2, rwkv_a0=rwkv_a0, rwkv_a2=rwkv_a2, rwkv_g2=rwkv_g2, rwkv_k_k=rwkv_k_k,
             rwkv_k_a=rwkv_k_a, rwkv_r_k=rwkv_r_k, rwkv_ln_w=rwkv_ln_w, rwkv_ln_b=rwkv_ln_b, ssm_conv_w=ssm_conv_w,
             ssm_conv_b=ssm_conv_b, ssm_dt_bias=ssm_dt_bias, ssm_a_log=ssm_a_log, ssm_d=ssm_d, ssm_norm=ssm_norm,
             w_branch=w_branch, w_out=w_out, ffn_w1=ffn_w1, ffn_w3=ffn_w3, ffn_w2=ffn_w2, moe_router=moe_router,
             moe_router_b=moe_router_b, moe_w1=moe_w1, moe_w3=moe_w3, moe_w2=moe_w2)
    depth = w_in.shape[0]
    layers = [_pack_layer(l, W) for l in range(depth)]
    bp, sp, _ = x_prompt.shape
    bs, ss, _ = x_sample.shape
    dt = x_prompt.dtype
    y_prompt, (p_ckv, p_kpe, p_shift, p_wkv, p_conv, p_ssm) = _trunk(
        x_prompt, c_prompt, jnp.arange(sp, dtype=jnp.int32), None,
        jnp.zeros((depth, bp, RWKV_IN), dt), jnp.zeros((depth, bp, 8, RHD, RHD), dt),
        jnp.zeros((depth, bp, SCONV - 1, SCD), dt), jnp.zeros((depth, bp, SHEADS, SP, SN), dt), layers, norm_final)
    p_ckv = p_ckv.reshape(depth, bp * sp // PAGE, PAGE, KVL)
    p_kpe = p_kpe.reshape(depth, bp * sp // PAGE, PAGE, ROPE)
    past_len = page_table.shape[1] * PAGE
    pos_s = past_len + jnp.arange(ss, dtype=jnp.int32)
    y_sample, (s_ckv, s_kpe, s_shift, s_wkv, s_conv, s_ssm) = _trunk(
        x_sample, c_sample, pos_s, (cache_ckv, jnp.swapaxes(cache_kpe, 2, 3), page_table),
        state_rwkv_shift, state_rwkv_wkv, state_ssm_conv, state_ssm, layers, norm_final)
    return (y_prompt, y_sample, p_ckv, p_kpe, p_shift, p_wkv, p_conv, p_ssm,
            s_ckv, s_kpe, s_shift, s_wkv, s_conv, s_ssm)
```
